```python
import math
import jax, jax.numpy as jnp
from jax import lax
import numpy as np

D_MODEL = 1024
BATCH = 16
SEQ = 2048
DEPTH = 2

HEAD_DIM = 64
DIFF_HEADS = 4
DIFF_QK = 2 * HEAD_DIM
DIFF_V = 2 * HEAD_DIM
Q_BLOCK = 128
GDN_HEADS = 4
GDN_DK = 128
GDN_DV = 128
GDN_QKV = GDN_HEADS * (2 * GDN_DK + GDN_DV)
GDN_CONV = 5
GDN_CHUNK = 64
SWA_HEADS = 8
SWA_KV_HEADS = 2
SWA_WINDOW = 128
SWA_BLOCK = 128
SWA_SIDE_BLOCKS = SWA_WINDOW // SWA_BLOCK
N_BRANCHES = 3
BRANCH_WIDTH = 512
REL_BUCKETS = 32
REL_MAX_DIST = 128
REL_HEADS = DIFF_HEADS + SWA_HEADS
D_FF = 2816
FFN_CONV = 3
DEEPNORM_ALPHA = (2 * DEPTH) ** 0.25
DEEPNORM_BETA = (8 * DEPTH) ** -0.25
LN_EPS = 1e-5
RMS_EPS = 1e-6

COLS = (
    DIFF_HEADS * DIFF_QK,
    DIFF_HEADS * DIFF_QK,
    DIFF_HEADS * DIFF_V,
    GDN_QKV,
    GDN_HEADS * GDN_DV,
    2 * GDN_HEADS,
    2 * GDN_HEADS,
    SWA_HEADS * HEAD_DIM,
    SWA_KV_HEADS * HEAD_DIM,
    SWA_KV_HEADS * HEAD_DIM,
    N_BRANCHES * D_MODEL,
)
D_IN = sum(COLS)
SPLITS = tuple(sum(COLS[:i + 1]) for i in range(len(COLS) - 1))

kernel_name = 'hybrid_gated_diff_gdn_swa_encoder'


def layer_norm(x, g, b):
    xf = x.astype(jnp.float32)
    mu = jnp.mean(xf, axis=-1, keepdims=True)
    var = jnp.mean(jnp.square(xf - mu), axis=-1, keepdims=True)
    return ((xf - mu) * lax.rsqrt(var + LN_EPS) * g + b).astype(x.dtype)


def rms_norm(x, g):
    xf = x.astype(jnp.float32)
    return (xf * lax.rsqrt(jnp.mean(xf * xf, axis=-1, keepdims=True) + RMS_EPS) * g).astype(x.dtype)


def l2_normalize(x):
    return x * lax.rsqrt(jnp.sum(x * x, axis=-1, keepdims=True) + RMS_EPS)


def dwconv_centred(x, w):
    K, C = w.shape
    return lax.conv_general_dilated(
        x, w[:, None, :].astype(x.dtype), window_strides=(1,),
        padding=((K // 2, K // 2),), dimension_numbers=('NWC', 'WIO', 'NWC'),
        feature_group_count=C)


def t5_bucket(rel):
    nb = REL_BUCKETS // 2
    ret = jnp.where(rel > 0, nb, 0)
    n = jnp.abs(rel)
    max_exact = nb // 2
    large = max_exact + (jnp.log(jnp.maximum(n, 1).astype(jnp.float32) / max_exact)
                         / math.log(REL_MAX_DIST / max_exact) * (nb - max_exact)).astype(jnp.int32)
    large = jnp.minimum(large, nb - 1)
    return ret + jnp.where(n < max_exact, n, large)


def diff_attention(q, k, v, lam, bias_table):
    B, T, H, _, dh = q.shape
    nb = T // Q_BLOCK
    scale = dh ** -0.5
    qb = q.reshape(B, nb, Q_BLOCK, H, 2, dh).transpose(1, 0, 3, 4, 2, 5)
    kt = k.transpose(0, 2, 3, 1, 4)
    vt = v.transpose(0, 2, 1, 3)
    key_pos = jnp.arange(T)

    def block(args):
        qi, start = args
        s = jnp.einsum('bhcqd,bhckd->bhcqk', qi, kt).astype(jnp.float32) * scale
        rel = key_pos[None, :] - (start + jnp.arange(Q_BLOCK))[:, None]
        bias = bias_table[t5_bucket(rel)].astype(jnp.float32).transpose(2, 0, 1)
        p = jax.nn.softmax(s + bias[None, :, None], axis=-1)
        a = p[:, :, 0] - lam * p[:, :, 1]
        return jnp.einsum('bhqk,bhkd->bhqd', a.astype(vt.dtype), vt)

    starts = jnp.arange(nb) * Q_BLOCK
    o = lax.map(block, (qb, starts))
    return o.transpose(1, 0, 3, 2, 4).reshape(B, T, H, -1)


def diff_mixer(qa, ka, va, lam_vecs, subln, bias_table, layer_idx):
    B, T, _ = qa.shape
    q = qa.reshape(B, T, DIFF_HEADS, 2, HEAD_DIM)
    k = ka.reshape(B, T, DIFF_HEADS, 2, HEAD_DIM)
    v = va.reshape(B, T, DIFF_HEADS, DIFF_V)
    lam_init = 0.8 - 0.6 * math.exp(-0.3 * layer_idx)
    lv = lam_vecs.astype(jnp.float32)
    lam = jnp.exp(jnp.dot(lv[0], lv[1])) - jnp.exp(jnp.dot(lv[2], lv[3])) + lam_init
    o = diff_attention(q, k, v, lam, bias_table)
    o = rms_norm(o, subln) * (1.0 - lam_init)
    return o.reshape(B, T, DIFF_HEADS * DIFF_V)


def gated_delta_chunked(q, k, v, g, beta):
    B, T, H, dk = q.shape
    dv = v.shape[-1]
    C = GDN_CHUNK
    N = T // C

    def chunks(t):
        return jnp.moveaxis(t.reshape((B, N, C, H) + t.shape[3:]), 3, 1)

    q = chunks(q * dk ** -0.5)
    k = chunks(k)
    v = chunks(v)
    gc = jnp.cumsum(chunks(g), axis=-1)
    beta = chunks(beta)
    kb = k * beta[..., None]
    idx = jnp.arange(C)
    incl = idx[:, None] >= idx[None, :]
    strict = idx[:, None] > idx[None, :]
    decay = jnp.exp(jnp.where(incl, gc[..., :, None] - gc[..., None, :], -jnp.inf))
    kk = jnp.einsum('bhncd,bhnsd->bhncs', kb, k) * decay
    a_mat = jnp.where(strict, kk, 0.0) + jnp.eye(C, dtype=kk.dtype)
    rhs = jnp.concatenate([v * beta[..., None], kb * jnp.exp(gc)[..., None]], axis=-1)
    sol = lax.linalg.triangular_solve(a_mat, rhs, left_side=True, lower=True, unit_diagonal=True)
    u, w = sol[..., :dv], sol[..., dv:]
    qk = jnp.where(incl, jnp.einsum('bhncd,bhnsd->bhncs', q, k) * decay, 0.0)

    def step(S, inp):
        q_i, k_i, u_i, w_i, g_i, qk_i = inp
        v_new = u_i - jnp.einsum('bhcd,bhde->bhce', w_i, S)
        o_i = (jnp.einsum('bhcd,bhde->bhce', q_i * jnp.exp(g_i)[..., None], S)
               + jnp.einsum('bhcs,bhse->bhce', qk_i, v_new))
        g_last = g_i[..., -1:]
        S = (S * jnp.exp(g_last)[..., None]
             + jnp.einsum('bhcd,bhce->bhde', k_i * jnp.exp(g_last - g_i)[..., None], v_new))
        return S, o_i

    xs = tuple(jnp.moveaxis(t, 2, 0) for t in (q, k, u, w, gc, qk))
    S0 = jnp.zeros((B, H, dk, dv), q.dtype)
    _, o = lax.scan(step, S0, xs)
    return jnp.moveaxis(o, 0, 2).transpose(0, 2, 3, 1, 4).reshape(B, T, H, dv)


def gdn_mixer(qkv, gate, a_in, b_in, conv_w, a_log, dt_bias, norm_w):
    B, T, _ = qkv.shape
    h = jax.nn.silu(dwconv_centred(qkv, conv_w)).astype(jnp.float32)
    q, k, v = jnp.split(h, [GDN_HEADS * GDN_DK, 2 * GDN_HEADS * GDN_DK], axis=-1)
    q = l2_normalize(q.reshape(B, T, GDN_HEADS, GDN_DK))
    k = l2_normalize(k.reshape(B, T, GDN_HEADS, GDN_DK))
    v = v.reshape(B, T, GDN_HEADS, GDN_DV)
    a = a_in.astype(jnp.float32).reshape(B, T, 2, GDN_HEADS)
    b = b_in.astype(jnp.float32).reshape(B, T, 2, GDN_HEADS)
    g = -jnp.exp(a_log.astype(jnp.float32)) * jax.nn.softplus(a + dt_bias.astype(jnp.float32))
    beta = jax.nn.sigmoid(b)
    o_fwd = gated_delta_chunked(q, k, v, g[:, :, 0], beta[:, :, 0])
    flip = lambda t: jnp.flip(t, axis=1)
    o_bwd = flip(gated_delta_chunked(flip(q), flip(k), flip(v), flip(g[:, :, 1]), flip(beta[:, :, 1])))
    o = rms_norm(o_fwd + o_bwd, norm_w) * jax.nn.silu(
        gate.astype(jnp.float32).reshape(B, T, GDN_HEADS, GDN_DV))
    return o.reshape(B, T, GDN_HEADS * GDN_DV).astype(qkv.dtype)


def window_gqa(q, k, v, sink, bias_table):
    B, T, Hq, dh = q.shape
    G = k.shape[2]
    R = Hq // G
    QB = SWA_BLOCK
    W = SWA_WINDOW
    side = SWA_SIDE_BLOCKS
    nb = T // QB
    KB = QB * (2 * side + 1)
    pad = ((0, 0), (W, W), (0, 0), (0, 0))
    kp = jnp.pad(k, pad).reshape(B, nb + 2 * side, QB, G, dh)
    vp = jnp.pad(v, pad).reshape(B, nb + 2 * side, QB, G, dh)
    kw = jnp.concatenate([kp[:, i:i + nb] for i in range(2 * side + 1)], axis=2)
    vw = jnp.concatenate([vp[:, i:i + nb] for i in range(2 * side + 1)], axis=2)
    qb = q.reshape(B, nb, QB, G, R, dh)
    s = jnp.einsum('bnqgrd,bnkgd->bngrqk', qb, kw).astype(jnp.float32) * dh ** -0.5
    rel = jnp.arange(KB)[None, :] - W - jnp.arange(QB)[:, None]
    bias = bias_table[t5_bucket(rel)].astype(jnp.float32).transpose(2, 0, 1).reshape(G, R, QB, KB)
    kabs = (jnp.arange(nb) * QB)[:, None] - W + jnp.arange(KB)[None, :]
    valid = (jnp.abs(rel) <= W)[None] & ((kabs >= 0) & (kabs < T))[:, None, :]
    s = jnp.where(valid[None, :, None, None], s + bias[None, None], -jnp.inf)
    sink_f = sink.astype(jnp.float32).reshape(G, R)[None, None, :, :, None, None]
    m = jnp.maximum(jnp.max(s, axis=-1, keepdims=True), sink_f)
    e = jnp.exp(s - m)
    p = e / (jnp.sum(e, axis=-1, keepdims=True) + jnp.exp(sink_f - m))
    o = jnp.einsum('bngrqk,bnkgd->bnqgrd', p.astype(vw.dtype), vw)
    return o.reshape(B, T, Hq * dh)


def hybrid_mixer(x, rel_bias, w_in, diff_lambda, diff_subln, gdn_conv, gdn_a_log, gdn_dt_bias,
                 gdn_norm, swa_sink, w_branch, w_out, layer_idx):
    B, T, D = x.shape
    z = x @ w_in
    aq, ak, av, bqkv, bgate, ba, bb, cq, ck, cv, gz = jnp.split(z, SPLITS, axis=-1)
    y_a = diff_mixer(aq, ak, av, diff_lambda, diff_subln, rel_bias[:, :DIFF_HEADS], layer_idx)
    y_b = gdn_mixer(bqkv, bgate, ba, bb, gdn_conv, gdn_a_log, gdn_dt_bias, gdn_norm)
    y_c = window_gqa(cq.reshape(B, T, SWA_HEADS, HEAD_DIM),
                     ck.reshape(B, T, SWA_KV_HEADS, HEAD_DIM),
                     cv.reshape(B, T, SWA_KV_HEADS, HEAD_DIM),
                     swa_sink, rel_bias[:, DIFF_HEADS:])
    ys = jnp.stack([y_a, y_b, y_c], axis=2)
    proj = jnp.einsum('btnc,ncd->btnd', ys, w_branch)
    gates = jax.nn.sigmoid(gz.reshape(B, T, N_BRANCHES, D))
    merged = jnp.sum(gates * proj, axis=2)
    return merged @ w_out


def conv_glu_ffn(x, w_up, conv_w, conv_b, w_down):
    gate, up = jnp.split(x @ w_up, 2, axis=-1)
    gate = dwconv_centred(gate, conv_w) + conv_b
    return (jax.nn.silu(gate) * up) @ w_down


def setup_inputs(seed: int = 0) -> dict:
    key = jax.random.key(seed)
    ks = jax.random.split(key, 24)
    L = DEPTH

    def nrm(k, shape, scale):
        return jax.random.normal(k, shape, jnp.float32) * scale

    x = nrm(ks[0], (BATCH, SEQ, D_MODEL), 1.0)
    rel_bias = nrm(ks[1], (REL_BUCKETS, REL_HEADS), 0.5)
    w_in = nrm(ks[2], (L, D_MODEL, D_IN), D_MODEL ** -0.5)
    diff_lambda = nrm(ks[3], (L, 4, HEAD_DIM), 0.1)
    diff_subln = 1.0 + nrm(ks[4], (L, DIFF_V), 0.02)
    gdn_conv = nrm(ks[5], (L, GDN_CONV, GDN_QKV), GDN_CONV ** -0.5)
    gdn_a_log = jnp.log(jax.random.uniform(ks[6], (L, 2, GDN_HEADS), jnp.float32, 1.0, 16.0))
    dt = jnp.exp(jax.random.uniform(ks[7], (L, 2, GDN_HEADS), jnp.float32,
                                    math.log(1e-3), math.log(1e-1)))
    gdn_dt_bias = dt + jnp.log(-jnp.expm1(-dt))
    gdn_norm = 1.0 + nrm(ks[8], (L, GDN_DV), 0.02)
    swa_sink = nrm(ks[9], (L, SWA_HEADS), 0.5)
    w_branch = nrm(ks[10], (L, N_BRANCHES, BRANCH_WIDTH, D_MODEL), BRANCH_WIDTH ** -0.5)
    w_out = nrm(ks[11], (L, D_MODEL, D_MODEL), D_MODEL ** -0.5 * DEEPNORM_BETA)
    ln1_g = 1.0 + nrm(ks[12], (L, D_MODEL), 0.02)
    ln1_b = nrm(ks[13], (L, D_MODEL), 0.02)
    ffn_up = nrm(ks[14], (L, D_MODEL, 2 * D_FF), D_MODEL ** -0.5)
    ffn_conv = nrm(ks[15], (L, FFN_CONV, D_FF), FFN_CONV ** -0.5)
    ffn_conv_b = nrm(ks[16], (L, D_FF), 0.02)
    ffn_down = nrm(ks[17], (L, D_FF, D_MODEL), D_FF ** -0.5 * DEEPNORM_BETA)
    ln2_g = 1.0 + nrm(ks[18], (L, D_MODEL), 0.02)
    ln2_b = nrm(ks[19], (L, D_MODEL), 0.02)
    return {'x': x, 'rel_bias': rel_bias, 'w_in': w_in, 'diff_lambda': diff_lambda,
            'diff_subln': diff_subln, 'gdn_conv': gdn_conv, 'gdn_a_log': gdn_a_log,
            'gdn_dt_bias': gdn_dt_bias, 'gdn_norm': gdn_norm, 'swa_sink': swa_sink,
            'w_branch': w_branch, 'w_out': w_out, 'ln1_g': ln1_g, 'ln1_b': ln1_b,
            'ffn_up': ffn_up, 'ffn_conv': ffn_conv, 'ffn_conv_b': ffn_conv_b,
            'ffn_down': ffn_down, 'ln2_g': ln2_g, 'ln2_b': ln2_b}


def reference(x, rel_bias, w_in, diff_lambda, diff_subln, gdn_conv, gdn_a_log, gdn_dt_bias,
              gdn_norm, swa_sink, w_branch, w_out, ln1_g, ln1_b, ffn_up, ffn_conv, ffn_conv_b,
              ffn_down, ln2_g, ln2_b):
    for l in range(DEPTH):
        h = hybrid_mixer(x, rel_bias, w_in[l], diff_lambda[l], diff_subln[l], gdn_conv[l],
                         gdn_a_log[l], gdn_dt_bias[l], gdn_norm[l], swa_sink[l], w_branch[l],
                         w_out[l], l)
        x = layer_norm(DEEPNORM_ALPHA * x + h, ln1_g[l], ln1_b[l])
        f = conv_glu_ffn(x, ffn_up[l], ffn_conv[l], ffn_conv_b[l], ffn_down[l])
        x = layer_norm(DEEPNORM_ALPHA * x + f, ln2_g[l], ln2_b[l])
    return x
```

```python
import functools
import math

import jax
import jax.numpy as jnp
from jax import lax
from jax.experimental import pallas as pl
from jax.experimental.pallas import tpu as pltpu

F32 = jnp.float32
BF16 = jnp.bfloat16

D_MODEL = 1024
DEPTH = 2
HEAD_DIM = 64
DIFF_HEADS = 4
DIFF_V = 2 * HEAD_DIM
Q_BLOCK = 128
GDN_HEADS = 4
GDN_DK = 128
GDN_DV = 128
GDN_QKV = GDN_HEADS * (2 * GDN_DK + GDN_DV)
GDN_CONV = 5
GDN_CHUNK = 64
SWA_HEADS = 8
SWA_KV_HEADS = 2
SWA_WINDOW = 128
SWA_BLOCK = 128
BRANCH_WIDTH = 512
N_BRANCHES = 3
REL_BUCKETS = 32
REL_MAX_DIST = 128
D_FF = 2816
FFN_CONV = 3
DEEPNORM_ALPHA = (2 * DEPTH) ** 0.25
LN_EPS = 1e-5
RMS_EPS = 1e-6

V7X_LANES = 128
V7X_SUBLANES = 8
V7X_VMEM_BYTES = 64 * 1024 * 1024
VMEM_LIMIT = V7X_VMEM_BYTES * 7 // 8

NEG_BIG = -1e30

ZA_AQ, ZA_AK, ZA_AV = 0, 512, 1024
ZA_CQ = 1536
ZA_CK = 2048
ZA_CV = 2304
ZA_COLS = 2560
ZB_GZ = 0
ZB_QKV = 3072
ZB_GATE = 4608
ZB_AB = 5120
ZB_COLS = 5376

GDN_GROUP = 256
GDN_CHAINS = 2 * GDN_HEADS


def _cparams(n_grid):
    return pltpu.CompilerParams(dimension_semantics=("arbitrary",) * n_grid, vmem_limit_bytes=VMEM_LIMIT)


def _t5_bucket(rel):
    nb = REL_BUCKETS // 2
    ret = jnp.where(rel > 0, nb, 0)
    n = jnp.abs(rel)
    max_exact = nb // 2
    large = max_exact + (jnp.log(jnp.maximum(n, 1).astype(jnp.float32) / max_exact)
                         / math.log(REL_MAX_DIST / max_exact) * (nb - max_exact)).astype(jnp.int32)
    large = jnp.minimum(large, nb - 1)
    return ret + jnp.where(n < max_exact, n, large)


def _diff_bias_tiles(rel_bias, seq):
    nq = seq // Q_BLOCK
    d = jnp.arange(2 * nq - 1) - (nq - 1)
    r = jnp.arange(Q_BLOCK)
    rel = d[:, None, None] * Q_BLOCK + r[None, None, :] - r[None, :, None]
    tb = rel_bias[:, :DIFF_HEADS][_t5_bucket(rel)].astype(F32)
    return tb.transpose(3, 0, 1, 2)


def _swa_bias_tiles(rel_bias):
    kb = 3 * SWA_BLOCK
    rel = jnp.arange(kb)[None, :] - SWA_WINDOW - jnp.arange(SWA_BLOCK)[:, None]
    bias = rel_bias[:, DIFF_HEADS:][_t5_bucket(rel)].astype(F32).transpose(2, 0, 1)
    return jnp.where((jnp.abs(rel) <= SWA_WINDOW)[None], bias, NEG_BIG)


def _matmul_kernel(x_ref, w_ref, o_ref):
    o_ref[...] = jnp.dot(x_ref[...].astype(BF16), w_ref[...],
                         preferred_element_type=F32).astype(o_ref.dtype)


def _matmul(x, w, out_dtype, tm, tn, name):
    m, k = x.shape
    n = w.shape[1]
    return pl.pallas_call(
        _matmul_kernel,
        grid=(n // tn, m // tm),
        in_specs=[pl.BlockSpec((tm, k), lambda j, i: (i, 0)),
                  pl.BlockSpec((k, tn), lambda j, i: (0, j))],
        out_specs=pl.BlockSpec((tm, tn), lambda j, i: (i, j)),
        out_shape=jax.ShapeDtypeStruct((m, n), out_dtype),
        compiler_params=_cparams(2),
        name=name,
    )(x, w)


def _diff_attn_kernel(q_ref, k_ref, v_ref, tb_ref, lam_ref, subln_ref, o_ref, s_scr, *, nq, lam_init):
    qi = pl.program_id(2)
    q = q_ref[...]
    lane = lax.broadcasted_iota(jnp.int32, q.shape, 1)
    zero = jnp.zeros_like(q)
    qz = jnp.concatenate([jnp.where(lane < HEAD_DIM, q, zero),
                          jnp.where(lane >= HEAD_DIM, q, zero)], axis=0)
    scale = HEAD_DIM ** -0.5
    for kj in range(nq):
        kblk = k_ref[kj * Q_BLOCK:(kj + 1) * Q_BLOCK, :]
        s = lax.dot_general(qz, kblk, (((1,), (1,)), ((), ())), preferred_element_type=F32) * scale
        b = tb_ref[0, nq - 1 - qi + kj]
        s_scr[0:Q_BLOCK, kj * Q_BLOCK:(kj + 1) * Q_BLOCK] = s[:Q_BLOCK] + b
        s_scr[Q_BLOCK:2 * Q_BLOCK, kj * Q_BLOCK:(kj + 1) * Q_BLOCK] = s[Q_BLOCK:] + b
    s = s_scr[...]
    m = jnp.max(s, axis=-1, keepdims=True)
    e = jnp.exp(s - m)
    l = jnp.sum(e, axis=-1, keepdims=True)
    pv = jnp.dot(e.astype(BF16), v_ref[...], preferred_element_type=F32) / l
    lv = lam_ref[...]
    lam = (jnp.exp(jnp.sum(lv[0:1] * lv[1:2], axis=-1, keepdims=True))
           - jnp.exp(jnp.sum(lv[2:3] * lv[3:4], axis=-1, keepdims=True)) + lam_init)
    o = pv[:Q_BLOCK] - lam * pv[Q_BLOCK:]
    o = o * lax.rsqrt(jnp.mean(o * o, axis=-1, keepdims=True) + RMS_EPS) * subln_ref[...]
    o_ref[...] = (o * (1.0 - lam_init)).astype(o_ref.dtype)


def _diff_attention(za, tb, lam_vecs, subln, batch, seq, layer_idx):
    nq = seq // Q_BLOCK
    lam_init = 0.8 - 0.6 * math.exp(-0.3 * layer_idx)
    kern = functools.partial(_diff_attn_kernel, nq=nq, lam_init=lam_init)
    qcol, kcol, vcol = ZA_AQ // 128, ZA_AK // 128, ZA_AV // 128
    return pl.pallas_call(
        kern,
        grid=(batch, DIFF_HEADS, nq),
        in_specs=[
            pl.BlockSpec((Q_BLOCK, 128), lambda b, h, i: (b * nq + i, qcol + h)),
            pl.BlockSpec((seq, 128), lambda b, h, i: (b, kcol + h)),
            pl.BlockSpec((seq, 128), lambda b, h, i: (b, vcol + h)),
            pl.BlockSpec((1, 2 * nq - 1, Q_BLOCK, Q_BLOCK), lambda b, h, i: (h, 0, 0, 0)),
            pl.BlockSpec((4, HEAD_DIM), lambda b, h, i: (0, 0)),
            pl.BlockSpec((1, DIFF_V), lambda b, h, i: (0, 0)),
        ],
        out_specs=pl.BlockSpec((Q_BLOCK, 128), lambda b, h, i: (b * nq + i, h)),
        out_shape=jax.ShapeDtypeStruct((batch * seq, BRANCH_WIDTH), BF16),
        scratch_shapes=[pltpu.VMEM((2 * Q_BLOCK, seq), F32)],
        compiler_params=_cparams(3),
        name="diff_attention",
    )(za, za, za, tb, lam_vecs.astype(F32), subln.astype(F32).reshape(1, DIFF_V))


def _swa_kernel(q_ref, kp_ref, kc_ref, kn_ref, vp_ref, vc_ref, vn_ref, bias_ref, sink_ref, o_ref, *, nb):
    n = pl.program_id(2)
    q = q_ref[...]
    k = jnp.concatenate([kp_ref[...], kc_ref[...], kn_ref[...]], axis=0)
    v = jnp.concatenate([vp_ref[...], vc_ref[...], vn_ref[...]], axis=0)
    col = lax.broadcasted_iota(jnp.int32, (SWA_BLOCK, 3 * SWA_BLOCK), 1)
    outside = ((col < SWA_BLOCK) & (n == 0)) | ((col >= 2 * SWA_BLOCK) & (n == nb - 1))
    lane = lax.broadcasted_iota(jnp.int32, (SWA_BLOCK, 128), 1)
    low = lane < HEAD_DIM
    zero = jnp.zeros((SWA_BLOCK, 128), q.dtype)
    scale = HEAD_DIM ** -0.5
    rep = SWA_HEADS // SWA_KV_HEADS
    outs = []
    for j in range(rep):
        qc = q[:, (j // 2) * 128:(j // 2 + 1) * 128]
        qm = jnp.where(low if j % 2 == 0 else jnp.logical_not(low), qc, zero)
        s = lax.dot_general(qm, k, (((1,), (1,)), ((), ())), preferred_element_type=F32) * scale
        s = jnp.where(outside, NEG_BIG, s + bias_ref[j])
        sink = sink_ref[j]
        m = jnp.maximum(jnp.max(s, axis=-1, keepdims=True), sink[:, 0:1])
        e = jnp.exp(s - m)
        den = jnp.sum(e, axis=-1, keepdims=True) + jnp.exp(sink[:, 0:1] - m)
        outs.append(jnp.dot(e.astype(BF16), v, preferred_element_type=F32) / den)
    o = jnp.concatenate([jnp.where(low, outs[0], outs[1]), jnp.where(low, outs[2], outs[3])], axis=1)
    o_ref[...] = o.astype(o_ref.dtype)


def _swa_attention(za, bias, sink, batch, seq):
    nb = seq // SWA_BLOCK
    rep = SWA_HEADS // SWA_KV_HEADS
    kern = functools.partial(_swa_kernel, nb=nb)
    qcol, kcol, vcol = ZA_CQ // 256, ZA_CK // 128, ZA_CV // 128
    prev = lambda b, g, n: (b * nb + jnp.maximum(n - 1, 0))
    nxt = lambda b, g, n: (b * nb + jnp.minimum(n + 1, nb - 1))
    blk = (SWA_BLOCK, 128)
    sink_b = jnp.broadcast_to(sink.astype(F32).reshape(SWA_HEADS, 1, 1), (SWA_HEADS, 1, 128))
    return pl.pallas_call(
        kern,
        grid=(batch, SWA_KV_HEADS, nb),
        in_specs=[
            pl.BlockSpec((SWA_BLOCK, 256), lambda b, g, n: (b * nb + n, qcol + g)),
            pl.BlockSpec(blk, lambda b, g, n: (prev(b, g, n), kcol + g)),
            pl.BlockSpec(blk, lambda b, g, n: (b * nb + n, kcol + g)),
            pl.BlockSpec(blk, lambda b, g, n: (nxt(b, g, n), kcol + g)),
            pl.BlockSpec(blk, lambda b, g, n: (prev(b, g, n), vcol + g)),
            pl.BlockSpec(blk, lambda b, g, n: (b * nb + n, vcol + g)),
            pl.BlockSpec(blk, lambda b, g, n: (nxt(b, g, n), vcol + g)),
            pl.BlockSpec((rep, SWA_BLOCK, 3 * SWA_BLOCK), lambda b, g, n: (g, 0, 0)),
            pl.BlockSpec((rep, 1, 128), lambda b, g, n: (g, 0, 0)),
        ],
        out_specs=pl.BlockSpec((SWA_BLOCK, 256), lambda b, g, n: (b * nb + n, g)),
        out_shape=jax.ShapeDtypeStruct((batch * seq, BRANCH_WIDTH), BF16),
        compiler_params=_cparams(3),
        name="swa_attention",
    )(za, za, za, za, za, za, za, bias, sink_b)


def _shift_rows(x, d):
    rows = x.shape[0]
    t = lax.broadcasted_iota(jnp.int32, x.shape, 0)
    rolled = pltpu.roll(x, (-d) % rows, 0)
    return jnp.where((t + d >= 0) & (t + d < rows), rolled, 0.0)


def _gdn_conv_kernel(x_ref, w_ref, o_ref):
    j = pl.program_id(1)
    x = x_ref[...]
    w = w_ref[...]
    half = GDN_CONV // 2
    acc = x * w[half:half + 1]
    for tap in range(GDN_CONV):
        if tap != half:
            acc = acc + _shift_rows(x, tap - half) * w[tap:tap + 1]
    h = acc * jax.nn.sigmoid(acc)
    inv = lax.rsqrt(jnp.sum(h * h, axis=-1, keepdims=True) + RMS_EPS)
    factor = jnp.where(j < 2 * GDN_HEADS, inv, 1.0) * jnp.where(j < GDN_HEADS, GDN_DK ** -0.5, 1.0)
    o_ref[...] = h * factor


def _gdn_conv(zb, conv_w, batch, seq):
    c0 = ZB_QKV // 128
    return pl.pallas_call(
        _gdn_conv_kernel,
        grid=(batch, GDN_QKV // 128),
        in_specs=[pl.BlockSpec((seq, 128), lambda b, j: (b, c0 + j)),
                  pl.BlockSpec((GDN_CONV, 128), lambda b, j: (0, j))],
        out_specs=pl.BlockSpec((seq, 128), lambda b, j: (b, j)),
        out_shape=jax.ShapeDtypeStruct((batch * seq, GDN_QKV), F32),
        compiler_params=_cparams(2),
        name="gdn_conv",
    )(zb, conv_w.astype(F32))


def _gdn_gate_kernel(ab_ref, alog_ref, dtb_ref, gc_ref, gd_ref, gl_ref, beta_ref):
    ab = ab_ref[...]
    a = ab[0:GDN_CHAINS]
    seq = a.shape[1]
    xs = a + dtb_ref[...]
    softplus = jnp.maximum(xs, 0.0) + jnp.log1p(jnp.exp(-jnp.abs(xs)))
    g = -jnp.exp(alog_ref[...]) * softplus
    beta_ref[...] = jax.nn.sigmoid(ab[GDN_CHAINS:2 * GDN_CHAINS])
    pos = lax.broadcasted_iota(jnp.int32, g.shape, 1) % GDN_CHUNK
    cf = g
    cr = g
    s = 1
    while s < GDN_CHUNK:
        cf = cf + jnp.where(pos >= s, pltpu.roll(cf, s, 1), 0.0)
        cr = cr + jnp.where(pos < GDN_CHUNK - s, pltpu.roll(cr, seq - s, 1), 0.0)
        s *= 2
    fwd = lax.broadcasted_iota(jnp.int32, g.shape, 0) < GDN_HEADS
    gc_ref[...] = jnp.where(fwd, cf, cr)
    gd_ref[...] = jnp.where(fwd, cr, cf) - g
    gl_ref[...] = cf + cr - g


def _gdn_gates(ab_t, a_log, dt_bias, batch, seq):
    row = pl.BlockSpec((GDN_CHAINS, seq), lambda b: (0, b))
    shp = jax.ShapeDtypeStruct((GDN_CHAINS, batch * seq), F32)
    par = pl.BlockSpec((GDN_CHAINS, 1), lambda b: (0, 0))
    return pl.pallas_call(
        _gdn_gate_kernel,
        grid=(batch,),
        in_specs=[pl.BlockSpec((2 * GDN_CHAINS, seq), lambda b: (0, b)), par, par],
        out_specs=[row, row, row, row],
        out_shape=[shp, shp, shp, shp],
        compiler_params=_cparams(1),
        name="gdn_gates",
    )(ab_t, a_log.astype(F32).reshape(GDN_CHAINS, 1), dt_bias.astype(F32).reshape(GDN_CHAINS, 1))


def _hdot(a, b):
    return jnp.dot(a, b, precision=lax.Precision.HIGHEST, preferred_element_type=F32)


def _gdn_local_kernel(qkv_ref, gcc_ref, gdc_ref, betac_ref, gcr_ref,
                      u_ref, w_ref, qg_ref, qk_ref, kgt_ref):
    g = GDN_GROUP
    c = GDN_CHUNK
    ri = lax.broadcasted_iota(jnp.int32, (g, g), 0)
    ci = lax.broadcasted_iota(jnp.int32, (g, g), 1)
    same = (ri // c) == (ci // c)
    ahead = jnp.where(same, ri - ci, -g)
    behind = jnp.where(same, ci - ri, -g)
    eye_s = (lax.broadcasted_iota(jnp.int32, (c, g), 0)
             == lax.broadcasted_iota(jnp.int32, (c, g), 1) % c).astype(F32)

    def block_diag(xs):
        return jnp.where(same, jnp.concatenate([xs] * (g // c), axis=0), 0.0)

    def row_blocks_sum(xd):
        out = xd[0:c]
        for i in range(1, g // c):
            out = out + xd[i * c:(i + 1) * c]
        return out

    for h in range(GDN_HEADS):
        q = qkv_ref[:, h * GDN_DK:(h + 1) * GDN_DK]
        k = qkv_ref[:, (GDN_HEADS + h) * GDN_DK:(GDN_HEADS + h + 1) * GDN_DK]
        v = qkv_ref[:, (2 * GDN_HEADS + h) * GDN_DK:(2 * GDN_HEADS + h + 1) * GDN_DK]
        kbf = k.astype(BF16)
        qk_raw = lax.dot_general(q.astype(BF16), kbf, (((1,), (1,)), ((), ())), preferred_element_type=F32)
        for d in range(2):
            ch = d * GDN_HEADS + h
            gcc = gcc_ref[:, ch:ch + 1]
            gdc = gdc_ref[:, ch:ch + 1]
            beta = betac_ref[:, ch:ch + 1]
            gcr = gcr_ref[ch:ch + 1, :]
            order = ahead if d == 0 else behind
            kb = k * beta
            kk = lax.dot_general(kb.astype(BF16), kbf, (((1,), (1,)), ((), ())), preferred_element_type=F32)
            decay = jnp.exp(jnp.where(order >= 0, gcc - gcr, NEG_BIG))
            xd = -jnp.where(order > 0, kk * decay, 0.0)
            qkm = qk_raw * decay
            xs = row_blocks_sum(xd)
            p = eye_s + xs
            xs = _hdot(xs, xd)
            for _ in range(4):
                r = _hdot(jnp.concatenate([p, xs], axis=0), block_diag(xs))
                p = p + r[0:c]
                xs = r[c:2 * c]
            p = p + _hdot(p, block_diag(xs))
            rhs = jnp.concatenate([v * beta, kb * jnp.exp(gcc)], axis=1)
            sol = _hdot(block_diag(p), rhs)
            lo, hi = ch * GDN_DK, (ch + 1) * GDN_DK
            u_ref[:, lo:hi] = sol[:, 0:GDN_DV]
            w_ref[:, lo:hi] = sol[:, GDN_DV:].astype(w_ref.dtype)
            qg_ref[:, lo:hi] = (q * jnp.exp(gcc)).astype(qg_ref.dtype)
            qk_ref[:, lo:hi] = (qkm[:, 0:128] + qkm[:, 128:256]).astype(qk_ref.dtype)
            kgt_ref[0, lo:hi, :] = (k * jnp.exp(gdc)).T.astype(kgt_ref.dtype)


def _gdn_local(qkvn, gcc, gdc, betac, gcr, batch, seq):
    ng = seq // GDN_GROUP
    wide = GDN_CHAINS * GDN_DK
    colb = pl.BlockSpec((GDN_GROUP, GDN_CHAINS), lambda b, s: (b * ng + s, 0))
    outb = pl.BlockSpec((GDN_GROUP, wide), lambda b, s: (b * ng + s, 0))
    n = batch * seq
    return pl.pallas_call(
        _gdn_local_kernel,
        grid=(batch, ng),
        in_specs=[pl.BlockSpec((GDN_GROUP, GDN_QKV), lambda b, s: (b * ng + s, 0)),
                  colb, colb, colb,
                  pl.BlockSpec((GDN_CHAINS, GDN_GROUP), lambda b, s: (0, b * ng + s))],
        out_specs=[outb, outb, outb, outb,
                   pl.BlockSpec((1, wide, GDN_GROUP), lambda b, s: (b, 0, s))],
        out_shape=[jax.ShapeDtypeStruct((n, wide), F32),
                   jax.ShapeDtypeStruct((n, wide), BF16),
                   jax.ShapeDtypeStruct((n, wide), BF16),
                   jax.ShapeDtypeStruct((n, wide), BF16),
                   jax.ShapeDtypeStruct((batch, wide, seq), BF16)],
        compiler_params=_cparams(2),
        name="gdn_local",
    )(qkvn, gcc, gdc, betac, gcr)


def _gdn_scan_kernel(uf_ref, ub_ref, wf_ref, wb_ref, qgf_ref, qgb_ref, qkf_ref, qkb_ref,
                     kgf_ref, kgb_ref, glf_ref, glb_ref, of_ref, ob_ref, s_scr):
    @pl.when(pl.program_id(1) == 0)
    def _():
        s_scr[...] = jnp.zeros_like(s_scr)

    c = GDN_CHUNK
    nc = GDN_GROUP // c
    zeros = jnp.zeros((c, GDN_DV), BF16)
    dirs = ((uf_ref, wf_ref, qgf_ref, qkf_ref, kgf_ref, glf_ref, of_ref, range(nc)),
            (ub_ref, wb_ref, qgb_ref, qkb_ref, kgb_ref, glb_ref, ob_ref, range(nc - 1, -1, -1)))
    for d, (u_ref, w_ref, qg_ref, qk_ref, kg_ref, gl_ref, o_ref, order) in enumerate(dirs):
        for h in range(GDN_HEADS):
            ch = d * GDN_HEADS + h
            lo, hi = h * GDN_DK, (h + 1) * GDN_DK
            state = s_scr[ch]
            for i in order:
                rows = slice(i * c, (i + 1) * c)
                sb = state.astype(BF16)
                v_new = u_ref[rows, lo:hi] - jnp.dot(w_ref[rows, lo:hi], sb, preferred_element_type=F32)
                vb = v_new.astype(BF16)
                v2 = jnp.concatenate([vb, zeros] if i % 2 == 0 else [zeros, vb], axis=0)
                o_ref[rows, lo:hi] = (jnp.dot(qg_ref[rows, lo:hi], sb, preferred_element_type=F32)
                                      + jnp.dot(qk_ref[rows, lo:hi], v2, preferred_element_type=F32))
                pair = slice((i // 2) * 2 * c, (i // 2 + 1) * 2 * c)
                decay = jnp.exp(gl_ref[ch:ch + 1, i * c:i * c + 1])
                state = state * decay + jnp.dot(kg_ref[0, lo:hi, pair], v2, preferred_element_type=F32)
            s_scr[ch] = state


def _gdn_scan(u, w, qg, qk, kgt, gl, batch, seq):
    ng = seq // GDN_GROUP
    half = GDN_HEADS * GDN_DK
    fwd = lambda b, s: (b * ng + s, 0)
    bwd = lambda b, s: (b * ng + ng - 1 - s, 1)
    blk = (GDN_GROUP, half)
    n = batch * seq
    pair = lambda arr: (arr, arr)
    out_f = pl.BlockSpec(blk, fwd)
    out_b = pl.BlockSpec(blk, lambda b, s: (b * ng + ng - 1 - s, 0))
    in_specs = []
    for _ in range(4):
        in_specs += [pl.BlockSpec(blk, fwd), pl.BlockSpec(blk, bwd)]
    in_specs += [pl.BlockSpec((1, half, GDN_GROUP), lambda b, s: (b, 0, s)),
                 pl.BlockSpec((1, half, GDN_GROUP), lambda b, s: (b, 1, ng - 1 - s)),
                 pl.BlockSpec((GDN_CHAINS, GDN_GROUP), lambda b, s: (0, b * ng + s)),
                 pl.BlockSpec((GDN_CHAINS, GDN_GROUP), lambda b, s: (0, b * ng + ng - 1 - s))]
    return pl.pallas_call(
        _gdn_scan_kernel,
        grid=(batch, ng),
        in_specs=in_specs,
        out_specs=[out_f, out_b],
        out_shape=[jax.ShapeDtypeStruct((n, half), F32), jax.ShapeDtypeStruct((n, half), F32)],
        scratch_shapes=[pltpu.VMEM((GDN_CHAINS, GDN_DK, GDN_DV), F32)],
        compiler_params=_cparams(2),
        name="gdn_scan",
    )(*pair(u), *pair(w), *pair(qg), *pair(qk), *pair(kgt), *pair(gl))


def _layer_norm(r, g, b):
    mu = jnp.mean(r, axis=-1, keepdims=True)
    var = jnp.mean(jnp.square(r - mu), axis=-1, keepdims=True)
    return (r - mu) * lax.rsqrt(var + LN_EPS) * g + b


def _merge_kernel(ya_ref, of_ref, ob_ref, gate_ref, yc_ref, gz_ref, x_ref, wb_ref, wo_ref,
                  nw_ref, lng_ref, lnb_ref, o_ref):
    segs = []
    for h in range(GDN_HEADS):
        sl = slice(h * GDN_DV, (h + 1) * GDN_DV)
        o = of_ref[:, sl] + ob_ref[:, sl]
        o = o * lax.rsqrt(jnp.mean(o * o, axis=-1, keepdims=True) + RMS_EPS) * nw_ref[...]
        gt = gate_ref[:, sl]
        segs.append(o * (gt * jax.nn.sigmoid(gt)))
    yb = jnp.concatenate(segs, axis=1).astype(BF16)
    merged = None
    for n, y in enumerate((ya_ref[...], yb, yc_ref[...])):
        proj = jnp.dot(y, wb_ref[n], preferred_element_type=F32)
        term = jax.nn.sigmoid(gz_ref[:, n * D_MODEL:(n + 1) * D_MODEL]) * proj
        merged = term if merged is None else merged + term
    hmix = jnp.dot(merged.astype(BF16), wo_ref[...], preferred_element_type=F32)
    o_ref[...] = _layer_norm(DEEPNORM_ALPHA * x_ref[...] + hmix, lng_ref[...], lnb_ref[...])


def _merge(ya, o_f, o_b, zb, yc, x, w_branch, w_out, norm_w, ln_g, ln_b, tm):
    n = x.shape[0]
    row = lambda width, col=0: pl.BlockSpec((tm, width), lambda i, col=col: (i, col))
    full = lambda shape: pl.BlockSpec(shape, lambda i: (0,) * len(shape))
    return pl.pallas_call(
        _merge_kernel,
        grid=(n // tm,),
        in_specs=[row(BRANCH_WIDTH), row(BRANCH_WIDTH), row(BRANCH_WIDTH),
                  row(BRANCH_WIDTH, ZB_GATE // BRANCH_WIDTH), row(BRANCH_WIDTH),
                  row(N_BRANCHES * D_MODEL, ZB_GZ // (N_BRANCHES * D_MODEL)), row(D_MODEL),
                  full((N_BRANCHES, BRANCH_WIDTH, D_MODEL)), full((D_MODEL, D_MODEL)),
                  full((1, GDN_DV)), full((1, D_MODEL)), full((1, D_MODEL))],
        out_specs=row(D_MODEL),
        out_shape=jax.ShapeDtypeStruct((n, D_MODEL), F32),
        compiler_params=_cparams(1),
        name="merge_ln",
    )(ya, o_f, o_b, zb, yc, zb, x, w_branch, w_out, norm_w.astype(F32).reshape(1, GDN_DV),
      ln_g.astype(F32).reshape(1, D_MODEL), ln_b.astype(F32).reshape(1, D_MODEL))


FFN_SPLIT = 2


def _ffn_kernel(xp_ref, x_ref, xn_ref, wup_ref, cw_ref, cb_ref, wdn_ref, lng_ref, lnb_ref, o_ref, *, tiles_per_seq):
    i = pl.program_id(0)
    tm = x_ref.shape[0]
    halo = V7X_SUBLANES
    x = x_ref[...]
    xe = jnp.concatenate([xp_ref[...], x, xn_ref[...]], axis=0).astype(BF16)
    xb = x.astype(BF16)
    rows = lax.broadcasted_iota(jnp.int32, (tm + 2 * halo, 1), 0)
    first = (i % tiles_per_seq) == 0
    last = (i % tiles_per_seq) == tiles_per_seq - 1
    keep = jnp.logical_not(((rows < halo) & first) | ((rows >= tm + halo) & last))
    fc = D_FF // FFN_SPLIT
    ext = tm + 2 * halo
    acc = None
    for c in range(FFN_SPLIT):
        ge = jnp.dot(xe, wup_ref[:, c * fc:(c + 1) * fc], preferred_element_type=F32)
        ge = jnp.where(keep, ge, 0.0)
        up = jnp.dot(xb, wup_ref[:, D_FF + c * fc:D_FF + (c + 1) * fc], preferred_element_type=F32)
        cw = cw_ref[:, c * fc:(c + 1) * fc]
        conv = (pltpu.roll(ge, 1, 0) * cw[0:1] + ge * cw[1:2] + pltpu.roll(ge, ext - 1, 0) * cw[2:3])
        gate = conv[halo:halo + tm] + cb_ref[:, c * fc:(c + 1) * fc]
        act = (gate * jax.nn.sigmoid(gate) * up).astype(BF16)
        part = jnp.dot(act, wdn_ref[c * fc:(c + 1) * fc, :], preferred_element_type=F32)
        acc = part if acc is None else acc + part
    o_ref[...] = _layer_norm(DEEPNORM_ALPHA * x + acc, lng_ref[...], lnb_ref[...])


def _ffn(x, w_up, conv_w, conv_b, w_down, ln_g, ln_b, seq, tm):
    n = x.shape[0]
    halo = V7X_SUBLANES
    per = tm // halo
    nh = n // halo
    full = lambda shape: pl.BlockSpec(shape, lambda i: (0,) * len(shape))
    kern = functools.partial(_ffn_kernel, tiles_per_seq=seq // tm)
    return pl.pallas_call(
        kern,
        grid=(n // tm,),
        in_specs=[pl.BlockSpec((halo, D_MODEL), lambda i: (jnp.maximum(i * per - 1, 0), 0)),
                  pl.BlockSpec((tm, D_MODEL), lambda i: (i, 0)),
                  pl.BlockSpec((halo, D_MODEL), lambda i: (jnp.minimum((i + 1) * per, nh - 1), 0)),
                  full((D_MODEL, 2 * D_FF)), full((FFN_CONV, D_FF)), full((1, D_FF)),
                  full((D_FF, D_MODEL)), full((1, D_MODEL)), full((1, D_MODEL))],
        out_specs=pl.BlockSpec((tm, D_MODEL), lambda i: (i, 0)),
        out_shape=jax.ShapeDtypeStruct((n, D_MODEL), F32),
        compiler_params=_cparams(1),
        name="ffn_ln",
    )(x, x, x, w_up, conv_w.astype(F32), conv_b.astype(F32).reshape(1, D_FF), w_down,
      ln_g.astype(F32).reshape(1, D_MODEL), ln_b.astype(F32).reshape(1, D_MODEL))


def _pack_w_in(w):
    widths = (512, 512, 512, GDN_QKV, 512, 8, 8, 512, 128, 128, N_BRANCHES * D_MODEL)
    offs = [0]
    for wd in widths:
        offs.append(offs[-1] + wd)
    seg = lambda i: w[:, offs[i]:offs[i + 1]]
    aq, ak, av, bqkv, bgate, ba, bb, cq, ck, cv, gz = (seg(i) for i in range(len(widths)))
    dup = lambda t: jnp.concatenate([t[:, :HEAD_DIM], t[:, :HEAD_DIM], t[:, HEAD_DIM:], t[:, HEAD_DIM:]], axis=1)
    w_a = jnp.concatenate([aq, ak, av, cq, dup(ck), dup(cv)], axis=1)
    pad = jnp.zeros((w.shape[0], ZB_COLS - ZB_AB - 16), w.dtype)
    w_b = jnp.concatenate([gz, bqkv, bgate, ba, bb, pad], axis=1)
    return w_a.astype(BF16), w_b.astype(BF16)


def kernel(x, rel_bias, w_in, diff_lambda, diff_subln, gdn_conv, gdn_a_log, gdn_dt_bias, gdn_norm, swa_sink,
           w_branch, w_out, ln1_g, ln1_b, ffn_up, ffn_conv, ffn_conv_b, ffn_down, ln2_g, ln2_b):
    batch, seq, d = x.shape
    assert d == D_MODEL and seq % GDN_GROUP == 0 and seq % Q_BLOCK == 0
    n = batch * seq
    tm = 256
    tb_diff = _diff_bias_tiles(rel_bias, seq)
    tb_swa = _swa_bias_tiles(rel_bias)
    xf = x.reshape(n, d)
    for l in range(DEPTH):
        w_a, w_b = _pack_w_in(w_in[l])
        za = _matmul(xf, w_a, BF16, 512, ZA_COLS // 2, "in_proj_a")
        zb = _matmul(xf, w_b, F32, 512, ZB_COLS // 3, "in_proj_b")
        ya = _diff_attention(za, tb_diff, diff_lambda[l], diff_subln[l], batch, seq, l)
        yc = _swa_attention(za, tb_swa, swa_sink[l], batch, seq)
        qkvn = _gdn_conv(zb, gdn_conv[l], batch, seq)
        ab_t = zb[:, ZB_AB:ZB_AB + 2 * GDN_CHAINS].T
        gc, gd, gl, beta = _gdn_gates(ab_t, gdn_a_log[l], gdn_dt_bias[l], batch, seq)
        u, w, qg, qk, kgt = _gdn_local(qkvn, gc.T, gd.T, beta.T, gc, batch, seq)
        o_f, o_b = _gdn_scan(u, w, qg, qk, kgt, gl, batch, seq)
        xf = _merge(ya, o_f, o_b, zb, yc, xf, w_branch[l].astype(BF16), w_out[l].astype(BF16),
                    gdn_norm[l], ln1_g[l], ln1_b[l], tm)
        xf = _ffn(xf, ffn_up[l].astype(BF16), ffn_conv[l], ffn_conv_b[l], ffn_down[l].astype(BF16),
                  ln2_g[l], ln2_b[l], seq, tm)
    return xf.reshape(batch, seq, d)
```

```python
import functools
import math

import jax
import jax.numpy as jnp
from jax import lax
from jax.experimental import pallas as pl
from jax.experimental.pallas import tpu as pltpu

F32 = jnp.float32
BF16 = jnp.bfloat16

D_MODEL = 1024
DEPTH = 2
HEAD_DIM = 64
DIFF_HEADS = 4
DIFF_V = 2 * HEAD_DIM
Q_BLOCK = 128
GDN_HEADS = 4
GDN_DK = 128
GDN_DV = 128
GDN_QKV = GDN_HEADS * (2 * GDN_DK + GDN_DV)
GDN_CONV = 5
GDN_CHUNK = 64
SWA_HEADS = 8
SWA_KV_HEADS = 2
SWA_WINDOW = 128
SWA_BLOCK = 128
BRANCH_WIDTH = 512
N_BRANCHES = 3
REL_BUCKETS = 32
REL_MAX_DIST = 128
D_FF = 2816
FFN_CONV = 3
DEEPNORM_ALPHA = (2 * DEPTH) ** 0.25
LN_EPS = 1e-5
RMS_EPS = 1e-6

V7X_LANES = 128
V7X_SUBLANES = 8
V7X_VMEM_BYTES = 64 * 1024 * 1024
VMEM_LIMIT = V7X_VMEM_BYTES * 7 // 8

NEG_BIG = -1e30

ZA_AQ, ZA_AK, ZA_AV = 0, 512, 1024
ZA_CQ = 1536
ZA_CK = 2048
ZA_CV = 2304
ZA_COLS = 2560
ZB_GZ = 0
ZB_QKV = 3072
ZB_GATE = 4608
ZB_AB = 5120
ZB_COLS = 5376

GDN_GROUP = 256
GDN_CHAINS = 2 * GDN_HEADS


def _cparams(n_grid):
    return pltpu.CompilerParams(dimension_semantics=("arbitrary",) * n_grid, vmem_limit_bytes=VMEM_LIMIT)


def _t5_bucket(rel):
    nb = REL_BUCKETS // 2
    ret = jnp.where(rel > 0, nb, 0)
    n = jnp.abs(rel)
    max_exact = nb // 2
    large = max_exact + (jnp.log(jnp.maximum(n, 1).astype(jnp.float32) / max_exact)
                         / math.log(REL_MAX_DIST / max_exact) * (nb - max_exact)).astype(jnp.int32)
    large = jnp.minimum(large, nb - 1)
    return ret + jnp.where(n < max_exact, n, large)


def _table_lookup(table, bucket):
    out = jnp.zeros((table.shape[1],) + bucket.shape, F32)
    for i in range(REL_BUCKETS):
        out = jnp.where(bucket[None] == i, table[i].astype(F32).reshape((-1,) + (1,) * bucket.ndim), out)
    return out


def _diff_bias_tiles(rel_bias, seq):
    nq = seq // Q_BLOCK
    d = jnp.arange(2 * nq - 1) - (nq - 1)
    r = jnp.arange(Q_BLOCK)
    rel = d[:, None, None] * Q_BLOCK + r[None, None, :] - r[None, :, None]
    return _table_lookup(rel_bias[:, :DIFF_HEADS], _t5_bucket(rel))


def _swa_bias_tiles(rel_bias):
    kb = 3 * SWA_BLOCK
    rel = jnp.arange(kb)[None, :] - SWA_WINDOW - jnp.arange(SWA_BLOCK)[:, None]
    bias = _table_lookup(rel_bias[:, DIFF_HEADS:], _t5_bucket(rel))
    return jnp.where((jnp.abs(rel) <= SWA_WINDOW)[None], bias, NEG_BIG)


def _matmul_kernel(x_ref, w_ref, o_ref):
    o_ref[...] = jnp.dot(x_ref[...].astype(BF16), w_ref[...],
                         preferred_element_type=F32).astype(o_ref.dtype)


def _matmul(x, w, out_dtype, tm, tn, name):
    m, k = x.shape
    n = w.shape[1]
    return pl.pallas_call(
        _matmul_kernel,
        grid=(n // tn, m // tm),
        in_specs=[pl.BlockSpec((tm, k), lambda j, i: (i, 0)),
                  pl.BlockSpec((k, tn), lambda j, i: (0, j))],
        out_specs=pl.BlockSpec((tm, tn), lambda j, i: (i, j)),
        out_shape=jax.ShapeDtypeStruct((m, n), out_dtype),
        compiler_params=_cparams(2),
        name=name,
    )(x, w)


def _diff_attn_kernel(q_ref, k_ref, v_ref, tb_ref, lam_ref, subln_ref, o_ref, s_scr, *, nq, lam_init):
    qi = pl.program_id(2)
    q = q_ref[...]
    lane = lax.broadcasted_iota(jnp.int32, q.shape, 1)
    zero = jnp.zeros_like(q)
    qz = jnp.concatenate([jnp.where(lane < HEAD_DIM, q, zero),
                          jnp.where(lane >= HEAD_DIM, q, zero)], axis=0)
    scale = HEAD_DIM ** -0.5
    for kj in range(nq):
        kblk = k_ref[kj * Q_BLOCK:(kj + 1) * Q_BLOCK, :]
        s = lax.dot_general(qz, kblk, (((1,), (1,)), ((), ())), preferred_element_type=F32) * scale
        b = tb_ref[0, nq - 1 - qi + kj]
        s_scr[0:Q_BLOCK, kj * Q_BLOCK:(kj + 1) * Q_BLOCK] = s[:Q_BLOCK] + b
        s_scr[Q_BLOCK:2 * Q_BLOCK, kj * Q_BLOCK:(kj + 1) * Q_BLOCK] = s[Q_BLOCK:] + b
    s = s_scr[...]
    m = jnp.max(s, axis=-1, keepdims=True)
    e = jnp.exp(s - m)
    l = jnp.sum(e, axis=-1, keepdims=True)
    pv = jnp.dot(e.astype(BF16), v_ref[...], preferred_element_type=F32) / l
    lv = lam_ref[...]
    lam = (jnp.exp(jnp.sum(lv[0:1] * lv[1:2], axis=-1, keepdims=True))
           - jnp.exp(jnp.sum(lv[2:3] * lv[3:4], axis=-1, keepdims=True)) + lam_init)
    o = pv[:Q_BLOCK] - lam * pv[Q_BLOCK:]
    o = o * lax.rsqrt(jnp.mean(o * o, axis=-1, keepdims=True) + RMS_EPS) * subln_ref[...]
    o_ref[...] = (o * (1.0 - lam_init)).astype(o_ref.dtype)


def _diff_attention(za, tb, lam_vecs, subln, batch, seq, layer_idx):
    nq = seq // Q_BLOCK
    lam_init = 0.8 - 0.6 * math.exp(-0.3 * layer_idx)
    kern = functools.partial(_diff_attn_kernel, nq=nq, lam_init=lam_init)
    qcol, kcol, vcol = ZA_AQ // 128, ZA_AK // 128, ZA_AV // 128
    return pl.pallas_call(
        kern,
        grid=(batch, DIFF_HEADS, nq),
        in_specs=[
            pl.BlockSpec((Q_BLOCK, 128), lambda b, h, i: (b * nq + i, qcol + h)),
            pl.BlockSpec((seq, 128), lambda b, h, i: (b, kcol + h)),
            pl.BlockSpec((seq, 128), lambda b, h, i: (b, vcol + h)),
            pl.BlockSpec((1, 2 * nq - 1, Q_BLOCK, Q_BLOCK), lambda b, h, i: (h, 0, 0, 0)),
            pl.BlockSpec((4, HEAD_DIM), lambda b, h, i: (0, 0)),
            pl.BlockSpec((1, DIFF_V), lambda b, h, i: (0, 0)),
        ],
        out_specs=pl.BlockSpec((Q_BLOCK, 128), lambda b, h, i: (b * nq + i, h)),
        out_shape=jax.ShapeDtypeStruct((batch * seq, BRANCH_WIDTH), BF16),
        scratch_shapes=[pltpu.VMEM((2 * Q_BLOCK, seq), F32)],
        compiler_params=_cparams(3),
        name="diff_attention",
    )(za, za, za, tb, lam_vecs.astype(F32), subln.astype(F32).reshape(1, DIFF_V))


def _swa_kernel(q_ref, kp_ref, kc_ref, kn_ref, vp_ref, vc_ref, vn_ref, bias_ref, sink_ref, o_ref, *, nb):
    n = pl.program_id(2)
    q = q_ref[...]
    k = jnp.concatenate([kp_ref[...], kc_ref[...], kn_ref[...]], axis=0)
    v = jnp.concatenate([vp_ref[...], vc_ref[...], vn_ref[...]], axis=0)
    col = lax.broadcasted_iota(jnp.int32, (SWA_BLOCK, 3 * SWA_BLOCK), 1)
    outside = ((col < SWA_BLOCK) & (n == 0)) | ((col >= 2 * SWA_BLOCK) & (n == nb - 1))
    lane = lax.broadcasted_iota(jnp.int32, (SWA_BLOCK, 128), 1)
    low = lane < HEAD_DIM
    zero = jnp.zeros((SWA_BLOCK, 128), q.dtype)
    scale = HEAD_DIM ** -0.5
    rep = SWA_HEADS // SWA_KV_HEADS
    outs = []
    for j in range(rep):
        qc = q[:, (j // 2) * 128:(j // 2 + 1) * 128]
        qm = jnp.where(low if j % 2 == 0 else jnp.logical_not(low), qc, zero)
        s = lax.dot_general(qm, k, (((1,), (1,)), ((), ())), preferred_element_type=F32) * scale
        s = jnp.where(outside, NEG_BIG, s + bias_ref[j])
        sink = sink_ref[j]
        m = jnp.maximum(jnp.max(s, axis=-1, keepdims=True), sink[:, 0:1])
        e = jnp.exp(s - m)
        den = jnp.sum(e, axis=-1, keepdims=True) + jnp.exp(sink[:, 0:1] - m)
        outs.append(jnp.dot(e.astype(BF16), v, preferred_element_type=F32) / den)
    o = jnp.concatenate([jnp.where(low, outs[0], outs[1]), jnp.where(low, outs[2], outs[3])], axis=1)
    o_ref[...] = o.astype(o_ref.dtype)


def _swa_attention(za, bias, sink, batch, seq):
    nb = seq // SWA_BLOCK
    rep = SWA_HEADS // SWA_KV_HEADS
    kern = functools.partial(_swa_kernel, nb=nb)
    qcol, kcol, vcol = ZA_CQ // 256, ZA_CK // 128, ZA_CV // 128
    prev = lambda b, g, n: (b * nb + jnp.maximum(n - 1, 0))
    nxt = lambda b, g, n: (b * nb + jnp.minimum(n + 1, nb - 1))
    blk = (SWA_BLOCK, 128)
    sink_b = jnp.broadcast_to(sink.astype(F32).reshape(SWA_HEADS, 1, 1), (SWA_HEADS, 1, 128))
    return pl.pallas_call(
        kern,
        grid=(batch, SWA_KV_HEADS, nb),
        in_specs=[
            pl.BlockSpec((SWA_BLOCK, 256), lambda b, g, n: (b * nb + n, qcol + g)),
            pl.BlockSpec(blk, lambda b, g, n: (prev(b, g, n), kcol + g)),
            pl.BlockSpec(blk, lambda b, g, n: (b * nb + n, kcol + g)),
            pl.BlockSpec(blk, lambda b, g, n: (nxt(b, g, n), kcol + g)),
            pl.BlockSpec(blk, lambda b, g, n: (prev(b, g, n), vcol + g)),
            pl.BlockSpec(blk, lambda b, g, n: (b * nb + n, vcol + g)),
            pl.BlockSpec(blk, lambda b, g, n: (nxt(b, g, n), vcol + g)),
            pl.BlockSpec((rep, SWA_BLOCK, 3 * SWA_BLOCK), lambda b, g, n: (g, 0, 0)),
            pl.BlockSpec((rep, 1, 128), lambda b, g, n: (g, 0, 0)),
        ],
        out_specs=pl.BlockSpec((SWA_BLOCK, 256), lambda b, g, n: (b * nb + n, g)),
        out_shape=jax.ShapeDtypeStruct((batch * seq, BRANCH_WIDTH), BF16),
        compiler_params=_cparams(3),
        name="swa_attention",
    )(za, za, za, za, za, za, za, bias, sink_b)


def _shift_rows(x, d):
    rows = x.shape[0]
    t = lax.broadcasted_iota(jnp.int32, x.shape, 0)
    rolled = pltpu.roll(x, (-d) % rows, 0)
    return jnp.where((t + d >= 0) & (t + d < rows), rolled, 0.0)


def _gdn_conv_kernel(x_ref, w_ref, o_ref):
    j = pl.program_id(1)
    x = x_ref[...]
    w = w_ref[...]
    half = GDN_CONV // 2
    acc = x * w[half:half + 1]
    for tap in range(GDN_CONV):
        if tap != half:
            acc = acc + _shift_rows(x, tap - half) * w[tap:tap + 1]
    h = acc * jax.nn.sigmoid(acc)
    inv = lax.rsqrt(jnp.sum(h * h, axis=-1, keepdims=True) + RMS_EPS)
    factor = jnp.where(j < 2 * GDN_HEADS, inv, 1.0) * jnp.where(j < GDN_HEADS, GDN_DK ** -0.5, 1.0)
    o_ref[...] = h * factor


def _gdn_conv(zb, conv_w, batch, seq):
    c0 = ZB_QKV // 128
    return pl.pallas_call(
        _gdn_conv_kernel,
        grid=(batch, GDN_QKV // 128),
        in_specs=[pl.BlockSpec((seq, 128), lambda b, j: (b, c0 + j)),
                  pl.BlockSpec((GDN_CONV, 128), lambda b, j: (0, j))],
        out_specs=pl.BlockSpec((seq, 128), lambda b, j: (b, j)),
        out_shape=jax.ShapeDtypeStruct((batch * seq, GDN_QKV), F32),
        compiler_params=_cparams(2),
        name="gdn_conv",
    )(zb, conv_w.astype(F32))


def _gdn_gate_kernel(ab_ref, alog_ref, dtb_ref, gc_ref, gd_ref, gl_ref, beta_ref):
    ab = ab_ref[...]
    a = ab[0:GDN_CHAINS]
    seq = a.shape[1]
    xs = a + dtb_ref[...]
    softplus = jnp.maximum(xs, 0.0) + jnp.log1p(jnp.exp(-jnp.abs(xs)))
    g = -jnp.exp(alog_ref[...]) * softplus
    beta_ref[...] = jax.nn.sigmoid(ab[GDN_CHAINS:2 * GDN_CHAINS])
    pos = lax.broadcasted_iota(jnp.int32, g.shape, 1) % GDN_CHUNK
    cf = g
    cr = g
    s = 1
    while s < GDN_CHUNK:
        cf = cf + jnp.where(pos >= s, pltpu.roll(cf, s, 1), 0.0)
        cr = cr + jnp.where(pos < GDN_CHUNK - s, pltpu.roll(cr, seq - s, 1), 0.0)
        s *= 2
    fwd = lax.broadcasted_iota(jnp.int32, g.shape, 0) < GDN_HEADS
    gc_ref[...] = jnp.where(fwd, cf, cr)
    gd_ref[...] = jnp.where(fwd, cr, cf) - g
    gl_ref[...] = cf + cr - g


def _gdn_gates(ab_t, a_log, dt_bias, batch, seq):
    row = pl.BlockSpec((GDN_CHAINS, seq), lambda b: (0, b))
    shp = jax.ShapeDtypeStruct((GDN_CHAINS, batch * seq), F32)
    par = pl.BlockSpec((GDN_CHAINS, 1), lambda b: (0, 0))
    return pl.pallas_call(
        _gdn_gate_kernel,
        grid=(batch,),
        in_specs=[pl.BlockSpec((2 * GDN_CHAINS, seq), lambda b: (0, b)), par, par],
        out_specs=[row, row, row, row],
        out_shape=[shp, shp, shp, shp],
        compiler_params=_cparams(1),
        name="gdn_gates",
    )(ab_t, a_log.astype(F32).reshape(GDN_CHAINS, 1), dt_bias.astype(F32).reshape(GDN_CHAINS, 1))


def _hdot(a, b):
    return jnp.dot(a.astype(BF16), b.astype(BF16), preferred_element_type=F32)


def _gdn_local_kernel(qkv_ref, gcc_ref, gdc_ref, betac_ref, gcr_ref,
                      u_ref, w_ref, qg_ref, qk_ref, kgt_ref):
    g = GDN_GROUP
    c = GDN_CHUNK
    ri = lax.broadcasted_iota(jnp.int32, (g, g), 0)
    ci = lax.broadcasted_iota(jnp.int32, (g, g), 1)
    same = (ri // c) == (ci // c)
    ahead = jnp.where(same, ri - ci, -g)
    behind = jnp.where(same, ci - ri, -g)
    eye_s = (lax.broadcasted_iota(jnp.int32, (c, g), 0)
             == lax.broadcasted_iota(jnp.int32, (c, g), 1) % c).astype(F32)

    def block_diag(xs):
        return jnp.where(same, jnp.concatenate([xs] * (g // c), axis=0), 0.0)

    def row_blocks_sum(xd):
        out = xd[0:c]
        for i in range(1, g // c):
            out = out + xd[i * c:(i + 1) * c]
        return out

    for h in range(GDN_HEADS):
        q = qkv_ref[:, h * GDN_DK:(h + 1) * GDN_DK]
        k = qkv_ref[:, (GDN_HEADS + h) * GDN_DK:(GDN_HEADS + h + 1) * GDN_DK]
        v = qkv_ref[:, (2 * GDN_HEADS + h) * GDN_DK:(2 * GDN_HEADS + h + 1) * GDN_DK]
        kbf = k.astype(BF16)
        qk_raw = lax.dot_general(q.astype(BF16), kbf, (((1,), (1,)), ((), ())), preferred_element_type=F32)
        for d in range(2):
            ch = d * GDN_HEADS + h
            gcc = gcc_ref[:, ch:ch + 1]
            gdc = gdc_ref[:, ch:ch + 1]
            beta = betac_ref[:, ch:ch + 1]
            gcr = gcr_ref[ch:ch + 1, :]
            order = ahead if d == 0 else behind
            kb = k * beta
            kk = lax.dot_general(kb.astype(BF16), kbf, (((1,), (1,)), ((), ())), preferred_element_type=F32)
            decay = jnp.exp(jnp.where(order >= 0, gcc - gcr, NEG_BIG))
            xd = -jnp.where(order > 0, kk * decay, 0.0)
            qkm = qk_raw * decay
            xs = row_blocks_sum(xd)
            p = eye_s + xs
            xs = _hdot(xs, xd)
            for _ in range(4):
                r = _hdot(jnp.concatenate([p, xs], axis=0), block_diag(xs))
                p = p + r[0:c]
                xs = r[c:2 * c]
            p = p + _hdot(p, block_diag(xs))
            rhs = jnp.concatenate([v * beta, kb * jnp.exp(gcc)], axis=1)
            sol = _hdot(block_diag(p), rhs)
            lo, hi = ch * GDN_DK, (ch + 1) * GDN_DK
            u_ref[:, lo:hi] = sol[:, 0:GDN_DV]
            w_ref[:, lo:hi] = sol[:, GDN_DV:].astype(w_ref.dtype)
            qg_ref[:, lo:hi] = (q * jnp.exp(gcc)).astype(qg_ref.dtype)
            qk_ref[:, lo:hi] = (qkm[:, 0:128] + qkm[:, 128:256]).astype(qk_ref.dtype)
            kgt_ref[0, lo:hi, :] = (k * jnp.exp(gdc)).T.astype(kgt_ref.dtype)


def _gdn_local(qkvn, gcc, gdc, betac, gcr, batch, seq):
    ng = seq // GDN_GROUP
    wide = GDN_CHAINS * GDN_DK
    colb = pl.BlockSpec((GDN_GROUP, GDN_CHAINS), lambda b, s: (b * ng + s, 0))
    outb = pl.BlockSpec((GDN_GROUP, wide), lambda b, s: (b * ng + s, 0))
    n = batch * seq
    return pl.pallas_call(
        _gdn_local_kernel,
        grid=(batch, ng),
        in_specs=[pl.BlockSpec((GDN_GROUP, GDN_QKV), lambda b, s: (b * ng + s, 0)),
                  colb, colb, colb,
                  pl.BlockSpec((GDN_CHAINS, GDN_GROUP), lambda b, s: (0, b * ng + s))],
        out_specs=[outb, outb, outb, outb,
                   pl.BlockSpec((1, wide, GDN_GROUP), lambda b, s: (b, 0, s))],
        out_shape=[jax.ShapeDtypeStruct((n, wide), F32),
                   jax.ShapeDtypeStruct((n, wide), BF16),
                   jax.ShapeDtypeStruct((n, wide), BF16),
                   jax.ShapeDtypeStruct((n, wide), BF16),
                   jax.ShapeDtypeStruct((batch, wide, seq), BF16)],
        compiler_params=_cparams(2),
        name="gdn_local",
    )(qkvn, gcc, gdc, betac, gcr)


def _gdn_scan_kernel(uf_ref, ub_ref, wf_ref, wb_ref, qgf_ref, qgb_ref, qkf_ref, qkb_ref,
                     kgf_ref, kgb_ref, glf_ref, glb_ref, of_ref, ob_ref, s_scr):
    @pl.when(pl.program_id(1) == 0)
    def _():
        s_scr[...] = jnp.zeros_like(s_scr)

    c = GDN_CHUNK
    nc = GDN_GROUP // c
    zeros = jnp.zeros((c, GDN_DV), BF16)
    dirs = ((uf_ref, wf_ref, qgf_ref, qkf_ref, kgf_ref, glf_ref, of_ref, range(nc)),
            (ub_ref, wb_ref, qgb_ref, qkb_ref, kgb_ref, glb_ref, ob_ref, range(nc - 1, -1, -1)))
    for d, (u_ref, w_ref, qg_ref, qk_ref, kg_ref, gl_ref, o_ref, order) in enumerate(dirs):
        for h in range(GDN_HEADS):
            ch = d * GDN_HEADS + h
            lo, hi = h * GDN_DK, (h + 1) * GDN_DK
            state = s_scr[ch]
            for i in order:
                rows = slice(i * c, (i + 1) * c)
                sb = state.astype(BF16)
                v_new = u_ref[rows, lo:hi] - jnp.dot(w_ref[rows, lo:hi], sb, preferred_element_type=F32)
                vb = v_new.astype(BF16)
                v2 = jnp.concatenate([vb, zeros] if i % 2 == 0 else [zeros, vb], axis=0)
                o_ref[rows, lo:hi] = (jnp.dot(qg_ref[rows, lo:hi], sb, preferred_element_type=F32)
                                      + jnp.dot(qk_ref[rows, lo:hi], v2, preferred_element_type=F32))
                pair = slice((i // 2) * 2 * c, (i // 2 + 1) * 2 * c)
                decay = jnp.exp(gl_ref[ch:ch + 1, i * c:i * c + 1])
                state = state * decay + jnp.dot(kg_ref[0, lo:hi, pair], v2, preferred_element_type=F32)
            s_scr[ch] = state


def _gdn_scan(u, w, qg, qk, kgt, gl, batch, seq):
    ng = seq // GDN_GROUP
    half = GDN_HEADS * GDN_DK
    fwd = lambda b, s: (b * ng + s, 0)
    bwd = lambda b, s: (b * ng + ng - 1 - s, 1)
    blk = (GDN_GROUP, half)
    n = batch * seq
    pair = lambda arr: (arr, arr)
    out_f = pl.BlockSpec(blk, fwd)
    out_b = pl.BlockSpec(blk, lambda b, s: (b * ng + ng - 1 - s, 0))
    in_specs = []
    for _ in range(4):
        in_specs += [pl.BlockSpec(blk, fwd), pl.BlockSpec(blk, bwd)]
    in_specs += [pl.BlockSpec((1, half, GDN_GROUP), lambda b, s: (b, 0, s)),
                 pl.BlockSpec((1, half, GDN_GROUP), lambda b, s: (b, 1, ng - 1 - s)),
                 pl.BlockSpec((GDN_CHAINS, GDN_GROUP), lambda b, s: (0, b * ng + s)),
                 pl.BlockSpec((GDN_CHAINS, GDN_GROUP), lambda b, s: (0, b * ng + ng - 1 - s))]
    return pl.pallas_call(
        _gdn_scan_kernel,
        grid=(batch, ng),
        in_specs=in_specs,
        out_specs=[out_f, out_b],
        out_shape=[jax.ShapeDtypeStruct((n, half), F32), jax.ShapeDtypeStruct((n, half), F32)],
        scratch_shapes=[pltpu.VMEM((GDN_CHAINS, GDN_DK, GDN_DV), F32)],
        compiler_params=_cparams(2),
        name="gdn_scan",
    )(*pair(u), *pair(w), *pair(qg), *pair(qk), *pair(kgt), *pair(gl))


def _layer_norm(r, g, b):
    mu = jnp.mean(r, axis=-1, keepdims=True)
    var = jnp.mean(jnp.square(r - mu), axis=-1, keepdims=True)
    return (r - mu) * lax.rsqrt(var + LN_EPS) * g + b


def _merge_kernel(ya_ref, of_ref, ob_ref, gate_ref, yc_ref, gz_ref, x_ref, wb_ref, wo_ref,
                  nw_ref, lng_ref, lnb_ref, o_ref):
    segs = []
    for h in range(GDN_HEADS):
        sl = slice(h * GDN_DV, (h + 1) * GDN_DV)
        o = of_ref[:, sl] + ob_ref[:, sl]
        o = o * lax.rsqrt(jnp.mean(o * o, axis=-1, keepdims=True) + RMS_EPS) * nw_ref[...]
        gt = gate_ref[:, sl]
        segs.append(o * (gt * jax.nn.sigmoid(gt)))
    yb = jnp.concatenate(segs, axis=1).astype(BF16)
    merged = None
    for n, y in enumerate((ya_ref[...], yb, yc_ref[...])):
        proj = jnp.dot(y, wb_ref[n], preferred_element_type=F32)
        term = jax.nn.sigmoid(gz_ref[:, n * D_MODEL:(n + 1) * D_MODEL]) * proj
        merged = term if merged is None else merged + term
    hmix = jnp.dot(merged.astype(BF16), wo_ref[...], preferred_element_type=F32)
    o_ref[...] = _layer_norm(DEEPNORM_ALPHA * x_ref[...] + hmix, lng_ref[...], lnb_ref[...])


def _merge(ya, o_f, o_b, zb, yc, x, w_branch, w_out, norm_w, ln_g, ln_b, tm):
    n = x.shape[0]
    row = lambda width, col=0: pl.BlockSpec((tm, width), lambda i, col=col: (i, col))
    full = lambda shape: pl.BlockSpec(shape, lambda i: (0,) * len(shape))
    return pl.pallas_call(
        _merge_kernel,
        grid=(n // tm,),
        in_specs=[row(BRANCH_WIDTH), row(BRANCH_WIDTH), row(BRANCH_WIDTH),
                  row(BRANCH_WIDTH, ZB_GATE // BRANCH_WIDTH), row(BRANCH_WIDTH),
                  row(N_BRANCHES * D_MODEL, ZB_GZ // (N_BRANCHES * D_MODEL)), row(D_MODEL),
                  full((N_BRANCHES, BRANCH_WIDTH, D_MODEL)), full((D_MODEL, D_MODEL)),
                  full((1, GDN_DV)), full((1, D_MODEL)), full((1, D_MODEL))],
        out_specs=row(D_MODEL),
        out_shape=jax.ShapeDtypeStruct((n, D_MODEL), F32),
        compiler_params=_cparams(1),
        name="merge_ln",
    )(ya, o_f, o_b, zb, yc, zb, x, w_branch, w_out, norm_w.astype(F32).reshape(1, GDN_DV),
      ln_g.astype(F32).reshape(1, D_MODEL), ln_b.astype(F32).reshape(1, D_MODEL))


FFN_SPLIT = 2


def _ffn_kernel(xp_ref, x_ref, xn_ref, wup_ref, cw_ref, cb_ref, wdn_ref, lng_ref, lnb_ref, o_ref, *, tiles_per_seq):
    i = pl.program_id(0)
    tm = x_ref.shape[0]
    halo = V7X_SUBLANES
    x = x_ref[...]
    xe = jnp.concatenate([xp_ref[...], x, xn_ref[...]], axis=0).astype(BF16)
    xb = x.astype(BF16)
    rows = lax.broadcasted_iota(jnp.int32, (tm + 2 * halo, 1), 0)
    first = (i % tiles_per_seq) == 0
    last = (i % tiles_per_seq) == tiles_per_seq - 1
    keep = jnp.logical_not(((rows < halo) & first) | ((rows >= tm + halo) & last))
    fc = D_FF // FFN_SPLIT
    ext = tm + 2 * halo
    acc = None
    for c in range(FFN_SPLIT):
        ge = jnp.dot(xe, wup_ref[:, c * fc:(c + 1) * fc], preferred_element_type=F32)
        ge = jnp.where(keep, ge, 0.0)
        up = jnp.dot(xb, wup_ref[:, D_FF + c * fc:D_FF + (c + 1) * fc], preferred_element_type=F32)
        cw = cw_ref[:, c * fc:(c + 1) * fc]
        conv = (pltpu.roll(ge, 1, 0) * cw[0:1] + ge * cw[1:2] + pltpu.roll(ge, ext - 1, 0) * cw[2:3])
        gate = conv[halo:halo + tm] + cb_ref[:, c * fc:(c + 1) * fc]
        act = (gate * jax.nn.sigmoid(gate) * up).astype(BF16)
        part = jnp.dot(act, wdn_ref[c * fc:(c + 1) * fc, :], preferred_element_type=F32)
        acc = part if acc is None else acc + part
    o_ref[...] = _layer_norm(DEEPNORM_ALPHA * x + acc, lng_ref[...], lnb_ref[...])


def _ffn(x, w_up, conv_w, conv_b, w_down, ln_g, ln_b, seq, tm):
    n = x.shape[0]
    halo = V7X_SUBLANES
    per = tm // halo
    nh = n // halo
    full = lambda shape: pl.BlockSpec(shape, lambda i: (0,) * len(shape))
    kern = functools.partial(_ffn_kernel, tiles_per_seq=seq // tm)
    return pl.pallas_call(
        kern,
        grid=(n // tm,),
        in_specs=[pl.BlockSpec((halo, D_MODEL), lambda i: (jnp.maximum(i * per - 1, 0), 0)),
                  pl.BlockSpec((tm, D_MODEL), lambda i: (i, 0)),
                  pl.BlockSpec((halo, D_MODEL), lambda i: (jnp.minimum((i + 1) * per, nh - 1), 0)),
                  full((D_MODEL, 2 * D_FF)), full((FFN_CONV, D_FF)), full((1, D_FF)),
                  full((D_FF, D_MODEL)), full((1, D_MODEL)), full((1, D_MODEL))],
        out_specs=pl.BlockSpec((tm, D_MODEL), lambda i: (i, 0)),
        out_shape=jax.ShapeDtypeStruct((n, D_MODEL), F32),
        compiler_params=_cparams(1),
        name="ffn_ln",
    )(x, x, x, w_up, conv_w.astype(F32), conv_b.astype(F32).reshape(1, D_FF), w_down,
      ln_g.astype(F32).reshape(1, D_MODEL), ln_b.astype(F32).reshape(1, D_MODEL))


def _pack_w_in(w):
    widths = (512, 512, 512, GDN_QKV, 512, 8, 8, 512, 128, 128, N_BRANCHES * D_MODEL)
    offs = [0]
    for wd in widths:
        offs.append(offs[-1] + wd)
    seg = lambda i: w[:, offs[i]:offs[i + 1]]
    aq, ak, av, bqkv, bgate, ba, bb, cq, ck, cv, gz = (seg(i) for i in range(len(widths)))
    dup = lambda t: jnp.concatenate([t[:, :HEAD_DIM], t[:, :HEAD_DIM], t[:, HEAD_DIM:], t[:, HEAD_DIM:]], axis=1)
    w_a = jnp.concatenate([aq, ak, av, cq, dup(ck), dup(cv)], axis=1)
    pad = jnp.zeros((w.shape[0], ZB_COLS - ZB_AB - 16), w.dtype)
    w_b = jnp.concatenate([gz, bqkv, bgate, ba, bb, pad], axis=1)
    return w_a.astype(BF16), w_b.astype(BF16)


def kernel(x, rel_bias, w_in, diff_lambda, diff_subln, gdn_conv, gdn_a_log, gdn_dt_bias, gdn_norm, swa_sink,
           w_branch, w_out, ln1_g, ln1_b, ffn_up, ffn_conv, ffn_conv_b, ffn_down, ln2_g, ln2_b):
    batch, seq, d = x.shape
    assert d == D_MODEL and seq % GDN_GROUP == 0 and seq % Q_BLOCK == 0
    n = batch * seq
    tm = 256
    tb_diff = _diff_bias_tiles(rel_bias, seq)
    tb_swa = _swa_bias_tiles(rel_bias)
    xf = x.reshape(n, d)
    for l in range(DEPTH):
        w_a, w_b = _pack_w_in(w_in[l])
        za = _matmul(xf, w_a, BF16, 512, ZA_COLS // 2, "in_proj_a")
        zb = _matmul(xf, w_b, F32, 512, ZB_COLS // 3, "in_proj_b")
        ya = _diff_attention(za, tb_diff, diff_lambda[l], diff_subln[l], batch, seq, l)
        yc = _swa_attention(za, tb_swa, swa_sink[l], batch, seq)
        qkvn = _gdn_conv(zb, gdn_conv[l], batch, seq)
        ab_t = zb[:, ZB_AB:ZB_AB + 2 * GDN_CHAINS].T
        gc, gd, gl, beta = _gdn_gates(ab_t, gdn_a_log[l], gdn_dt_bias[l], batch, seq)
        u, w, qg, qk, kgt = _gdn_local(qkvn, gc.T, gd.T, beta.T, gc, batch, seq)
        o_f, o_b = _gdn_scan(u, w, qg, qk, kgt, gl, batch, seq)
        xf = _merge(ya, o_f, o_b, zb, yc, xf, w_branch[l].astype(BF16), w_out[l].astype(BF16),
                    gdn_norm[l], ln1_g[l], ln1_b[l], tm)
        xf = _ffn(xf, ffn_up[l].astype(BF16), ffn_conv[l], ffn_conv_b[l], ffn_down[l].astype(BF16),
                  ln2_g[l], ln2_b[l], seq, tm)
    return xf.reshape(batch, seq, d)
```

```python
import functools
import math

import jax
import jax.numpy as jnp
from jax import lax
from jax.experimental import pallas as pl
from jax.experimental.pallas import tpu as pltpu

F32 = jnp.float32
BF16 = jnp.bfloat16

D_MODEL = 1024
DEPTH = 2
HEAD_DIM = 64
DIFF_HEADS = 4
DIFF_V = 2 * HEAD_DIM
Q_BLOCK = 128
GDN_HEADS = 4
GDN_DK = 128
GDN_DV = 128
GDN_QKV = GDN_HEADS * (2 * GDN_DK + GDN_DV)
GDN_CONV = 5
GDN_CHUNK = 64
SWA_HEADS = 8
SWA_KV_HEADS = 2
SWA_WINDOW = 128
SWA_BLOCK = 128
BRANCH_WIDTH = 512
N_BRANCHES = 3
REL_BUCKETS = 32
REL_MAX_DIST = 128
D_FF = 2816
FFN_CONV = 3
DEEPNORM_ALPHA = (2 * DEPTH) ** 0.25
LN_EPS = 1e-5
RMS_EPS = 1e-6

V7X_LANES = 128
V7X_SUBLANES = 8
V7X_VMEM_BYTES = 64 * 1024 * 1024
VMEM_LIMIT = V7X_VMEM_BYTES * 7 // 8

NEG_BIG = -1e30

ZA_AQ, ZA_AK, ZA_AV = 0, 512, 1024
ZA_CQ = 1536
ZA_CK = 2048
ZA_CV = 2304
ZA_COLS = 2560
ZB_GZ = 0
ZB_QKV = 3072
ZB_GATE = 4608
ZB_AB = 5120
ZB_COLS = 5376

GDN_GROUP = 256
GDN_CHAINS = 2 * GDN_HEADS


def _cparams(n_grid):
    return pltpu.CompilerParams(dimension_semantics=("arbitrary",) * n_grid, vmem_limit_bytes=VMEM_LIMIT)


def _t5_bucket(rel):
    nb = REL_BUCKETS // 2
    ret = jnp.where(rel > 0, nb, 0)
    n = jnp.abs(rel)
    max_exact = nb // 2
    large = max_exact + (jnp.log(jnp.maximum(n, 1).astype(jnp.float32) / max_exact)
                         / math.log(REL_MAX_DIST / max_exact) * (nb - max_exact)).astype(jnp.int32)
    large = jnp.minimum(large, nb - 1)
    return ret + jnp.where(n < max_exact, n, large)


def _table_lookup(table, bucket):
    out = jnp.zeros((table.shape[1],) + bucket.shape, F32)
    for i in range(REL_BUCKETS):
        out = jnp.where(bucket[None] == i, table[i].astype(F32).reshape((-1,) + (1,) * bucket.ndim), out)
    return out


def _diff_bias_tiles(rel_bias, seq):
    nq = seq // Q_BLOCK
    d = jnp.arange(2 * nq - 1) - (nq - 1)
    r = jnp.arange(Q_BLOCK)
    rel = d[:, None, None] * Q_BLOCK + r[None, None, :] - r[None, :, None]
    return _table_lookup(rel_bias[:, :DIFF_HEADS], _t5_bucket(rel))


def _swa_bias_tiles(rel_bias):
    kb = 3 * SWA_BLOCK
    rel = jnp.arange(kb)[None, :] - SWA_WINDOW - jnp.arange(SWA_BLOCK)[:, None]
    bias = _table_lookup(rel_bias[:, DIFF_HEADS:], _t5_bucket(rel))
    return jnp.where((jnp.abs(rel) <= SWA_WINDOW)[None], bias, NEG_BIG)


def _matmul_kernel(x_ref, w_ref, o_ref):
    o_ref[...] = jnp.dot(x_ref[...].astype(BF16), w_ref[...],
                         preferred_element_type=F32).astype(o_ref.dtype)


def _matmul(x, w, out_dtype, tm, tn, name):
    m, k = x.shape
    n = w.shape[1]
    return pl.pallas_call(
        _matmul_kernel,
        grid=(n // tn, m // tm),
        in_specs=[pl.BlockSpec((tm, k), lambda j, i: (i, 0)),
                  pl.BlockSpec((k, tn), lambda j, i: (0, j))],
        out_specs=pl.BlockSpec((tm, tn), lambda j, i: (i, j)),
        out_shape=jax.ShapeDtypeStruct((m, n), out_dtype),
        compiler_params=_cparams(2),
        name=name,
    )(x, w)


DIFF_QROWS = 2 * Q_BLOCK
DIFF_UNROLL = 2


def _diff_attn_kernel(q_ref, k_ref, v_ref, tb_ref, lam_ref, subln_ref, o_ref, s_scr, *, nq, lam_init):
    qr = DIFF_QROWS
    scale = HEAD_DIM ** -0.5
    lv = lam_ref[...]
    lam = (jnp.exp(jnp.sum(lv[0:1] * lv[1:2], axis=-1, keepdims=True))
           - jnp.exp(jnp.sum(lv[2:3] * lv[3:4], axis=-1, keepdims=True)) + lam_init)
    lane = lax.broadcasted_iota(jnp.int32, (qr, 128), 1)
    zero = jnp.zeros((qr, 128), q_ref.dtype)

    def block(qb, scr):
        r0 = pl.multiple_of(qb * qr, qr)
        q = q_ref[pl.ds(r0, qr), :] * scale
        qz = jnp.concatenate([jnp.where(lane < HEAD_DIM, q, zero),
                              jnp.where(lane >= HEAD_DIM, q, zero)], axis=0)
        for kp in range(nq // 2):
            kblk = k_ref[kp * qr:(kp + 1) * qr, :]
            s = lax.dot_general(qz, kblk, (((1,), (1,)), ((), ())), preferred_element_type=F32)
            for a in range(2):
                d0 = nq - 1 - (2 * qb + a) + 2 * kp
                bias = jnp.concatenate([tb_ref[0, d0], tb_ref[0, d0 + 1]], axis=1)
                for c in range(2):
                    rows = slice(c * qr + a * Q_BLOCK, c * qr + (a + 1) * Q_BLOCK)
                    scr[rows, kp * qr:(kp + 1) * qr] = s[rows] + bias
        s = scr[...]
        m = jnp.max(s, axis=-1, keepdims=True)
        e = jnp.exp(s - m)
        l = jnp.sum(e, axis=-1, keepdims=True)
        pv = jnp.dot(e.astype(BF16), v_ref[...], preferred_element_type=F32) / l
        o = pv[:qr] - lam * pv[qr:]
        o = o * lax.rsqrt(jnp.mean(o * o, axis=-1, keepdims=True) + RMS_EPS) * subln_ref[...]
        o_ref[pl.ds(r0, qr), :] = (o * (1.0 - lam_init)).astype(o_ref.dtype)

    def body(i, carry):
        for j in range(DIFF_UNROLL):
            block(i * DIFF_UNROLL + j, s_scr.at[j])
        return carry

    lax.fori_loop(0, nq // (2 * DIFF_UNROLL), body, 0)


def _diff_attention(za, tb, lam_vecs, subln, batch, seq, layer_idx):
    nq = seq // Q_BLOCK
    lam_init = 0.8 - 0.6 * math.exp(-0.3 * layer_idx)
    kern = functools.partial(_diff_attn_kernel, nq=nq, lam_init=lam_init)
    qcol, kcol, vcol = ZA_AQ // 128, ZA_AK // 128, ZA_AV // 128
    return pl.pallas_call(
        kern,
        grid=(batch, DIFF_HEADS),
        in_specs=[
            pl.BlockSpec((seq, 128), lambda b, h: (b, qcol + h)),
            pl.BlockSpec((seq, 128), lambda b, h: (b, kcol + h)),
            pl.BlockSpec((seq, 128), lambda b, h: (b, vcol + h)),
            pl.BlockSpec((1, 2 * nq - 1, Q_BLOCK, Q_BLOCK), lambda b, h: (h, 0, 0, 0)),
            pl.BlockSpec((4, HEAD_DIM), lambda b, h: (0, 0)),
            pl.BlockSpec((1, DIFF_V), lambda b, h: (0, 0)),
        ],
        out_specs=pl.BlockSpec((seq, 128), lambda b, h: (b, h)),
        out_shape=jax.ShapeDtypeStruct((batch * seq, BRANCH_WIDTH), BF16),
        scratch_shapes=[pltpu.VMEM((DIFF_UNROLL, 2 * DIFF_QROWS, seq), F32)],
        compiler_params=_cparams(2),
        name="diff_attention",
    )(za, za, za, tb, lam_vecs.astype(F32), subln.astype(F32).reshape(1, DIFF_V))


def _swa_kernel(q_ref, k_ref, v_ref, bias_ref, sink_ref, o_ref, kp_scr, vp_scr, *, nb):
    seq = q_ref.shape[0]
    blk = SWA_BLOCK
    pad = jnp.zeros((blk, kp_scr.shape[1]), kp_scr.dtype)
    for scr, ref in ((kp_scr, k_ref), (vp_scr, v_ref)):
        scr[0:blk] = pad
        scr[blk + seq:2 * blk + seq] = pad
        scr[blk:blk + seq] = ref[...]
    col = lax.broadcasted_iota(jnp.int32, (blk, 3 * blk), 1)
    lane = lax.broadcasted_iota(jnp.int32, (blk, 128), 1)
    low = lane < HEAD_DIM
    zero = jnp.zeros((blk, 128), q_ref.dtype)
    scale = HEAD_DIM ** -0.5
    rep = SWA_HEADS // SWA_KV_HEADS

    def body(n, carry):
        r0 = pl.multiple_of(n * blk, blk)
        q = q_ref[pl.ds(r0, blk), :]
        kw = kp_scr[pl.ds(r0, 3 * blk), :]
        vw = vp_scr[pl.ds(r0, 3 * blk), :]
        outside = ((col < blk) & (n == 0)) | ((col >= 2 * blk) & (n == nb - 1))
        halves = []
        for hd in range(SWA_HEADS):
            g = hd // rep
            qc = q[:, (hd // 2) * 128:(hd // 2 + 1) * 128]
            qm = jnp.where(low if hd % 2 == 0 else jnp.logical_not(low), qc, zero)
            s = lax.dot_general(qm, kw[:, g * 128:(g + 1) * 128], (((1,), (1,)), ((), ())),
                                preferred_element_type=F32) * scale
            s = jnp.where(outside, NEG_BIG, s + bias_ref[hd])
            sink = sink_ref[hd][:, 0:1]
            m = jnp.maximum(jnp.max(s, axis=-1, keepdims=True), sink)
            e = jnp.exp(s - m)
            den = jnp.sum(e, axis=-1, keepdims=True) + jnp.exp(sink - m)
            halves.append(jnp.dot(e.astype(BF16), vw[:, g * 128:(g + 1) * 128],
                                  preferred_element_type=F32) / den)
        o = jnp.concatenate([jnp.where(low, halves[2 * c], halves[2 * c + 1]) for c in range(SWA_HEADS // 2)], axis=1)
        o_ref[pl.ds(r0, blk), :] = o.astype(o_ref.dtype)
        return carry

    lax.fori_loop(0, nb, body, 0)


def _swa_attention(za, bias, sink, batch, seq):
    nb = seq // SWA_BLOCK
    kern = functools.partial(_swa_kernel, nb=nb)
    sink_b = jnp.broadcast_to(sink.astype(F32).reshape(SWA_HEADS, 1, 1), (SWA_HEADS, 1, 128))
    kvw = 2 * SWA_KV_HEADS * HEAD_DIM
    return pl.pallas_call(
        kern,
        grid=(batch,),
        in_specs=[
            pl.BlockSpec((seq, BRANCH_WIDTH), lambda b: (b, ZA_CQ // BRANCH_WIDTH)),
            pl.BlockSpec((seq, kvw), lambda b: (b, ZA_CK // kvw)),
            pl.BlockSpec((seq, kvw), lambda b: (b, ZA_CV // kvw)),
            pl.BlockSpec((SWA_HEADS, SWA_BLOCK, 3 * SWA_BLOCK), lambda b: (0, 0, 0)),
            pl.BlockSpec((SWA_HEADS, 1, 128), lambda b: (0, 0, 0)),
        ],
        out_specs=pl.BlockSpec((seq, BRANCH_WIDTH), lambda b: (b, 0)),
        out_shape=jax.ShapeDtypeStruct((batch * seq, BRANCH_WIDTH), BF16),
        scratch_shapes=[pltpu.VMEM((seq + 2 * SWA_BLOCK, kvw), BF16),
                        pltpu.VMEM((seq + 2 * SWA_BLOCK, kvw), BF16)],
        compiler_params=_cparams(1),
        name="swa_attention",
    )(za, za, za, bias, sink_b)


def _shift_rows(x, d):
    rows = x.shape[0]
    t = lax.broadcasted_iota(jnp.int32, x.shape, 0)
    rolled = pltpu.roll(x, (-d) % rows, 0)
    return jnp.where((t + d >= 0) & (t + d < rows), rolled, 0.0)


def _gdn_conv_kernel(x_ref, w_ref, o_ref):
    j = pl.program_id(1)
    x = x_ref[...]
    w = w_ref[...]
    half = GDN_CONV // 2
    acc = x * w[half:half + 1]
    for tap in range(GDN_CONV):
        if tap != half:
            acc = acc + _shift_rows(x, tap - half) * w[tap:tap + 1]
    h = acc * jax.nn.sigmoid(acc)
    inv = lax.rsqrt(jnp.sum(h * h, axis=-1, keepdims=True) + RMS_EPS)
    factor = jnp.where(j < 2 * GDN_HEADS, inv, 1.0) * jnp.where(j < GDN_HEADS, GDN_DK ** -0.5, 1.0)
    o_ref[...] = h * factor


def _gdn_conv(zb, conv_w, batch, seq):
    c0 = ZB_QKV // 128
    return pl.pallas_call(
        _gdn_conv_kernel,
        grid=(batch, GDN_QKV // 128),
        in_specs=[pl.BlockSpec((seq, 128), lambda b, j: (b, c0 + j)),
                  pl.BlockSpec((GDN_CONV, 128), lambda b, j: (0, j))],
        out_specs=pl.BlockSpec((seq, 128), lambda b, j: (b, j)),
        out_shape=jax.ShapeDtypeStruct((batch * seq, GDN_QKV), F32),
        compiler_params=_cparams(2),
        name="gdn_conv",
    )(zb, conv_w.astype(F32))


def _gdn_gate_kernel(ab_ref, alog_ref, dtb_ref, gc_ref, gd_ref, gl_ref, beta_ref):
    ab = ab_ref[...]
    a = ab[0:GDN_CHAINS]
    seq = a.shape[1]
    xs = a + dtb_ref[...]
    softplus = jnp.maximum(xs, 0.0) + jnp.log1p(jnp.exp(-jnp.abs(xs)))
    g = -jnp.exp(alog_ref[...]) * softplus
    beta_ref[...] = jax.nn.sigmoid(ab[GDN_CHAINS:2 * GDN_CHAINS])
    pos = lax.broadcasted_iota(jnp.int32, g.shape, 1) % GDN_CHUNK
    cf = g
    cr = g
    s = 1
    while s < GDN_CHUNK:
        cf = cf + jnp.where(pos >= s, pltpu.roll(cf, s, 1), 0.0)
        cr = cr + jnp.where(pos < GDN_CHUNK - s, pltpu.roll(cr, seq - s, 1), 0.0)
        s *= 2
    fwd = lax.broadcasted_iota(jnp.int32, g.shape, 0) < GDN_HEADS
    gc_ref[...] = jnp.where(fwd, cf, cr)
    gd_ref[...] = jnp.where(fwd, cr, cf) - g
    gl_ref[...] = cf + cr - g


def _gdn_gates(ab_t, a_log, dt_bias, batch, seq):
    row = pl.BlockSpec((GDN_CHAINS, seq), lambda b: (0, b))
    shp = jax.ShapeDtypeStruct((GDN_CHAINS, batch * seq), F32)
    par = pl.BlockSpec((GDN_CHAINS, 1), lambda b: (0, 0))
    return pl.pallas_call(
        _gdn_gate_kernel,
        grid=(batch,),
        in_specs=[pl.BlockSpec((2 * GDN_CHAINS, seq), lambda b: (0, b)), par, par],
        out_specs=[row, row, row, row],
        out_shape=[shp, shp, shp, shp],
        compiler_params=_cparams(1),
        name="gdn_gates",
    )(ab_t, a_log.astype(F32).reshape(GDN_CHAINS, 1), dt_bias.astype(F32).reshape(GDN_CHAINS, 1))


def _hdot(a, b):
    return jnp.dot(a.astype(BF16), b.astype(BF16), preferred_element_type=F32)


def _gdn_local_kernel(qkv_ref, gcc_ref, gdc_ref, betac_ref, gcr_ref,
                      u_ref, w_ref, qg_ref, qk_ref, kgt_ref):
    g = GDN_GROUP
    c = GDN_CHUNK
    ri = lax.broadcasted_iota(jnp.int32, (g, g), 0)
    ci = lax.broadcasted_iota(jnp.int32, (g, g), 1)
    same = (ri // c) == (ci // c)
    ahead = jnp.where(same, ri - ci, -g)
    behind = jnp.where(same, ci - ri, -g)
    eye_s = (lax.broadcasted_iota(jnp.int32, (c, g), 0)
             == lax.broadcasted_iota(jnp.int32, (c, g), 1) % c).astype(F32)

    def block_diag(xs):
        return jnp.where(same, jnp.concatenate([xs] * (g // c), axis=0), 0.0)

    def row_blocks_sum(xd):
        out = xd[0:c]
        for i in range(1, g // c):
            out = out + xd[i * c:(i + 1) * c]
        return out

    for h in range(GDN_HEADS):
        q = qkv_ref[:, h * GDN_DK:(h + 1) * GDN_DK]
        k = qkv_ref[:, (GDN_HEADS + h) * GDN_DK:(GDN_HEADS + h + 1) * GDN_DK]
        v = qkv_ref[:, (2 * GDN_HEADS + h) * GDN_DK:(2 * GDN_HEADS + h + 1) * GDN_DK]
        kbf = k.astype(BF16)
        qk_raw = lax.dot_general(q.astype(BF16), kbf, (((1,), (1,)), ((), ())), preferred_element_type=F32)
        for d in range(2):
            ch = d * GDN_HEADS + h
            gcc = gcc_ref[:, ch:ch + 1]
            gdc = gdc_ref[:, ch:ch + 1]
            beta = betac_ref[:, ch:ch + 1]
            gcr = gcr_ref[ch:ch + 1, :]
            order = ahead if d == 0 else behind
            kb = k * beta
            kk = lax.dot_general(kb.astype(BF16), kbf, (((1,), (1,)), ((), ())), preferred_element_type=F32)
            decay = jnp.exp(jnp.where(order >= 0, gcc - gcr, NEG_BIG))
            xd = -jnp.where(order > 0, kk * decay, 0.0)
            qkm = qk_raw * decay
            xs = row_blocks_sum(xd)
            p = eye_s + xs
            xs = _hdot(xs, xd)
            for _ in range(4):
                r = _hdot(jnp.concatenate([p, xs], axis=0), block_diag(xs))
                p = p + r[0:c]
                xs = r[c:2 * c]
            p = p + _hdot(p, block_diag(xs))
            rhs = jnp.concatenate([v * beta, kb * jnp.exp(gcc)], axis=1)
            sol = _hdot(block_diag(p), rhs)
            lo, hi = ch * GDN_DK, (ch + 1) * GDN_DK
            u_ref[:, lo:hi] = sol[:, 0:GDN_DV]
            w_ref[:, lo:hi] = sol[:, GDN_DV:].astype(w_ref.dtype)
            qg_ref[:, lo:hi] = (q * jnp.exp(gcc)).astype(qg_ref.dtype)
            qk_ref[:, lo:hi] = (qkm[:, 0:128] + qkm[:, 128:256]).astype(qk_ref.dtype)
            kgt_ref[0, lo:hi, :] = (k * jnp.exp(gdc)).T.astype(kgt_ref.dtype)


def _gdn_local(qkvn, gcc, gdc, betac, gcr, batch, seq):
    ng = seq // GDN_GROUP
    wide = GDN_CHAINS * GDN_DK
    colb = pl.BlockSpec((GDN_GROUP, GDN_CHAINS), lambda b, s: (b * ng + s, 0))
    outb = pl.BlockSpec((GDN_GROUP, wide), lambda b, s: (b * ng + s, 0))
    n = batch * seq
    return pl.pallas_call(
        _gdn_local_kernel,
        grid=(batch, ng),
        in_specs=[pl.BlockSpec((GDN_GROUP, GDN_QKV), lambda b, s: (b * ng + s, 0)),
                  colb, colb, colb,
                  pl.BlockSpec((GDN_CHAINS, GDN_GROUP), lambda b, s: (0, b * ng + s))],
        out_specs=[outb, outb, outb, outb,
                   pl.BlockSpec((1, wide, GDN_GROUP), lambda b, s: (b, 0, s))],
        out_shape=[jax.ShapeDtypeStruct((n, wide), F32),
                   jax.ShapeDtypeStruct((n, wide), BF16),
                   jax.ShapeDtypeStruct((n, wide), BF16),
                   jax.ShapeDtypeStruct((n, wide), BF16),
                   jax.ShapeDtypeStruct((batch, wide, seq), BF16)],
        compiler_params=_cparams(2),
        name="gdn_local",
    )(qkvn, gcc, gdc, betac, gcr)


def _gdn_scan_kernel(uf_ref, ub_ref, wf_ref, wb_ref, qgf_ref, qgb_ref, qkf_ref, qkb_ref,
                     kgf_ref, kgb_ref, glf_ref, glb_ref, of_ref, ob_ref, s_scr):
    @pl.when(pl.program_id(1) == 0)
    def _():
        s_scr[...] = jnp.zeros_like(s_scr)

    c = GDN_CHUNK
    nc = GDN_GROUP // c
    zeros = jnp.zeros((c, GDN_DV), BF16)
    dirs = ((uf_ref, wf_ref, qgf_ref, qkf_ref, kgf_ref, glf_ref, of_ref),
            (ub_ref, wb_ref, qgb_ref, qkb_ref, kgb_ref, glb_ref, ob_ref))
    states = [s_scr[ch] for ch in range(GDN_CHAINS)]
    for step in range(nc):
        for d, (u_ref, w_ref, qg_ref, qk_ref, kg_ref, gl_ref, o_ref) in enumerate(dirs):
            i = step if d == 0 else nc - 1 - step
            rows = slice(i * c, (i + 1) * c)
            pair = slice((i // 2) * 2 * c, (i // 2 + 1) * 2 * c)
            for h in range(GDN_HEADS):
                ch = d * GDN_HEADS + h
                lo, hi = h * GDN_DK, (h + 1) * GDN_DK
                sb = states[ch].astype(BF16)
                top = jnp.dot(jnp.concatenate([w_ref[rows, lo:hi], qg_ref[rows, lo:hi]], axis=0), sb,
                              preferred_element_type=F32)
                vb = (u_ref[rows, lo:hi] - top[0:c]).astype(BF16)
                v2 = jnp.concatenate([vb, zeros] if i % 2 == 0 else [zeros, vb], axis=0)
                bot = jnp.dot(jnp.concatenate([qk_ref[rows, lo:hi], kg_ref[0, lo:hi, pair]], axis=0), v2,
                              preferred_element_type=F32)
                o_ref[rows, lo:hi] = top[c:2 * c] + bot[0:c]
                decay = jnp.exp(gl_ref[ch:ch + 1, i * c:i * c + 1])
                states[ch] = states[ch] * decay + bot[c:c + GDN_DK]
    for ch in range(GDN_CHAINS):
        s_scr[ch] = states[ch]


def _gdn_scan(u, w, qg, qk, kgt, gl, batch, seq):
    ng = seq // GDN_GROUP
    half = GDN_HEADS * GDN_DK
    fwd = lambda b, s: (b * ng + s, 0)
    bwd = lambda b, s: (b * ng + ng - 1 - s, 1)
    blk = (GDN_GROUP, half)
    n = batch * seq
    pair = lambda arr: (arr, arr)
    out_f = pl.BlockSpec(blk, fwd)
    out_b = pl.BlockSpec(blk, lambda b, s: (b * ng + ng - 1 - s, 0))
    in_specs = []
    for _ in range(4):
        in_specs += [pl.BlockSpec(blk, fwd), pl.BlockSpec(blk, bwd)]
    in_specs += [pl.BlockSpec((1, half, GDN_GROUP), lambda b, s: (b, 0, s)),
                 pl.BlockSpec((1, half, GDN_GROUP), lambda b, s: (b, 1, ng - 1 - s)),
                 pl.BlockSpec((GDN_CHAINS, GDN_GROUP), lambda b, s: (0, b * ng + s)),
                 pl.BlockSpec((GDN_CHAINS, GDN_GROUP), lambda b, s: (0, b * ng + ng - 1 - s))]
    return pl.pallas_call(
        _gdn_scan_kernel,
        grid=(batch, ng),
        in_specs=in_specs,
        out_specs=[out_f, out_b],
        out_shape=[jax.ShapeDtypeStruct((n, half), F32), jax.ShapeDtypeStruct((n, half), F32)],
        scratch_shapes=[pltpu.VMEM((GDN_CHAINS, GDN_DK, GDN_DV), F32)],
        compiler_params=_cparams(2),
        name="gdn_scan",
    )(*pair(u), *pair(w), *pair(qg), *pair(qk), *pair(kgt), *pair(gl))


def _layer_norm(r, g, b):
    mu = jnp.mean(r, axis=-1, keepdims=True)
    var = jnp.mean(jnp.square(r - mu), axis=-1, keepdims=True)
    return (r - mu) * lax.rsqrt(var + LN_EPS) * g + b


def _merge_kernel(ya_ref, of_ref, ob_ref, gate_ref, yc_ref, gz_ref, x_ref, wb_ref, wo_ref,
                  nw_ref, lng_ref, lnb_ref, o_ref):
    segs = []
    for h in range(GDN_HEADS):
        sl = slice(h * GDN_DV, (h + 1) * GDN_DV)
        o = of_ref[:, sl] + ob_ref[:, sl]
        o = o * lax.rsqrt(jnp.mean(o * o, axis=-1, keepdims=True) + RMS_EPS) * nw_ref[...]
        gt = gate_ref[:, sl]
        segs.append(o * (gt * jax.nn.sigmoid(gt)))
    yb = jnp.concatenate(segs, axis=1).astype(BF16)
    merged = None
    for n, y in enumerate((ya_ref[...], yb, yc_ref[...])):
        proj = jnp.dot(y, wb_ref[n], preferred_element_type=F32)
        term = jax.nn.sigmoid(gz_ref[:, n * D_MODEL:(n + 1) * D_MODEL]) * proj
        merged = term if merged is None else merged + term
    hmix = jnp.dot(merged.astype(BF16), wo_ref[...], preferred_element_type=F32)
    o_ref[...] = _layer_norm(DEEPNORM_ALPHA * x_ref[...] + hmix, lng_ref[...], lnb_ref[...])


def _merge(ya, o_f, o_b, zb, yc, x, w_branch, w_out, norm_w, ln_g, ln_b, tm):
    n = x.shape[0]
    row = lambda width, col=0: pl.BlockSpec((tm, width), lambda i, col=col: (i, col))
    full = lambda shape: pl.BlockSpec(shape, lambda i: (0,) * len(shape))
    return pl.pallas_call(
        _merge_kernel,
        grid=(n // tm,),
        in_specs=[row(BRANCH_WIDTH), row(BRANCH_WIDTH), row(BRANCH_WIDTH),
                  row(BRANCH_WIDTH, ZB_GATE // BRANCH_WIDTH), row(BRANCH_WIDTH),
                  row(N_BRANCHES * D_MODEL, ZB_GZ // (N_BRANCHES * D_MODEL)), row(D_MODEL),
                  full((N_BRANCHES, BRANCH_WIDTH, D_MODEL)), full((D_MODEL, D_MODEL)),
                  full((1, GDN_DV)), full((1, D_MODEL)), full((1, D_MODEL))],
        out_specs=row(D_MODEL),
        out_shape=jax.ShapeDtypeStruct((n, D_MODEL), F32),
        compiler_params=_cparams(1),
        name="merge_ln",
    )(ya, o_f, o_b, zb, yc, zb, x, w_branch, w_out, norm_w.astype(F32).reshape(1, GDN_DV),
      ln_g.astype(F32).reshape(1, D_MODEL), ln_b.astype(F32).reshape(1, D_MODEL))


FFN_SPLIT = 2


def _ffn_kernel(xp_ref, x_ref, xn_ref, wup_ref, cw_ref, cb_ref, wdn_ref, lng_ref, lnb_ref, o_ref, *, tiles_per_seq):
    i = pl.program_id(0)
    tm = x_ref.shape[0]
    halo = V7X_SUBLANES
    x = x_ref[...]
    xe = jnp.concatenate([xp_ref[...], x, xn_ref[...]], axis=0).astype(BF16)
    xb = x.astype(BF16)
    rows = lax.broadcasted_iota(jnp.int32, (tm + 2 * halo, 1), 0)
    first = (i % tiles_per_seq) == 0
    last = (i % tiles_per_seq) == tiles_per_seq - 1
    keep = jnp.logical_not(((rows < halo) & first) | ((rows >= tm + halo) & last))
    fc = D_FF // FFN_SPLIT
    ext = tm + 2 * halo
    acc = None
    for c in range(FFN_SPLIT):
        ge = jnp.dot(xe, wup_ref[:, c * fc:(c + 1) * fc], preferred_element_type=F32)
        ge = jnp.where(keep, ge, 0.0)
        up = jnp.dot(xb, wup_ref[:, D_FF + c * fc:D_FF + (c + 1) * fc], preferred_element_type=F32)
        cw = cw_ref[:, c * fc:(c + 1) * fc]
        conv = (pltpu.roll(ge, 1, 0) * cw[0:1] + ge * cw[1:2] + pltpu.roll(ge, ext - 1, 0) * cw[2:3])
        gate = conv[halo:halo + tm] + cb_ref[:, c * fc:(c + 1) * fc]
        act = (gate * jax.nn.sigmoid(gate) * up).astype(BF16)
        part = jnp.dot(act, wdn_ref[c * fc:(c + 1) * fc, :], preferred_element_type=F32)
        acc = part if acc is None else acc + part
    o_ref[...] = _layer_norm(DEEPNORM_ALPHA * x + acc, lng_ref[...], lnb_ref[...])


def _ffn(x, w_up, conv_w, conv_b, w_down, ln_g, ln_b, seq, tm):
    n = x.shape[0]
    halo = V7X_SUBLANES
    per = tm // halo
    nh = n // halo
    full = lambda shape: pl.BlockSpec(shape, lambda i: (0,) * len(shape))
    kern = functools.partial(_ffn_kernel, tiles_per_seq=seq // tm)
    return pl.pallas_call(
        kern,
        grid=(n // tm,),
        in_specs=[pl.BlockSpec((halo, D_MODEL), lambda i: (jnp.maximum(i * per - 1, 0), 0)),
                  pl.BlockSpec((tm, D_MODEL), lambda i: (i, 0)),
                  pl.BlockSpec((halo, D_MODEL), lambda i: (jnp.minimum((i + 1) * per, nh - 1), 0)),
                  full((D_MODEL, 2 * D_FF)), full((FFN_CONV, D_FF)), full((1, D_FF)),
                  full((D_FF, D_MODEL)), full((1, D_MODEL)), full((1, D_MODEL))],
        out_specs=pl.BlockSpec((tm, D_MODEL), lambda i: (i, 0)),
        out_shape=jax.ShapeDtypeStruct((n, D_MODEL), F32),
        compiler_params=_cparams(1),
        name="ffn_ln",
    )(x, x, x, w_up, conv_w.astype(F32), conv_b.astype(F32).reshape(1, D_FF), w_down,
      ln_g.astype(F32).reshape(1, D_MODEL), ln_b.astype(F32).reshape(1, D_MODEL))


def _pack_w_in(w):
    widths = (512, 512, 512, GDN_QKV, 512, 8, 8, 512, 128, 128, N_BRANCHES * D_MODEL)
    offs = [0]
    for wd in widths:
        offs.append(offs[-1] + wd)
    seg = lambda i: w[:, offs[i]:offs[i + 1]]
    aq, ak, av, bqkv, bgate, ba, bb, cq, ck, cv, gz = (seg(i) for i in range(len(widths)))
    dup = lambda t: jnp.concatenate([t[:, :HEAD_DIM], t[:, :HEAD_DIM], t[:, HEAD_DIM:], t[:, HEAD_DIM:]], axis=1)
    w_a = jnp.concatenate([aq, ak, av, cq, dup(ck), dup(cv)], axis=1)
    pad = jnp.zeros((w.shape[0], ZB_COLS - ZB_AB - 16), w.dtype)
    w_b = jnp.concatenate([gz, bqkv, bgate, ba, bb, pad], axis=1)
    return w_a.astype(BF16), w_b.astype(BF16)


def kernel(x, rel_bias, w_in, diff_lambda, diff_subln, gdn_conv, gdn_a_log, gdn_dt_bias, gdn_norm, swa_sink,
           w_branch, w_out, ln1_g, ln1_b, ffn_up, ffn_conv, ffn_conv_b, ffn_down, ln2_g, ln2_b):
    batch, seq, d = x.shape
    assert d == D_MODEL and seq % GDN_GROUP == 0 and seq % (DIFF_QROWS * DIFF_UNROLL) == 0
    n = batch * seq
    tm = 256
    tb_diff = _diff_bias_tiles(rel_bias, seq)
    tb_swa = _swa_bias_tiles(rel_bias)
    xf = x.reshape(n, d)
    for l in range(DEPTH):
        w_a, w_b = _pack_w_in(w_in[l])
        za = _matmul(xf, w_a, BF16, 512, ZA_COLS // 2, "in_proj_a")
        zb = _matmul(xf, w_b, F32, 512, ZB_COLS // 3, "in_proj_b")
        ya = _diff_attention(za, tb_diff, diff_lambda[l], diff_subln[l], batch, seq, l)
        yc = _swa_attention(za, tb_swa, swa_sink[l], batch, seq)
        qkvn = _gdn_conv(zb, gdn_conv[l], batch, seq)
        ab_t = zb[:, ZB_AB:ZB_AB + 2 * GDN_CHAINS].T
        gc, gd, gl, beta = _gdn_gates(ab_t, gdn_a_log[l], gdn_dt_bias[l], batch, seq)
        u, w, qg, qk, kgt = _gdn_local(qkvn, gc.T, gd.T, beta.T, gc, batch, seq)
        o_f, o_b = _gdn_scan(u, w, qg, qk, kgt, gl, batch, seq)
        xf = _merge(ya, o_f, o_b, zb, yc, xf, w_branch[l].astype(BF16), w_out[l].astype(BF16),
                    gdn_norm[l], ln1_g[l], ln1_b[l], tm)
        xf = _ffn(xf, ffn_up[l].astype(BF16), ffn_conv[l], ffn_conv_b[l], ffn_down[l].astype(BF16),
                  ln2_g[l], ln2_b[l], seq, tm)
    return xf.reshape(batch, seq, d)
```

```python
import functools
import math

import jax
import jax.numpy as jnp
from jax import lax
from jax.experimental import pallas as pl
from jax.experimental.pallas import tpu as pltpu

F32 = jnp.float32
BF16 = jnp.bfloat16

D_MODEL = 1024
DEPTH = 2
HEAD_DIM = 64
DIFF_HEADS = 4
DIFF_V = 2 * HEAD_DIM
Q_BLOCK = 128
GDN_HEADS = 4
GDN_DK = 128
GDN_DV = 128
GDN_QKV = GDN_HEADS * (2 * GDN_DK + GDN_DV)
GDN_CONV = 5
GDN_CHUNK = 64
SWA_HEADS = 8
SWA_KV_HEADS = 2
SWA_WINDOW = 128
SWA_BLOCK = 128
BRANCH_WIDTH = 512
N_BRANCHES = 3
REL_BUCKETS = 32
REL_MAX_DIST = 128
D_FF = 2816
FFN_CONV = 3
DEEPNORM_ALPHA = (2 * DEPTH) ** 0.25
LN_EPS = 1e-5
RMS_EPS = 1e-6

V7X_LANES = 128
V7X_SUBLANES = 8
V7X_VMEM_BYTES = 64 * 1024 * 1024
VMEM_LIMIT = V7X_VMEM_BYTES * 7 // 8

NEG_BIG = -1e30

ZA_AQ, ZA_AK, ZA_AV = 0, 512, 1024
ZA_CQ = 1536
ZA_CK = 2048
ZA_CV = 2304
ZA_COLS = 2560
ZB_GZ = 0
ZB_QKV = 3072
ZB_GATE = 4608
ZB_AB = 5120
ZB_COLS = 5376

GDN_GROUP = 256
GDN_CHAINS = 2 * GDN_HEADS


def _cparams(n_grid):
    return pltpu.CompilerParams(dimension_semantics=("arbitrary",) * n_grid, vmem_limit_bytes=VMEM_LIMIT)


def _t5_bucket(rel):
    nb = REL_BUCKETS // 2
    ret = jnp.where(rel > 0, nb, 0)
    n = jnp.abs(rel)
    max_exact = nb // 2
    large = max_exact + (jnp.log(jnp.maximum(n, 1).astype(jnp.float32) / max_exact)
                         / math.log(REL_MAX_DIST / max_exact) * (nb - max_exact)).astype(jnp.int32)
    large = jnp.minimum(large, nb - 1)
    return ret + jnp.where(n < max_exact, n, large)


def _table_lookup(table, bucket):
    out = jnp.zeros((table.shape[1],) + bucket.shape, F32)
    for i in range(REL_BUCKETS):
        out = jnp.where(bucket[None] == i, table[i].astype(F32).reshape((-1,) + (1,) * bucket.ndim), out)
    return out


def _diff_bias_tiles(rel_bias, seq):
    nq = seq // Q_BLOCK
    d = jnp.arange(2 * nq - 1) - (nq - 1)
    r = jnp.arange(Q_BLOCK)
    rel = d[:, None, None] * Q_BLOCK + r[None, None, :] - r[None, :, None]
    return _table_lookup(rel_bias[:, :DIFF_HEADS], _t5_bucket(rel))


def _swa_bias_tiles(rel_bias):
    kb = 3 * SWA_BLOCK
    rel = jnp.arange(kb)[None, :] - SWA_WINDOW - jnp.arange(SWA_BLOCK)[:, None]
    bias = _table_lookup(rel_bias[:, DIFF_HEADS:], _t5_bucket(rel))
    return jnp.where((jnp.abs(rel) <= SWA_WINDOW)[None], bias, NEG_BIG)


def _matmul_kernel(x_ref, w_ref, o_ref):
    o_ref[...] = jnp.dot(x_ref[...].astype(BF16), w_ref[...],
                         preferred_element_type=F32).astype(o_ref.dtype)


def _matmul(x, w, out_dtype, tm, tn, name):
    m, k = x.shape
    n = w.shape[1]
    return pl.pallas_call(
        _matmul_kernel,
        grid=(n // tn, m // tm),
        in_specs=[pl.BlockSpec((tm, k), lambda j, i: (i, 0)),
                  pl.BlockSpec((k, tn), lambda j, i: (0, j))],
        out_specs=pl.BlockSpec((tm, tn), lambda j, i: (i, j)),
        out_shape=jax.ShapeDtypeStruct((m, n), out_dtype),
        compiler_params=_cparams(2),
        name=name,
    )(x, w)


DIFF_QROWS = 2 * Q_BLOCK
DIFF_UNROLL = 2


def _diff_attn_kernel(q_ref, k_ref, v_ref, tb_ref, lam_ref, subln_ref, o_ref, s_scr, *, nq, lam_init):
    qr = DIFF_QROWS
    scale = HEAD_DIM ** -0.5
    lv = lam_ref[...]
    lam = (jnp.exp(jnp.sum(lv[0:1] * lv[1:2], axis=-1, keepdims=True))
           - jnp.exp(jnp.sum(lv[2:3] * lv[3:4], axis=-1, keepdims=True)) + lam_init)
    lane = lax.broadcasted_iota(jnp.int32, (qr, 128), 1)
    zero = jnp.zeros((qr, 128), q_ref.dtype)

    def block(qb, scr):
        r0 = pl.multiple_of(qb * qr, qr)
        q = q_ref[pl.ds(r0, qr), :] * scale
        qz = jnp.concatenate([jnp.where(lane < HEAD_DIM, q, zero),
                              jnp.where(lane >= HEAD_DIM, q, zero)], axis=0)
        for kp in range(nq // 2):
            kblk = k_ref[kp * qr:(kp + 1) * qr, :]
            s = lax.dot_general(qz, kblk, (((1,), (1,)), ((), ())), preferred_element_type=F32)
            for a in range(2):
                d0 = nq - 1 - (2 * qb + a) + 2 * kp
                bias = jnp.concatenate([tb_ref[0, d0], tb_ref[0, d0 + 1]], axis=1)
                for c in range(2):
                    rows = slice(c * qr + a * Q_BLOCK, c * qr + (a + 1) * Q_BLOCK)
                    scr[rows, kp * qr:(kp + 1) * qr] = s[rows] + bias
        s = scr[...]
        m = jnp.max(s, axis=-1, keepdims=True)
        e = jnp.exp(s - m)
        l = jnp.sum(e, axis=-1, keepdims=True)
        pv = jnp.dot(e.astype(BF16), v_ref[...], preferred_element_type=F32) / l
        o = pv[:qr] - lam * pv[qr:]
        o = o * lax.rsqrt(jnp.mean(o * o, axis=-1, keepdims=True) + RMS_EPS) * subln_ref[...]
        o_ref[pl.ds(r0, qr), :] = (o * (1.0 - lam_init)).astype(o_ref.dtype)

    def body(i, carry):
        for j in range(DIFF_UNROLL):
            block(i * DIFF_UNROLL + j, s_scr.at[j])
        return carry

    lax.fori_loop(0, nq // (2 * DIFF_UNROLL), body, 0)


def _diff_attention(za, tb, lam_vecs, subln, batch, seq, layer_idx):
    nq = seq // Q_BLOCK
    lam_init = 0.8 - 0.6 * math.exp(-0.3 * layer_idx)
    kern = functools.partial(_diff_attn_kernel, nq=nq, lam_init=lam_init)
    qcol, kcol, vcol = ZA_AQ // 128, ZA_AK // 128, ZA_AV // 128
    return pl.pallas_call(
        kern,
        grid=(batch, DIFF_HEADS),
        in_specs=[
            pl.BlockSpec((seq, 128), lambda b, h: (b, qcol + h)),
            pl.BlockSpec((seq, 128), lambda b, h: (b, kcol + h)),
            pl.BlockSpec((seq, 128), lambda b, h: (b, vcol + h)),
            pl.BlockSpec((1, 2 * nq - 1, Q_BLOCK, Q_BLOCK), lambda b, h: (h, 0, 0, 0)),
            pl.BlockSpec((4, HEAD_DIM), lambda b, h: (0, 0)),
            pl.BlockSpec((1, DIFF_V), lambda b, h: (0, 0)),
        ],
        out_specs=pl.BlockSpec((seq, 128), lambda b, h: (b, h)),
        out_shape=jax.ShapeDtypeStruct((batch * seq, BRANCH_WIDTH), BF16),
        scratch_shapes=[pltpu.VMEM((DIFF_UNROLL, 2 * DIFF_QROWS, seq), F32)],
        compiler_params=_cparams(2),
        name="diff_attention",
    )(za, za, za, tb, lam_vecs.astype(F32), subln.astype(F32).reshape(1, DIFF_V))


def _swa_kernel(q_ref, k_ref, v_ref, bias_ref, sink_ref, o_ref, kp_scr, vp_scr, *, nb):
    seq = q_ref.shape[0]
    blk = SWA_BLOCK
    pad = jnp.zeros((blk, kp_scr.shape[1]), kp_scr.dtype)
    for scr, ref in ((kp_scr, k_ref), (vp_scr, v_ref)):
        scr[0:blk] = pad
        scr[blk + seq:2 * blk + seq] = pad
        scr[blk:blk + seq] = ref[...]
    col = lax.broadcasted_iota(jnp.int32, (blk, 3 * blk), 1)
    lane = lax.broadcasted_iota(jnp.int32, (blk, 128), 1)
    low = lane < HEAD_DIM
    zero = jnp.zeros((blk, 128), q_ref.dtype)
    scale = HEAD_DIM ** -0.5
    rep = SWA_HEADS // SWA_KV_HEADS

    def body(n, carry):
        r0 = pl.multiple_of(n * blk, blk)
        q = q_ref[pl.ds(r0, blk), :]
        kw = kp_scr[pl.ds(r0, 3 * blk), :]
        vw = vp_scr[pl.ds(r0, 3 * blk), :]
        outside = ((col < blk) & (n == 0)) | ((col >= 2 * blk) & (n == nb - 1))
        heads = range(SWA_HEADS)
        qms = [jnp.where(low if hd % 2 == 0 else jnp.logical_not(low),
                         q[:, (hd // 2) * 128:(hd // 2 + 1) * 128] * scale, zero) for hd in heads]
        ss = [lax.dot_general(qms[hd], kw[:, (hd // rep) * 128:(hd // rep + 1) * 128], (((1,), (1,)), ((), ())),
                              preferred_element_type=F32) for hd in heads]
        ss = [jnp.where(outside, NEG_BIG, ss[hd] + bias_ref[hd]) for hd in heads]
        sinks = [sink_ref[hd][:, 0:1] for hd in heads]
        ms = [jnp.maximum(jnp.max(ss[hd], axis=-1, keepdims=True), sinks[hd]) for hd in heads]
        es = [jnp.exp(ss[hd] - ms[hd]) for hd in heads]
        dens = [jnp.sum(es[hd], axis=-1, keepdims=True) + jnp.exp(sinks[hd] - ms[hd]) for hd in heads]
        pvs = [jnp.dot(es[hd].astype(BF16), vw[:, (hd // rep) * 128:(hd // rep + 1) * 128],
                       preferred_element_type=F32) for hd in heads]
        halves = [pvs[hd] / dens[hd] for hd in heads]
        o = jnp.concatenate([jnp.where(low, halves[2 * c], halves[2 * c + 1]) for c in range(SWA_HEADS // 2)], axis=1)
        o_ref[pl.ds(r0, blk), :] = o.astype(o_ref.dtype)
        return carry

    lax.fori_loop(0, nb, body, 0)


def _swa_attention(za, bias, sink, batch, seq):
    nb = seq // SWA_BLOCK
    kern = functools.partial(_swa_kernel, nb=nb)
    sink_b = jnp.broadcast_to(sink.astype(F32).reshape(SWA_HEADS, 1, 1), (SWA_HEADS, 1, 128))
    kvw = 2 * SWA_KV_HEADS * HEAD_DIM
    return pl.pallas_call(
        kern,
        grid=(batch,),
        in_specs=[
            pl.BlockSpec((seq, BRANCH_WIDTH), lambda b: (b, ZA_CQ // BRANCH_WIDTH)),
            pl.BlockSpec((seq, kvw), lambda b: (b, ZA_CK // kvw)),
            pl.BlockSpec((seq, kvw), lambda b: (b, ZA_CV // kvw)),
            pl.BlockSpec((SWA_HEADS, SWA_BLOCK, 3 * SWA_BLOCK), lambda b: (0, 0, 0)),
            pl.BlockSpec((SWA_HEADS, 1, 128), lambda b: (0, 0, 0)),
        ],
        out_specs=pl.BlockSpec((seq, BRANCH_WIDTH), lambda b: (b, 0)),
        out_shape=jax.ShapeDtypeStruct((batch * seq, BRANCH_WIDTH), BF16),
        scratch_shapes=[pltpu.VMEM((seq + 2 * SWA_BLOCK, kvw), BF16),
                        pltpu.VMEM((seq + 2 * SWA_BLOCK, kvw), BF16)],
        compiler_params=_cparams(1),
        name="swa_attention",
    )(za, za, za, bias, sink_b)


def _shift_rows(x, d):
    rows = x.shape[0]
    t = lax.broadcasted_iota(jnp.int32, x.shape, 0)
    rolled = pltpu.roll(x, (-d) % rows, 0)
    return jnp.where((t + d >= 0) & (t + d < rows), rolled, 0.0)


def _gdn_conv_kernel(x_ref, w_ref, o_ref):
    j = pl.program_id(1)
    x = x_ref[...]
    w = w_ref[...]
    half = GDN_CONV // 2
    acc = x * w[half:half + 1]
    for tap in range(GDN_CONV):
        if tap != half:
            acc = acc + _shift_rows(x, tap - half) * w[tap:tap + 1]
    h = acc * jax.nn.sigmoid(acc)
    inv = lax.rsqrt(jnp.sum(h * h, axis=-1, keepdims=True) + RMS_EPS)
    factor = jnp.where(j < 2 * GDN_HEADS, inv, 1.0) * jnp.where(j < GDN_HEADS, GDN_DK ** -0.5, 1.0)
    o_ref[...] = h * factor


def _gdn_conv(zb, conv_w, batch, seq):
    c0 = ZB_QKV // 128
    return pl.pallas_call(
        _gdn_conv_kernel,
        grid=(batch, GDN_QKV // 128),
        in_specs=[pl.BlockSpec((seq, 128), lambda b, j: (b, c0 + j)),
                  pl.BlockSpec((GDN_CONV, 128), lambda b, j: (0, j))],
        out_specs=pl.BlockSpec((seq, 128), lambda b, j: (b, j)),
        out_shape=jax.ShapeDtypeStruct((batch * seq, GDN_QKV), F32),
        compiler_params=_cparams(2),
        name="gdn_conv",
    )(zb, conv_w.astype(F32))


def _gdn_gate_kernel(ab_ref, alog_ref, dtb_ref, gc_ref, gd_ref, gl_ref, beta_ref):
    ab = ab_ref[...]
    a = ab[0:GDN_CHAINS]
    seq = a.shape[1]
    xs = a + dtb_ref[...]
    softplus = jnp.maximum(xs, 0.0) + jnp.log1p(jnp.exp(-jnp.abs(xs)))
    g = -jnp.exp(alog_ref[...]) * softplus
    beta_ref[...] = jax.nn.sigmoid(ab[GDN_CHAINS:2 * GDN_CHAINS])
    pos = lax.broadcasted_iota(jnp.int32, g.shape, 1) % GDN_CHUNK
    cf = g
    cr = g
    s = 1
    while s < GDN_CHUNK:
        cf = cf + jnp.where(pos >= s, pltpu.roll(cf, s, 1), 0.0)
        cr = cr + jnp.where(pos < GDN_CHUNK - s, pltpu.roll(cr, seq - s, 1), 0.0)
        s *= 2
    fwd = lax.broadcasted_iota(jnp.int32, g.shape, 0) < GDN_HEADS
    gc_ref[...] = jnp.where(fwd, cf, cr)
    gd_ref[...] = jnp.where(fwd, cr, cf) - g
    gl_ref[...] = cf + cr - g


def _gdn_gates(ab_t, a_log, dt_bias, batch, seq):
    row = pl.BlockSpec((GDN_CHAINS, seq), lambda b: (0, b))
    shp = jax.ShapeDtypeStruct((GDN_CHAINS, batch * seq), F32)
    par = pl.BlockSpec((GDN_CHAINS, 1), lambda b: (0, 0))
    return pl.pallas_call(
        _gdn_gate_kernel,
        grid=(batch,),
        in_specs=[pl.BlockSpec((2 * GDN_CHAINS, seq), lambda b: (0, b)), par, par],
        out_specs=[row, row, row, row],
        out_shape=[shp, shp, shp, shp],
        compiler_params=_cparams(1),
        name="gdn_gates",
    )(ab_t, a_log.astype(F32).reshape(GDN_CHAINS, 1), dt_bias.astype(F32).reshape(GDN_CHAINS, 1))


def _hdot(a, b):
    return jnp.dot(a.astype(BF16), b.astype(BF16), preferred_element_type=F32)


def _gdn_local_kernel(qkv_ref, gcc_ref, gdc_ref, betac_ref, gcr_ref,
                      u_ref, w_ref, qg_ref, qk_ref, kgt_ref):
    g = GDN_GROUP
    c = GDN_CHUNK
    ri = lax.broadcasted_iota(jnp.int32, (g, g), 0)
    ci = lax.broadcasted_iota(jnp.int32, (g, g), 1)
    same = (ri // c) == (ci // c)
    ahead = jnp.where(same, ri - ci, -g)
    behind = jnp.where(same, ci - ri, -g)
    eye_s = (lax.broadcasted_iota(jnp.int32, (c, g), 0)
             == lax.broadcasted_iota(jnp.int32, (c, g), 1) % c).astype(F32)

    def block_diag(xs):
        return jnp.where(same, jnp.concatenate([xs] * (g // c), axis=0), 0.0)

    def row_blocks_sum(xd):
        out = xd[0:c]
        for i in range(1, g // c):
            out = out + xd[i * c:(i + 1) * c]
        return out

    chains = [(h, d) for h in range(GDN_HEADS) for d in range(2)]
    xs_l, xd_l, p_l, rhs_l = [], [], [], []
    for h in range(GDN_HEADS):
        q = qkv_ref[:, h * GDN_DK:(h + 1) * GDN_DK]
        k = qkv_ref[:, (GDN_HEADS + h) * GDN_DK:(GDN_HEADS + h + 1) * GDN_DK]
        v = qkv_ref[:, (2 * GDN_HEADS + h) * GDN_DK:(2 * GDN_HEADS + h + 1) * GDN_DK]
        kbf = k.astype(BF16)
        qk_raw = lax.dot_general(q.astype(BF16), kbf, (((1,), (1,)), ((), ())), preferred_element_type=F32)
        for d in range(2):
            ch = d * GDN_HEADS + h
            lo, hi = ch * GDN_DK, (ch + 1) * GDN_DK
            gcc = gcc_ref[:, ch:ch + 1]
            gdc = gdc_ref[:, ch:ch + 1]
            beta = betac_ref[:, ch:ch + 1]
            gcr = gcr_ref[ch:ch + 1, :]
            order = ahead if d == 0 else behind
            kb = k * beta
            kk = lax.dot_general(kb.astype(BF16), kbf, (((1,), (1,)), ((), ())), preferred_element_type=F32)
            decay = jnp.exp(jnp.where(order >= 0, gcc - gcr, NEG_BIG))
            xd = -jnp.where(order > 0, kk * decay, 0.0)
            qkm = qk_raw * decay
            egc = jnp.exp(gcc)
            qg_ref[:, lo:hi] = (q * egc).astype(qg_ref.dtype)
            qk_ref[:, lo:hi] = (qkm[:, 0:128] + qkm[:, 128:256]).astype(qk_ref.dtype)
            kgt_ref[0, lo:hi, :] = (k * jnp.exp(gdc)).T.astype(kgt_ref.dtype)
            xs = row_blocks_sum(xd)
            xs_l.append(xs)
            xd_l.append(xd)
            p_l.append(eye_s + xs)
            rhs_l.append(jnp.concatenate([v * beta, kb * egc], axis=1).astype(BF16))
    n_ch = len(chains)
    xs_l = [_hdot(xs_l[i], xd_l[i]) for i in range(n_ch)]
    for _ in range(4):
        r_l = [_hdot(jnp.concatenate([p_l[i], xs_l[i]], axis=0), block_diag(xs_l[i])) for i in range(n_ch)]
        p_l = [p_l[i] + r_l[i][0:c] for i in range(n_ch)]
        xs_l = [r_l[i][c:2 * c] for i in range(n_ch)]
    p_l = [p_l[i] + _hdot(p_l[i], block_diag(xs_l[i])) for i in range(n_ch)]
    sol_l = [_hdot(block_diag(p_l[i]), rhs_l[i]) for i in range(n_ch)]
    for i, (h, d) in enumerate(chains):
        ch = d * GDN_HEADS + h
        lo, hi = ch * GDN_DK, (ch + 1) * GDN_DK
        u_ref[:, lo:hi] = sol_l[i][:, 0:GDN_DV]
        w_ref[:, lo:hi] = sol_l[i][:, GDN_DV:].astype(w_ref.dtype)


def _gdn_local(qkvn, gcc, gdc, betac, gcr, batch, seq):
    ng = seq // GDN_GROUP
    wide = GDN_CHAINS * GDN_DK
    colb = pl.BlockSpec((GDN_GROUP, GDN_CHAINS), lambda b, s: (b * ng + s, 0))
    outb = pl.BlockSpec((GDN_GROUP, wide), lambda b, s: (b * ng + s, 0))
    n = batch * seq
    return pl.pallas_call(
        _gdn_local_kernel,
        grid=(batch, ng),
        in_specs=[pl.BlockSpec((GDN_GROUP, GDN_QKV), lambda b, s: (b * ng + s, 0)),
                  colb, colb, colb,
                  pl.BlockSpec((GDN_CHAINS, GDN_GROUP), lambda b, s: (0, b * ng + s))],
        out_specs=[outb, outb, outb, outb,
                   pl.BlockSpec((1, wide, GDN_GROUP), lambda b, s: (b, 0, s))],
        out_shape=[jax.ShapeDtypeStruct((n, wide), F32),
                   jax.ShapeDtypeStruct((n, wide), BF16),
                   jax.ShapeDtypeStruct((n, wide), BF16),
                   jax.ShapeDtypeStruct((n, wide), BF16),
                   jax.ShapeDtypeStruct((batch, wide, seq), BF16)],
        compiler_params=_cparams(2),
        name="gdn_local",
    )(qkvn, gcc, gdc, betac, gcr)


def _gdn_scan_kernel(uf_ref, ub_ref, wf_ref, wb_ref, qgf_ref, qgb_ref, qkf_ref, qkb_ref,
                     kgf_ref, kgb_ref, glf_ref, glb_ref, of_ref, ob_ref, s_scr):
    @pl.when(pl.program_id(1) == 0)
    def _():
        s_scr[...] = jnp.zeros_like(s_scr)

    c = GDN_CHUNK
    nc = GDN_GROUP // c
    zeros = jnp.zeros((c, GDN_DV), BF16)
    dirs = ((uf_ref, wf_ref, qgf_ref, qkf_ref, kgf_ref, glf_ref, of_ref),
            (ub_ref, wb_ref, qgb_ref, qkb_ref, kgb_ref, glb_ref, ob_ref))
    states = [s_scr[ch] for ch in range(GDN_CHAINS)]
    chains = [(d, h) for d in range(2) for h in range(GDN_HEADS)]
    for step in range(nc):
        tops, v2s = [], []
        for d, h in chains:
            u_ref, w_ref, qg_ref = dirs[d][0:3]
            i = step if d == 0 else nc - 1 - step
            rows = slice(i * c, (i + 1) * c)
            lo, hi = h * GDN_DK, (h + 1) * GDN_DK
            sb = states[d * GDN_HEADS + h].astype(BF16)
            top = jnp.dot(jnp.concatenate([w_ref[rows, lo:hi], qg_ref[rows, lo:hi]], axis=0), sb,
                          preferred_element_type=F32)
            vb = (u_ref[rows, lo:hi] - top[0:c]).astype(BF16)
            v2s.append(jnp.concatenate([vb, zeros] if i % 2 == 0 else [zeros, vb], axis=0))
            tops.append(top)
        for n, (d, h) in enumerate(chains):
            qk_ref, kg_ref, gl_ref, o_ref = dirs[d][3:7]
            ch = d * GDN_HEADS + h
            i = step if d == 0 else nc - 1 - step
            rows = slice(i * c, (i + 1) * c)
            pair = slice((i // 2) * 2 * c, (i // 2 + 1) * 2 * c)
            lo, hi = h * GDN_DK, (h + 1) * GDN_DK
            bot = jnp.dot(jnp.concatenate([qk_ref[rows, lo:hi], kg_ref[0, lo:hi, pair]], axis=0), v2s[n],
                          preferred_element_type=F32)
            o_ref[rows, lo:hi] = tops[n][c:2 * c] + bot[0:c]
            decay = jnp.exp(gl_ref[ch:ch + 1, i * c:i * c + 1])
            states[ch] = states[ch] * decay + bot[c:c + GDN_DK]
    for ch in range(GDN_CHAINS):
        s_scr[ch] = states[ch]


def _gdn_scan(u, w, qg, qk, kgt, gl, batch, seq):
    ng = seq // GDN_GROUP
    half = GDN_HEADS * GDN_DK
    fwd = lambda b, s: (b * ng + s, 0)
    bwd = lambda b, s: (b * ng + ng - 1 - s, 1)
    blk = (GDN_GROUP, half)
    n = batch * seq
    pair = lambda arr: (arr, arr)
    out_f = pl.BlockSpec(blk, fwd)
    out_b = pl.BlockSpec(blk, lambda b, s: (b * ng + ng - 1 - s, 0))
    in_specs = []
    for _ in range(4):
        in_specs += [pl.BlockSpec(blk, fwd), pl.BlockSpec(blk, bwd)]
    in_specs += [pl.BlockSpec((1, half, GDN_GROUP), lambda b, s: (b, 0, s)),
                 pl.BlockSpec((1, half, GDN_GROUP), lambda b, s: (b, 1, ng - 1 - s)),
                 pl.BlockSpec((GDN_CHAINS, GDN_GROUP), lambda b, s: (0, b * ng + s)),
                 pl.BlockSpec((GDN_CHAINS, GDN_GROUP), lambda b, s: (0, b * ng + ng - 1 - s))]
    return pl.pallas_call(
        _gdn_scan_kernel,
        grid=(batch, ng),
        in_specs=in_specs,
        out_specs=[out_f, out_b],
        out_shape=[jax.ShapeDtypeStruct((n, half), F32), jax.ShapeDtypeStruct((n, half), F32)],
        scratch_shapes=[pltpu.VMEM((GDN_CHAINS, GDN_DK, GDN_DV), F32)],
        compiler_params=_cparams(2),
        name="gdn_scan",
    )(*pair(u), *pair(w), *pair(qg), *pair(qk), *pair(kgt), *pair(gl))


def _layer_norm(r, g, b):
    mu = jnp.mean(r, axis=-1, keepdims=True)
    var = jnp.mean(jnp.square(r - mu), axis=-1, keepdims=True)
    return (r - mu) * lax.rsqrt(var + LN_EPS) * g + b


def _merge_kernel(ya_ref, of_ref, ob_ref, gate_ref, yc_ref, gz_ref, x_ref, wb_ref, wo_ref,
                  nw_ref, lng_ref, lnb_ref, o_ref):
    segs = []
    for h in range(GDN_HEADS):
        sl = slice(h * GDN_DV, (h + 1) * GDN_DV)
        o = of_ref[:, sl] + ob_ref[:, sl]
        o = o * lax.rsqrt(jnp.mean(o * o, axis=-1, keepdims=True) + RMS_EPS) * nw_ref[...]
        gt = gate_ref[:, sl]
        segs.append(o * (gt * jax.nn.sigmoid(gt)))
    yb = jnp.concatenate(segs, axis=1).astype(BF16)
    merged = None
    for n, y in enumerate((ya_ref[...], yb, yc_ref[...])):
        proj = jnp.dot(y, wb_ref[n], preferred_element_type=F32)
        term = jax.nn.sigmoid(gz_ref[:, n * D_MODEL:(n + 1) * D_MODEL]) * proj
        merged = term if merged is None else merged + term
    hmix = jnp.dot(merged.astype(BF16), wo_ref[...], preferred_element_type=F32)
    o_ref[...] = _layer_norm(DEEPNORM_ALPHA * x_ref[...] + hmix, lng_ref[...], lnb_ref[...])


def _merge(ya, o_f, o_b, zb, yc, x, w_branch, w_out, norm_w, ln_g, ln_b, tm):
    n = x.shape[0]
    row = lambda width, col=0: pl.BlockSpec((tm, width), lambda i, col=col: (i, col))
    full = lambda shape: pl.BlockSpec(shape, lambda i: (0,) * len(shape))
    return pl.pallas_call(
        _merge_kernel,
        grid=(n // tm,),
        in_specs=[row(BRANCH_WIDTH), row(BRANCH_WIDTH), row(BRANCH_WIDTH),
                  row(BRANCH_WIDTH, ZB_GATE // BRANCH_WIDTH), row(BRANCH_WIDTH),
                  row(N_BRANCHES * D_MODEL, ZB_GZ // (N_BRANCHES * D_MODEL)), row(D_MODEL),
                  full((N_BRANCHES, BRANCH_WIDTH, D_MODEL)), full((D_MODEL, D_MODEL)),
                  full((1, GDN_DV)), full((1, D_MODEL)), full((1, D_MODEL))],
        out_specs=row(D_MODEL),
        out_shape=jax.ShapeDtypeStruct((n, D_MODEL), F32),
        compiler_params=_cparams(1),
        name="merge_ln",
    )(ya, o_f, o_b, zb, yc, zb, x, w_branch, w_out, norm_w.astype(F32).reshape(1, GDN_DV),
      ln_g.astype(F32).reshape(1, D_MODEL), ln_b.astype(F32).reshape(1, D_MODEL))


FFN_SPLIT = 2


def _ffn_kernel(xp_ref, x_ref, xn_ref, wup_ref, cw_ref, cb_ref, wdn_ref, lng_ref, lnb_ref, o_ref, *, tiles_per_seq):
    i = pl.program_id(0)
    tm = x_ref.shape[0]
    halo = V7X_SUBLANES
    x = x_ref[...]
    xe = jnp.concatenate([xp_ref[...], x, xn_ref[...]], axis=0).astype(BF16)
    xb = x.astype(BF16)
    rows = lax.broadcasted_iota(jnp.int32, (tm + 2 * halo, 1), 0)
    first = (i % tiles_per_seq) == 0
    last = (i % tiles_per_seq) == tiles_per_seq - 1
    keep = jnp.logical_not(((rows < halo) & first) | ((rows >= tm + halo) & last))
    fc = D_FF // FFN_SPLIT
    ext = tm + 2 * halo
    acc = None
    for c in range(FFN_SPLIT):
        ge = jnp.dot(xe, wup_ref[:, c * fc:(c + 1) * fc], preferred_element_type=F32)
        ge = jnp.where(keep, ge, 0.0)
        up = jnp.dot(xb, wup_ref[:, D_FF + c * fc:D_FF + (c + 1) * fc], preferred_element_type=F32)
        cw = cw_ref[:, c * fc:(c + 1) * fc]
        conv = (pltpu.roll(ge, 1, 0) * cw[0:1] + ge * cw[1:2] + pltpu.roll(ge, ext - 1, 0) * cw[2:3])
        gate = conv[halo:halo + tm] + cb_ref[:, c * fc:(c + 1) * fc]
        act = (gate * jax.nn.sigmoid(gate) * up).astype(BF16)
        part = jnp.dot(act, wdn_ref[c * fc:(c + 1) * fc, :], preferred_element_type=F32)
        acc = part if acc is None else acc + part
    o_ref[...] = _layer_norm(DEEPNORM_ALPHA * x + acc, lng_ref[...], lnb_ref[...])


def _ffn(x, w_up, conv_w, conv_b, w_down, ln_g, ln_b, seq, tm):
    n = x.shape[0]
    halo = V7X_SUBLANES
    per = tm // halo
    nh = n // halo
    full = lambda shape: pl.BlockSpec(shape, lambda i: (0,) * len(shape))
    kern = functools.partial(_ffn_kernel, tiles_per_seq=seq // tm)
    return pl.pallas_call(
        kern,
        grid=(n // tm,),
        in_specs=[pl.BlockSpec((halo, D_MODEL), lambda i: (jnp.maximum(i * per - 1, 0), 0)),
                  pl.BlockSpec((tm, D_MODEL), lambda i: (i, 0)),
                  pl.BlockSpec((halo, D_MODEL), lambda i: (jnp.minimum((i + 1) * per, nh - 1), 0)),
                  full((D_MODEL, 2 * D_FF)), full((FFN_CONV, D_FF)), full((1, D_FF)),
                  full((D_FF, D_MODEL)), full((1, D_MODEL)), full((1, D_MODEL))],
        out_specs=pl.BlockSpec((tm, D_MODEL), lambda i: (i, 0)),
        out_shape=jax.ShapeDtypeStruct((n, D_MODEL), F32),
        compiler_params=_cparams(1),
        name="ffn_ln",
    )(x, x, x, w_up, conv_w.astype(F32), conv_b.astype(F32).reshape(1, D_FF), w_down,
      ln_g.astype(F32).reshape(1, D_MODEL), ln_b.astype(F32).reshape(1, D_MODEL))


def _pack_w_in(w):
    widths = (512, 512, 512, GDN_QKV, 512, 8, 8, 512, 128, 128, N_BRANCHES * D_MODEL)
    offs = [0]
    for wd in widths:
        offs.append(offs[-1] + wd)
    seg = lambda i: w[:, offs[i]:offs[i + 1]]
    aq, ak, av, bqkv, bgate, ba, bb, cq, ck, cv, gz = (seg(i) for i in range(len(widths)))
    dup = lambda t: jnp.concatenate([t[:, :HEAD_DIM], t[:, :HEAD_DIM], t[:, HEAD_DIM:], t[:, HEAD_DIM:]], axis=1)
    w_a = jnp.concatenate([aq, ak, av, cq, dup(ck), dup(cv)], axis=1)
    pad = jnp.zeros((w.shape[0], ZB_COLS - ZB_AB - 16), w.dtype)
    w_b = jnp.concatenate([gz, bqkv, bgate, ba, bb, pad], axis=1)
    return w_a.astype(BF16), w_b.astype(BF16)


def kernel(x, rel_bias, w_in, diff_lambda, diff_subln, gdn_conv, gdn_a_log, gdn_dt_bias, gdn_norm, swa_sink,
           w_branch, w_out, ln1_g, ln1_b, ffn_up, ffn_conv, ffn_conv_b, ffn_down, ln2_g, ln2_b):
    batch, seq, d = x.shape
    assert d == D_MODEL and seq % GDN_GROUP == 0 and seq % (DIFF_QROWS * DIFF_UNROLL) == 0
    n = batch * seq
    tm = 256
    tb_diff = _diff_bias_tiles(rel_bias, seq)
    tb_swa = _swa_bias_tiles(rel_bias)
    xf = x.reshape(n, d)
    for l in range(DEPTH):
        w_a, w_b = _pack_w_in(w_in[l])
        za = _matmul(xf, w_a, BF16, 512, ZA_COLS // 2, "in_proj_a")
        zb = _matmul(xf, w_b, F32, 512, ZB_COLS // 3, "in_proj_b")
        ya = _diff_attention(za, tb_diff, diff_lambda[l], diff_subln[l], batch, seq, l)
        yc = _swa_attention(za, tb_swa, swa_sink[l], batch, seq)
        qkvn = _gdn_conv(zb, gdn_conv[l], batch, seq)
        ab_t = zb[:, ZB_AB:ZB_AB + 2 * GDN_CHAINS].T
        gc, gd, gl, beta = _gdn_gates(ab_t, gdn_a_log[l], gdn_dt_bias[l], batch, seq)
        u, w, qg, qk, kgt = _gdn_local(qkvn, gc.T, gd.T, beta.T, gc, batch, seq)
        o_f, o_b = _gdn_scan(u, w, qg, qk, kgt, gl, batch, seq)
        xf = _merge(ya, o_f, o_b, zb, yc, xf, w_branch[l].astype(BF16), w_out[l].astype(BF16),
                    gdn_norm[l], ln1_g[l], ln1_b[l], tm)
        xf = _ffn(xf, ffn_up[l].astype(BF16), ffn_conv[l], ffn_conv_b[l], ffn_down[l].astype(BF16),
                  ln2_g[l], ln2_b[l], seq, tm)
    return xf.reshape(batch, seq, d)
```

```python
import functools
import math

import jax
import jax.numpy as jnp
from jax import lax
from jax.experimental import pallas as pl
from jax.experimental.pallas import tpu as pltpu

F32 = jnp.float32
BF16 = jnp.bfloat16

D_MODEL = 1024
DEPTH = 2
HEAD_DIM = 64
DIFF_HEADS = 4
DIFF_V = 2 * HEAD_DIM
Q_BLOCK = 128
GDN_HEADS = 4
GDN_DK = 128
GDN_DV = 128
GDN_QKV = GDN_HEADS * (2 * GDN_DK + GDN_DV)
GDN_CONV = 5
GDN_CHUNK = 64
SWA_HEADS = 8
SWA_KV_HEADS = 2
SWA_WINDOW = 128
SWA_BLOCK = 128
BRANCH_WIDTH = 512
N_BRANCHES = 3
REL_BUCKETS = 32
REL_MAX_DIST = 128
D_FF = 2816
FFN_CONV = 3
DEEPNORM_ALPHA = (2 * DEPTH) ** 0.25
LN_EPS = 1e-5
RMS_EPS = 1e-6

V7X_LANES = 128
V7X_SUBLANES = 8
V7X_VMEM_BYTES = 64 * 1024 * 1024
VMEM_LIMIT = V7X_VMEM_BYTES * 7 // 8

NEG_BIG = -1e30

ZA_AQ, ZA_AK, ZA_AV = 0, 512, 1024
ZA_CQ = 1536
ZA_CK = 2048
ZA_CV = 2304
ZA_COLS = 2560
ZB_GZ = 0
ZB_QKV = 3072
ZB_GATE = 4608
ZB_AB = 5120
ZB_COLS = 5376

GDN_GROUP = 256
GDN_CHAINS = 2 * GDN_HEADS


def _cparams(n_grid):
    return pltpu.CompilerParams(dimension_semantics=("arbitrary",) * n_grid, vmem_limit_bytes=VMEM_LIMIT)


def _t5_bucket(rel):
    nb = REL_BUCKETS // 2
    ret = jnp.where(rel > 0, nb, 0)
    n = jnp.abs(rel)
    max_exact = nb // 2
    large = max_exact + (jnp.log(jnp.maximum(n, 1).astype(jnp.float32) / max_exact)
                         / math.log(REL_MAX_DIST / max_exact) * (nb - max_exact)).astype(jnp.int32)
    large = jnp.minimum(large, nb - 1)
    return ret + jnp.where(n < max_exact, n, large)


def _table_lookup(table, bucket):
    out = jnp.zeros((table.shape[1],) + bucket.shape, F32)
    for i in range(REL_BUCKETS):
        out = jnp.where(bucket[None] == i, table[i].astype(F32).reshape((-1,) + (1,) * bucket.ndim), out)
    return out


def _diff_bias_tiles(rel_bias, seq):
    nq = seq // Q_BLOCK
    d = jnp.arange(2 * nq - 1) - (nq - 1)
    r = jnp.arange(Q_BLOCK)
    rel = d[:, None, None] * Q_BLOCK + r[None, None, :] - r[None, :, None]
    return _table_lookup(rel_bias[:, :DIFF_HEADS], _t5_bucket(rel))


def _swa_bias_tiles(rel_bias):
    kb = 3 * SWA_BLOCK
    rel = jnp.arange(kb)[None, :] - SWA_WINDOW - jnp.arange(SWA_BLOCK)[:, None]
    bias = _table_lookup(rel_bias[:, DIFF_HEADS:], _t5_bucket(rel))
    return jnp.where((jnp.abs(rel) <= SWA_WINDOW)[None], bias, NEG_BIG)


def _matmul_kernel(x_ref, w_ref, o_ref):
    o_ref[...] = jnp.dot(x_ref[...].astype(BF16), w_ref[...],
                         preferred_element_type=F32).astype(o_ref.dtype)


def _matmul(x, w, out_dtype, tm, tn, name):
    m, k = x.shape
    n = w.shape[1]
    return pl.pallas_call(
        _matmul_kernel,
        grid=(n // tn, m // tm),
        in_specs=[pl.BlockSpec((tm, k), lambda j, i: (i, 0)),
                  pl.BlockSpec((k, tn), lambda j, i: (0, j))],
        out_specs=pl.BlockSpec((tm, tn), lambda j, i: (i, j)),
        out_shape=jax.ShapeDtypeStruct((m, n), out_dtype),
        compiler_params=_cparams(2),
        name=name,
    )(x, w)


DIFF_QROWS = 2 * Q_BLOCK
DIFF_UNROLL = 4


def _diff_attn_kernel(q_ref, k_ref, v_ref, tb_ref, lam_ref, subln_ref, o_ref, s_scr, *, nq, lam_init):
    qr = DIFF_QROWS
    scale = HEAD_DIM ** -0.5
    lv = lam_ref[...]
    lam = (jnp.exp(jnp.sum(lv[0:1] * lv[1:2], axis=-1, keepdims=True))
           - jnp.exp(jnp.sum(lv[2:3] * lv[3:4], axis=-1, keepdims=True)) + lam_init)
    lane = lax.broadcasted_iota(jnp.int32, (qr, 128), 1)
    zero = jnp.zeros((qr, 128), q_ref.dtype)

    def scores(qb, scr):
        r0 = pl.multiple_of(qb * qr, qr)
        q = q_ref[pl.ds(r0, qr), :] * scale
        qz = jnp.concatenate([jnp.where(lane < HEAD_DIM, q, zero),
                              jnp.where(lane >= HEAD_DIM, q, zero)], axis=0)
        for kp in range(nq // 2):
            kblk = k_ref[kp * qr:(kp + 1) * qr, :]
            s = lax.dot_general(qz, kblk, (((1,), (1,)), ((), ())), preferred_element_type=F32)
            for a in range(2):
                d0 = nq - 1 - (2 * qb + a) + 2 * kp
                bias = jnp.concatenate([tb_ref[0, d0], tb_ref[0, d0 + 1]], axis=1)
                for c in range(2):
                    rows = slice(c * qr + a * Q_BLOCK, c * qr + (a + 1) * Q_BLOCK)
                    scr[rows, kp * qr:(kp + 1) * qr] = s[rows] + bias

    def softmax(scr):
        s = scr[...]
        e = jnp.exp(s - jnp.max(s, axis=-1, keepdims=True))
        return e.astype(BF16), jnp.sum(e, axis=-1, keepdims=True)

    def finish(qb, e, l):
        r0 = pl.multiple_of(qb * qr, qr)
        pv = jnp.dot(e, v_ref[...], preferred_element_type=F32) / l
        o = pv[:qr] - lam * pv[qr:]
        o = o * lax.rsqrt(jnp.mean(o * o, axis=-1, keepdims=True) + RMS_EPS) * subln_ref[...]
        o_ref[pl.ds(r0, qr), :] = (o * (1.0 - lam_init)).astype(o_ref.dtype)

    def body(i, carry):
        blocks = [i * DIFF_UNROLL + j for j in range(DIFF_UNROLL)]
        for j, qb in enumerate(blocks):
            scores(qb, s_scr.at[j])
        probs = [softmax(s_scr.at[j]) for j in range(DIFF_UNROLL)]
        for qb, (e, l) in zip(blocks, probs):
            finish(qb, e, l)
        return carry

    lax.fori_loop(0, nq // (2 * DIFF_UNROLL), body, 0)


def _diff_attention(za, tb, lam_vecs, subln, batch, seq, layer_idx):
    nq = seq // Q_BLOCK
    lam_init = 0.8 - 0.6 * math.exp(-0.3 * layer_idx)
    kern = functools.partial(_diff_attn_kernel, nq=nq, lam_init=lam_init)
    qcol, kcol, vcol = ZA_AQ // 128, ZA_AK // 128, ZA_AV // 128
    return pl.pallas_call(
        kern,
        grid=(batch, DIFF_HEADS),
        in_specs=[
            pl.BlockSpec((seq, 128), lambda b, h: (b, qcol + h)),
            pl.BlockSpec((seq, 128), lambda b, h: (b, kcol + h)),
            pl.BlockSpec((seq, 128), lambda b, h: (b, vcol + h)),
            pl.BlockSpec((1, 2 * nq - 1, Q_BLOCK, Q_BLOCK), lambda b, h: (h, 0, 0, 0)),
            pl.BlockSpec((4, HEAD_DIM), lambda b, h: (0, 0)),
            pl.BlockSpec((1, DIFF_V), lambda b, h: (0, 0)),
        ],
        out_specs=pl.BlockSpec((seq, 128), lambda b, h: (b, h)),
        out_shape=jax.ShapeDtypeStruct((batch * seq, BRANCH_WIDTH), BF16),
        scratch_shapes=[pltpu.VMEM((DIFF_UNROLL, 2 * DIFF_QROWS, seq), F32)],
        compiler_params=_cparams(2),
        name="diff_attention",
    )(za, za, za, tb, lam_vecs.astype(F32), subln.astype(F32).reshape(1, DIFF_V))


def _swa_kernel(q_ref, k_ref, v_ref, bias_ref, sink_ref, o_ref, kp_scr, vp_scr, *, nb):
    seq = q_ref.shape[0]
    blk = SWA_BLOCK
    pad = jnp.zeros((blk, kp_scr.shape[1]), kp_scr.dtype)
    for scr, ref in ((kp_scr, k_ref), (vp_scr, v_ref)):
        scr[0:blk] = pad
        scr[blk + seq:2 * blk + seq] = pad
        scr[blk:blk + seq] = ref[...]
    col = lax.broadcasted_iota(jnp.int32, (blk, 3 * blk), 1)
    lane = lax.broadcasted_iota(jnp.int32, (blk, 128), 1)
    low = lane < HEAD_DIM
    zero = jnp.zeros((blk, 128), q_ref.dtype)
    scale = HEAD_DIM ** -0.5
    rep = SWA_HEADS // SWA_KV_HEADS

    def body(n, carry):
        r0 = pl.multiple_of(n * blk, blk)
        q = q_ref[pl.ds(r0, blk), :]
        kw = kp_scr[pl.ds(r0, 3 * blk), :]
        vw = vp_scr[pl.ds(r0, 3 * blk), :]
        outside = ((col < blk) & (n == 0)) | ((col >= 2 * blk) & (n == nb - 1))
        heads = range(SWA_HEADS)
        qms = [jnp.where(low if hd % 2 == 0 else jnp.logical_not(low),
                         q[:, (hd // 2) * 128:(hd // 2 + 1) * 128] * scale, zero) for hd in heads]
        ss = [lax.dot_general(qms[hd], kw[:, (hd // rep) * 128:(hd // rep + 1) * 128], (((1,), (1,)), ((), ())),
                              preferred_element_type=F32) for hd in heads]
        ss = [jnp.where(outside, NEG_BIG, ss[hd] + bias_ref[hd]) for hd in heads]
        sinks = [sink_ref[hd][:, 0:1] for hd in heads]
        ms = [jnp.maximum(jnp.max(ss[hd], axis=-1, keepdims=True), sinks[hd]) for hd in heads]
        es = [jnp.exp(ss[hd] - ms[hd]) for hd in heads]
        dens = [jnp.sum(es[hd], axis=-1, keepdims=True) + jnp.exp(sinks[hd] - ms[hd]) for hd in heads]
        pvs = [jnp.dot(es[hd].astype(BF16), vw[:, (hd // rep) * 128:(hd // rep + 1) * 128],
                       preferred_element_type=F32) for hd in heads]
        halves = [pvs[hd] / dens[hd] for hd in heads]
        o = jnp.concatenate([jnp.where(low, halves[2 * c], halves[2 * c + 1]) for c in range(SWA_HEADS // 2)], axis=1)
        o_ref[pl.ds(r0, blk), :] = o.astype(o_ref.dtype)
        return carry

    lax.fori_loop(0, nb, body, 0)


def _swa_attention(za, bias, sink, batch, seq):
    nb = seq // SWA_BLOCK
    kern = functools.partial(_swa_kernel, nb=nb)
    sink_b = jnp.broadcast_to(sink.astype(F32).reshape(SWA_HEADS, 1, 1), (SWA_HEADS, 1, 128))
    kvw = 2 * SWA_KV_HEADS * HEAD_DIM
    return pl.pallas_call(
        kern,
        grid=(batch,),
        in_specs=[
            pl.BlockSpec((seq, BRANCH_WIDTH), lambda b: (b, ZA_CQ // BRANCH_WIDTH)),
            pl.BlockSpec((seq, kvw), lambda b: (b, ZA_CK // kvw)),
            pl.BlockSpec((seq, kvw), lambda b: (b, ZA_CV // kvw)),
            pl.BlockSpec((SWA_HEADS, SWA_BLOCK, 3 * SWA_BLOCK), lambda b: (0, 0, 0)),
            pl.BlockSpec((SWA_HEADS, 1, 128), lambda b: (0, 0, 0)),
        ],
        out_specs=pl.BlockSpec((seq, BRANCH_WIDTH), lambda b: (b, 0)),
        out_shape=jax.ShapeDtypeStruct((batch * seq, BRANCH_WIDTH), BF16),
        scratch_shapes=[pltpu.VMEM((seq + 2 * SWA_BLOCK, kvw), BF16),
                        pltpu.VMEM((seq + 2 * SWA_BLOCK, kvw), BF16)],
        compiler_params=_cparams(1),
        name="swa_attention",
    )(za, za, za, bias, sink_b)


def _shift_rows(x, d):
    rows = x.shape[0]
    t = lax.broadcasted_iota(jnp.int32, x.shape, 0)
    rolled = pltpu.roll(x, (-d) % rows, 0)
    return jnp.where((t + d >= 0) & (t + d < rows), rolled, 0.0)


def _gdn_conv_kernel(x_ref, w_ref, o_ref):
    j = pl.program_id(1)
    x = x_ref[...]
    w = w_ref[...]
    half = GDN_CONV // 2
    acc = x * w[half:half + 1]
    for tap in range(GDN_CONV):
        if tap != half:
            acc = acc + _shift_rows(x, tap - half) * w[tap:tap + 1]
    h = acc * jax.nn.sigmoid(acc)
    inv = lax.rsqrt(jnp.sum(h * h, axis=-1, keepdims=True) + RMS_EPS)
    factor = jnp.where(j < 2 * GDN_HEADS, inv, 1.0) * jnp.where(j < GDN_HEADS, GDN_DK ** -0.5, 1.0)
    o_ref[...] = h * factor


def _gdn_conv(zb, conv_w, batch, seq):
    c0 = ZB_QKV // 128
    return pl.pallas_call(
        _gdn_conv_kernel,
        grid=(batch, GDN_QKV // 128),
        in_specs=[pl.BlockSpec((seq, 128), lambda b, j: (b, c0 + j)),
                  pl.BlockSpec((GDN_CONV, 128), lambda b, j: (0, j))],
        out_specs=pl.BlockSpec((seq, 128), lambda b, j: (b, j)),
        out_shape=jax.ShapeDtypeStruct((batch * seq, GDN_QKV), F32),
        compiler_params=_cparams(2),
        name="gdn_conv",
    )(zb, conv_w.astype(F32))


def _gdn_gate_kernel(ab_ref, alog_ref, dtb_ref, gc_ref, gd_ref, gl_ref, beta_ref):
    ab = ab_ref[...]
    a = ab[0:GDN_CHAINS]
    seq = a.shape[1]
    xs = a + dtb_ref[...]
    softplus = jnp.maximum(xs, 0.0) + jnp.log1p(jnp.exp(-jnp.abs(xs)))
    g = -jnp.exp(alog_ref[...]) * softplus
    beta_ref[...] = jax.nn.sigmoid(ab[GDN_CHAINS:2 * GDN_CHAINS])
    pos = lax.broadcasted_iota(jnp.int32, g.shape, 1) % GDN_CHUNK
    cf = g
    cr = g
    s = 1
    while s < GDN_CHUNK:
        cf = cf + jnp.where(pos >= s, pltpu.roll(cf, s, 1), 0.0)
        cr = cr + jnp.where(pos < GDN_CHUNK - s, pltpu.roll(cr, seq - s, 1), 0.0)
        s *= 2
    fwd = lax.broadcasted_iota(jnp.int32, g.shape, 0) < GDN_HEADS
    gc_ref[...] = jnp.where(fwd, cf, cr)
    gd_ref[...] = jnp.where(fwd, cr, cf) - g
    gl_ref[...] = cf + cr - g


def _gdn_gates(ab_t, a_log, dt_bias, batch, seq):
    row = pl.BlockSpec((GDN_CHAINS, seq), lambda b: (0, b))
    shp = jax.ShapeDtypeStruct((GDN_CHAINS, batch * seq), F32)
    par = pl.BlockSpec((GDN_CHAINS, 1), lambda b: (0, 0))
    return pl.pallas_call(
        _gdn_gate_kernel,
        grid=(batch,),
        in_specs=[pl.BlockSpec((2 * GDN_CHAINS, seq), lambda b: (0, b)), par, par],
        out_specs=[row, row, row, row],
        out_shape=[shp, shp, shp, shp],
        compiler_params=_cparams(1),
        name="gdn_gates",
    )(ab_t, a_log.astype(F32).reshape(GDN_CHAINS, 1), dt_bias.astype(F32).reshape(GDN_CHAINS, 1))


def _hdot(a, b):
    return jnp.dot(a.astype(BF16), b.astype(BF16), preferred_element_type=F32)


def _gdn_local_kernel(qkv_ref, gcc_ref, gdc_ref, betac_ref, gcr_ref,
                      u_ref, w_ref, qg_ref, qk_ref, kgt_ref):
    g = GDN_GROUP
    c = GDN_CHUNK
    ri = lax.broadcasted_iota(jnp.int32, (g, g), 0)
    ci = lax.broadcasted_iota(jnp.int32, (g, g), 1)
    same = (ri // c) == (ci // c)
    ahead = jnp.where(same, ri - ci, -g)
    behind = jnp.where(same, ci - ri, -g)
    eye_s = (lax.broadcasted_iota(jnp.int32, (c, g), 0)
             == lax.broadcasted_iota(jnp.int32, (c, g), 1) % c).astype(F32)

    def block_diag(xs):
        return jnp.where(same, jnp.concatenate([xs] * (g // c), axis=0), 0.0)

    def row_blocks_sum(xd):
        out = xd[0:c]
        for i in range(1, g // c):
            out = out + xd[i * c:(i + 1) * c]
        return out

    chains = [(h, d) for h in range(GDN_HEADS) for d in range(2)]
    xs_l, xd_l, p_l, rhs_l = [], [], [], []
    for h in range(GDN_HEADS):
        q = qkv_ref[:, h * GDN_DK:(h + 1) * GDN_DK]
        k = qkv_ref[:, (GDN_HEADS + h) * GDN_DK:(GDN_HEADS + h + 1) * GDN_DK]
        v = qkv_ref[:, (2 * GDN_HEADS + h) * GDN_DK:(2 * GDN_HEADS + h + 1) * GDN_DK]
        kbf = k.astype(BF16)
        qk_raw = lax.dot_general(q.astype(BF16), kbf, (((1,), (1,)), ((), ())), preferred_element_type=F32)
        for d in range(2):
            ch = d * GDN_HEADS + h
            lo, hi = ch * GDN_DK, (ch + 1) * GDN_DK
            gcc = gcc_ref[:, ch:ch + 1]
            gdc = gdc_ref[:, ch:ch + 1]
            beta = betac_ref[:, ch:ch + 1]
            gcr = gcr_ref[ch:ch + 1, :]
            order = ahead if d == 0 else behind
            kb = k * beta
            kk = lax.dot_general(kb.astype(BF16), kbf, (((1,), (1,)), ((), ())), preferred_element_type=F32)
            decay = jnp.exp(jnp.where(order >= 0, gcc - gcr, NEG_BIG))
            xd = -jnp.where(order > 0, kk * decay, 0.0)
            qkm = qk_raw * decay
            egc = jnp.exp(gcc)
            qg_ref[:, lo:hi] = (q * egc).astype(qg_ref.dtype)
            qk_ref[:, lo:hi] = (qkm[:, 0:128] + qkm[:, 128:256]).astype(qk_ref.dtype)
            kgt_ref[0, lo:hi, :] = (k * jnp.exp(gdc)).T.astype(kgt_ref.dtype)
            xs = row_blocks_sum(xd)
            xs_l.append(xs)
            xd_l.append(xd)
            p_l.append(eye_s + xs)
            rhs_l.append(jnp.concatenate([v * beta, kb * egc], axis=1).astype(BF16))
    n_ch = len(chains)
    xs_l = [_hdot(xs_l[i], xd_l[i]) for i in range(n_ch)]
    for _ in range(4):
        r_l = [_hdot(jnp.concatenate([p_l[i], xs_l[i]], axis=0), block_diag(xs_l[i])) for i in range(n_ch)]
        p_l = [p_l[i] + r_l[i][0:c] for i in range(n_ch)]
        xs_l = [r_l[i][c:2 * c] for i in range(n_ch)]
    p_l = [p_l[i] + _hdot(p_l[i], block_diag(xs_l[i])) for i in range(n_ch)]
    sol_l = [_hdot(block_diag(p_l[i]), rhs_l[i]) for i in range(n_ch)]
    for i, (h, d) in enumerate(chains):
        ch = d * GDN_HEADS + h
        lo, hi = ch * GDN_DK, (ch + 1) * GDN_DK
        u_ref[:, lo:hi] = sol_l[i][:, 0:GDN_DV]
        w_ref[:, lo:hi] = sol_l[i][:, GDN_DV:].astype(w_ref.dtype)


def _gdn_local(qkvn, gcc, gdc, betac, gcr, batch, seq):
    ng = seq // GDN_GROUP
    wide = GDN_CHAINS * GDN_DK
    colb = pl.BlockSpec((GDN_GROUP, GDN_CHAINS), lambda b, s: (b * ng + s, 0))
    outb = pl.BlockSpec((GDN_GROUP, wide), lambda b, s: (b * ng + s, 0))
    n = batch * seq
    return pl.pallas_call(
        _gdn_local_kernel,
        grid=(batch, ng),
        in_specs=[pl.BlockSpec((GDN_GROUP, GDN_QKV), lambda b, s: (b * ng + s, 0)),
                  colb, colb, colb,
                  pl.BlockSpec((GDN_CHAINS, GDN_GROUP), lambda b, s: (0, b * ng + s))],
        out_specs=[outb, outb, outb, outb,
                   pl.BlockSpec((1, wide, GDN_GROUP), lambda b, s: (b, 0, s))],
        out_shape=[jax.ShapeDtypeStruct((n, wide), F32),
                   jax.ShapeDtypeStruct((n, wide), BF16),
                   jax.ShapeDtypeStruct((n, wide), BF16),
                   jax.ShapeDtypeStruct((n, wide), BF16),
                   jax.ShapeDtypeStruct((batch, wide, seq), BF16)],
        compiler_params=_cparams(2),
        name="gdn_local",
    )(qkvn, gcc, gdc, betac, gcr)


def _gdn_scan_kernel(uf_ref, ub_ref, wf_ref, wb_ref, qgf_ref, qgb_ref, qkf_ref, qkb_ref,
                     kgf_ref, kgb_ref, glf_ref, glb_ref, of_ref, ob_ref, s_scr):
    @pl.when(pl.program_id(1) == 0)
    def _():
        s_scr[...] = jnp.zeros_like(s_scr)

    c = GDN_CHUNK
    nc = GDN_GROUP // c
    zeros = jnp.zeros((c, GDN_DV), BF16)
    dirs = ((uf_ref, wf_ref, qgf_ref, qkf_ref, kgf_ref, glf_ref, of_ref),
            (ub_ref, wb_ref, qgb_ref, qkb_ref, kgb_ref, glb_ref, ob_ref))
    states = [s_scr[ch] for ch in range(GDN_CHAINS)]
    chains = [(d, h) for d in range(2) for h in range(GDN_HEADS)]
    for step in range(nc):
        tops, v2s = [], []
        for d, h in chains:
            u_ref, w_ref, qg_ref = dirs[d][0:3]
            i = step if d == 0 else nc - 1 - step
            rows = slice(i * c, (i + 1) * c)
            lo, hi = h * GDN_DK, (h + 1) * GDN_DK
            sb = states[d * GDN_HEADS + h].astype(BF16)
            top = jnp.dot(jnp.concatenate([w_ref[rows, lo:hi], qg_ref[rows, lo:hi]], axis=0), sb,
                          preferred_element_type=F32)
            vb = (u_ref[rows, lo:hi] - top[0:c]).astype(BF16)
            v2s.append(jnp.concatenate([vb, zeros] if i % 2 == 0 else [zeros, vb], axis=0))
            tops.append(top)
        for n, (d, h) in enumerate(chains):
            qk_ref, kg_ref, gl_ref, o_ref = dirs[d][3:7]
            ch = d * GDN_HEADS + h
            i = step if d == 0 else nc - 1 - step
            rows = slice(i * c, (i + 1) * c)
            pair = slice((i // 2) * 2 * c, (i // 2 + 1) * 2 * c)
            lo, hi = h * GDN_DK, (h + 1) * GDN_DK
            bot = jnp.dot(jnp.concatenate([qk_ref[rows, lo:hi], kg_ref[0, lo:hi, pair]], axis=0), v2s[n],
                          preferred_element_type=F32)
            o_ref[rows, lo:hi] = tops[n][c:2 * c] + bot[0:c]
            decay = jnp.exp(gl_ref[ch:ch + 1, i * c:i * c + 1])
            states[ch] = states[ch] * decay + bot[c:c + GDN_DK]
    for ch in range(GDN_CHAINS):
        s_scr[ch] = states[ch]


def _gdn_scan(u, w, qg, qk, kgt, gl, batch, seq):
    ng = seq // GDN_GROUP
    half = GDN_HEADS * GDN_DK
    fwd = lambda b, s: (b * ng + s, 0)
    bwd = lambda b, s: (b * ng + ng - 1 - s, 1)
    blk = (GDN_GROUP, half)
    n = batch * seq
    pair = lambda arr: (arr, arr)
    out_f = pl.BlockSpec(blk, fwd)
    out_b = pl.BlockSpec(blk, lambda b, s: (b * ng + ng - 1 - s, 0))
    in_specs = []
    for _ in range(4):
        in_specs += [pl.BlockSpec(blk, fwd), pl.BlockSpec(blk, bwd)]
    in_specs += [pl.BlockSpec((1, half, GDN_GROUP), lambda b, s: (b, 0, s)),
                 pl.BlockSpec((1, half, GDN_GROUP), lambda b, s: (b, 1, ng - 1 - s)),
                 pl.BlockSpec((GDN_CHAINS, GDN_GROUP), lambda b, s: (0, b * ng + s)),
                 pl.BlockSpec((GDN_CHAINS, GDN_GROUP), lambda b, s: (0, b * ng + ng - 1 - s))]
    return pl.pallas_call(
        _gdn_scan_kernel,
        grid=(batch, ng),
        in_specs=in_specs,
        out_specs=[out_f, out_b],
        out_shape=[jax.ShapeDtypeStruct((n, half), F32), jax.ShapeDtypeStruct((n, half), F32)],
        scratch_shapes=[pltpu.VMEM((GDN_CHAINS, GDN_DK, GDN_DV), F32)],
        compiler_params=_cparams(2),
        name="gdn_scan",
    )(*pair(u), *pair(w), *pair(qg), *pair(qk), *pair(kgt), *pair(gl))


def _layer_norm(r, g, b):
    mu = jnp.mean(r, axis=-1, keepdims=True)
    var = jnp.mean(jnp.square(r - mu), axis=-1, keepdims=True)
    return (r - mu) * lax.rsqrt(var + LN_EPS) * g + b


def _merge_kernel(ya_ref, of_ref, ob_ref, gate_ref, yc_ref, gz_ref, x_ref, wb_ref, wo_ref,
                  nw_ref, lng_ref, lnb_ref, o_ref):
    segs = []
    for h in range(GDN_HEADS):
        sl = slice(h * GDN_DV, (h + 1) * GDN_DV)
        o = of_ref[:, sl] + ob_ref[:, sl]
        o = o * lax.rsqrt(jnp.mean(o * o, axis=-1, keepdims=True) + RMS_EPS) * nw_ref[...]
        gt = gate_ref[:, sl]
        segs.append(o * (gt * jax.nn.sigmoid(gt)))
    yb = jnp.concatenate(segs, axis=1).astype(BF16)
    merged = None
    for n, y in enumerate((ya_ref[...], yb, yc_ref[...])):
        proj = jnp.dot(y, wb_ref[n], preferred_element_type=F32)
        term = jax.nn.sigmoid(gz_ref[:, n * D_MODEL:(n + 1) * D_MODEL]) * proj
        merged = term if merged is None else merged + term
    hmix = jnp.dot(merged.astype(BF16), wo_ref[...], preferred_element_type=F32)
    o_ref[...] = _layer_norm(DEEPNORM_ALPHA * x_ref[...] + hmix, lng_ref[...], lnb_ref[...])


def _merge(ya, o_f, o_b, zb, yc, x, w_branch, w_out, norm_w, ln_g, ln_b, tm):
    n = x.shape[0]
    row = lambda width, col=0: pl.BlockSpec((tm, width), lambda i, col=col: (i, col))
    full = lambda shape: pl.BlockSpec(shape, lambda i: (0,) * len(shape), pipeline_mode=pl.Buffered(1))
    return pl.pallas_call(
        _merge_kernel,
        grid=(n // tm,),
        in_specs=[row(BRANCH_WIDTH), row(BRANCH_WIDTH), row(BRANCH_WIDTH),
                  row(BRANCH_WIDTH, ZB_GATE // BRANCH_WIDTH), row(BRANCH_WIDTH),
                  row(N_BRANCHES * D_MODEL, ZB_GZ // (N_BRANCHES * D_MODEL)), row(D_MODEL),
                  full((N_BRANCHES, BRANCH_WIDTH, D_MODEL)), full((D_MODEL, D_MODEL)),
                  full((1, GDN_DV)), full((1, D_MODEL)), full((1, D_MODEL))],
        out_specs=row(D_MODEL),
        out_shape=jax.ShapeDtypeStruct((n, D_MODEL), F32),
        compiler_params=_cparams(1),
        name="merge_ln",
    )(ya, o_f, o_b, zb, yc, zb, x, w_branch, w_out, norm_w.astype(F32).reshape(1, GDN_DV),
      ln_g.astype(F32).reshape(1, D_MODEL), ln_b.astype(F32).reshape(1, D_MODEL))


FFN_SPLIT = 2


def _ffn_kernel(xp_ref, x_ref, xn_ref, wup_ref, cw_ref, cb_ref, wdn_ref, lng_ref, lnb_ref, o_ref, *, tiles_per_seq):
    i = pl.program_id(0)
    tm = x_ref.shape[0]
    halo = V7X_SUBLANES
    x = x_ref[...]
    xe = jnp.concatenate([xp_ref[...], x, xn_ref[...]], axis=0).astype(BF16)
    xb = x.astype(BF16)
    rows = lax.broadcasted_iota(jnp.int32, (tm + 2 * halo, 1), 0)
    first = (i % tiles_per_seq) == 0
    last = (i % tiles_per_seq) == tiles_per_seq - 1
    keep = jnp.logical_not(((rows < halo) & first) | ((rows >= tm + halo) & last))
    fc = D_FF // FFN_SPLIT
    ext = tm + 2 * halo
    acc = None
    for c in range(FFN_SPLIT):
        ge = jnp.dot(xe, wup_ref[:, c * fc:(c + 1) * fc], preferred_element_type=F32)
        ge = jnp.where(keep, ge, 0.0)
        up = jnp.dot(xb, wup_ref[:, D_FF + c * fc:D_FF + (c + 1) * fc], preferred_element_type=F32)
        cw = cw_ref[:, c * fc:(c + 1) * fc]
        conv = (pltpu.roll(ge, 1, 0) * cw[0:1] + ge * cw[1:2] + pltpu.roll(ge, ext - 1, 0) * cw[2:3])
        gate = conv[halo:halo + tm] + cb_ref[:, c * fc:(c + 1) * fc]
        act = (gate * jax.nn.sigmoid(gate) * up).astype(BF16)
        part = jnp.dot(act, wdn_ref[c * fc:(c + 1) * fc, :], preferred_element_type=F32)
        acc = part if acc is None else acc + part
    o_ref[...] = _layer_norm(DEEPNORM_ALPHA * x + acc, lng_ref[...], lnb_ref[...])


def _ffn(x, w_up, conv_w, conv_b, w_down, ln_g, ln_b, seq, tm):
    n = x.shape[0]
    halo = V7X_SUBLANES
    per = tm // halo
    nh = n // halo
    full = lambda shape: pl.BlockSpec(shape, lambda i: (0,) * len(shape), pipeline_mode=pl.Buffered(1))
    kern = functools.partial(_ffn_kernel, tiles_per_seq=seq // tm)
    return pl.pallas_call(
        kern,
        grid=(n // tm,),
        in_specs=[pl.BlockSpec((halo, D_MODEL), lambda i: (jnp.maximum(i * per - 1, 0), 0)),
                  pl.BlockSpec((tm, D_MODEL), lambda i: (i, 0)),
                  pl.BlockSpec((halo, D_MODEL), lambda i: (jnp.minimum((i + 1) * per, nh - 1), 0)),
                  full((D_MODEL, 2 * D_FF)), full((FFN_CONV, D_FF)), full((1, D_FF)),
                  full((D_FF, D_MODEL)), full((1, D_MODEL)), full((1, D_MODEL))],
        out_specs=pl.BlockSpec((tm, D_MODEL), lambda i: (i, 0)),
        out_shape=jax.ShapeDtypeStruct((n, D_MODEL), F32),
        compiler_params=_cparams(1),
        name="ffn_ln",
    )(x, x, x, w_up, conv_w.astype(F32), conv_b.astype(F32).reshape(1, D_FF), w_down,
      ln_g.astype(F32).reshape(1, D_MODEL), ln_b.astype(F32).reshape(1, D_MODEL))


def _pack_w_in(w):
    widths = (512, 512, 512, GDN_QKV, 512, 8, 8, 512, 128, 128, N_BRANCHES * D_MODEL)
    offs = [0]
    for wd in widths:
        offs.append(offs[-1] + wd)
    seg = lambda i: w[:, offs[i]:offs[i + 1]]
    aq, ak, av, bqkv, bgate, ba, bb, cq, ck, cv, gz = (seg(i) for i in range(len(widths)))
    dup = lambda t: jnp.concatenate([t[:, :HEAD_DIM], t[:, :HEAD_DIM], t[:, HEAD_DIM:], t[:, HEAD_DIM:]], axis=1)
    w_a = jnp.concatenate([aq, ak, av, cq, dup(ck), dup(cv)], axis=1)
    pad = jnp.zeros((w.shape[0], ZB_COLS - ZB_AB - 16), w.dtype)
    w_b = jnp.concatenate([gz, bqkv, bgate, ba, bb, pad], axis=1)
    return w_a.astype(BF16), w_b.astype(BF16)


def kernel(x, rel_bias, w_in, diff_lambda, diff_subln, gdn_conv, gdn_a_log, gdn_dt_bias, gdn_norm, swa_sink,
           w_branch, w_out, ln1_g, ln1_b, ffn_up, ffn_conv, ffn_conv_b, ffn_down, ln2_g, ln2_b):
    batch, seq, d = x.shape
    assert d == D_MODEL and seq % GDN_GROUP == 0 and seq % (DIFF_QROWS * DIFF_UNROLL) == 0
    n = batch * seq
    tm = 256
    tb_diff = _diff_bias_tiles(rel_bias, seq)
    tb_swa = _swa_bias_tiles(rel_bias)
    xf = x.reshape(n, d)
    for l in range(DEPTH):
        w_a, w_b = _pack_w_in(w_in[l])
        za = _matmul(xf, w_a, BF16, 1024, ZA_COLS // 2, "in_proj_a")
        zb = _matmul(xf, w_b, F32, 1024, ZB_COLS // 3, "in_proj_b")
        ya = _diff_attention(za, tb_diff, diff_lambda[l], diff_subln[l], batch, seq, l)
        yc = _swa_attention(za, tb_swa, swa_sink[l], batch, seq)
        qkvn = _gdn_conv(zb, gdn_conv[l], batch, seq)
        ab_t = zb[:, ZB_AB:ZB_AB + 2 * GDN_CHAINS].T
        gc, gd, gl, beta = _gdn_gates(ab_t, gdn_a_log[l], gdn_dt_bias[l], batch, seq)
        u, w, qg, qk, kgt = _gdn_local(qkvn, gc.T, gd.T, beta.T, gc, batch, seq)
        o_f, o_b = _gdn_scan(u, w, qg, qk, kgt, gl, batch, seq)
        xf = _merge(ya, o_f, o_b, zb, yc, xf, w_branch[l].astype(BF16), w_out[l].astype(BF16),
                    gdn_norm[l], ln1_g[l], ln1_b[l], 2 * tm)
        xf = _ffn(xf, ffn_up[l].astype(BF16), ffn_conv[l], ffn_conv_b[l], ffn_down[l].astype(BF16),
                  ln2_g[l], ln2_b[l], seq, 2 * tm)
    return xf.reshape(batch, seq, d)
```

```python
import functools
import math

import jax
import jax.numpy as jnp
from jax import lax
from jax.experimental import pallas as pl
from jax.experimental.pallas import tpu as pltpu

F32 = jnp.float32
BF16 = jnp.bfloat16

D_MODEL = 1024
DEPTH = 2
HEAD_DIM = 64
DIFF_HEADS = 4
DIFF_V = 2 * HEAD_DIM
Q_BLOCK = 128
GDN_HEADS = 4
GDN_DK = 128
GDN_DV = 128
GDN_QKV = GDN_HEADS * (2 * GDN_DK + GDN_DV)
GDN_CONV = 5
GDN_CHUNK = 64
SWA_HEADS = 8
SWA_KV_HEADS = 2
SWA_WINDOW = 128
SWA_BLOCK = 128
BRANCH_WIDTH = 512
N_BRANCHES = 3
REL_BUCKETS = 32
REL_MAX_DIST = 128
D_FF = 2816
FFN_CONV = 3
DEEPNORM_ALPHA = (2 * DEPTH) ** 0.25
LN_EPS = 1e-5
RMS_EPS = 1e-6

V7X_LANES = 128
V7X_SUBLANES = 8
V7X_VMEM_BYTES = 64 * 1024 * 1024
VMEM_LIMIT = V7X_VMEM_BYTES * 7 // 8

NEG_BIG = -1e30

ZA_AQ, ZA_AK, ZA_AV = 0, 512, 1024
ZA_CQ = 1536
ZA_CK = 2048
ZA_CV = 2304
ZA_COLS = 2560
ZB_GZ = 0
ZB_QKV = 3072
ZB_GATE = 4608
ZB_AB = 5120
ZB_COLS = 5376

GDN_GROUP = 256
GDN_CHAINS = 2 * GDN_HEADS


def _cparams(n_grid):
    return pltpu.CompilerParams(dimension_semantics=("arbitrary",) * n_grid, vmem_limit_bytes=VMEM_LIMIT)


def _t5_bucket(rel):
    nb = REL_BUCKETS // 2
    ret = jnp.where(rel > 0, nb, 0)
    n = jnp.abs(rel)
    max_exact = nb // 2
    large = max_exact + (jnp.log(jnp.maximum(n, 1).astype(jnp.float32) / max_exact)
                         / math.log(REL_MAX_DIST / max_exact) * (nb - max_exact)).astype(jnp.int32)
    large = jnp.minimum(large, nb - 1)
    return ret + jnp.where(n < max_exact, n, large)


def _table_lookup(table, bucket):
    out = jnp.zeros((table.shape[1],) + bucket.shape, F32)
    for i in range(REL_BUCKETS):
        out = jnp.where(bucket[None] == i, table[i].astype(F32).reshape((-1,) + (1,) * bucket.ndim), out)
    return out


def _diff_bias_tiles(rel_bias, seq):
    nq = seq // Q_BLOCK
    d = jnp.arange(2 * nq - 1) - (nq - 1)
    r = jnp.arange(Q_BLOCK)
    rel = d[:, None, None] * Q_BLOCK + r[None, None, :] - r[None, :, None]
    return _table_lookup(rel_bias[:, :DIFF_HEADS], _t5_bucket(rel))


def _swa_bias_tiles(rel_bias):
    kb = 3 * SWA_BLOCK
    rel = jnp.arange(kb)[None, :] - SWA_WINDOW - jnp.arange(SWA_BLOCK)[:, None]
    bias = _table_lookup(rel_bias[:, DIFF_HEADS:], _t5_bucket(rel))
    return jnp.where((jnp.abs(rel) <= SWA_WINDOW)[None], bias, NEG_BIG)


def _matmul_kernel(x_ref, w_ref, o_ref):
    o_ref[...] = jnp.dot(x_ref[...].astype(BF16), w_ref[...],
                         preferred_element_type=F32).astype(o_ref.dtype)


def _matmul(x, w, out_dtype, tm, tn, name):
    m, k = x.shape
    n = w.shape[1]
    return pl.pallas_call(
        _matmul_kernel,
        grid=(n // tn, m // tm),
        in_specs=[pl.BlockSpec((tm, k), lambda j, i: (i, 0)),
                  pl.BlockSpec((k, tn), lambda j, i: (0, j))],
        out_specs=pl.BlockSpec((tm, tn), lambda j, i: (i, j)),
        out_shape=jax.ShapeDtypeStruct((m, n), out_dtype),
        compiler_params=_cparams(2),
        name=name,
    )(x, w)


LOG2_E = math.log2(math.e)
DIFF_QSCALE = HEAD_DIM ** -0.5 * LOG2_E
DIFF_QROWS = 2 * Q_BLOCK
DIFF_UNROLL = 4


def _diff_attn_kernel(q_ref, k_ref, v_ref, tb_ref, lam_ref, subln_ref, o_ref, s_scr, vx_scr, *, nq, lam_init):
    qr = DIFF_QROWS
    vx_scr[:, 0:DIFF_V] = v_ref[...]
    vx_scr[:, DIFF_V:2 * DIFF_V] = jnp.ones((v_ref.shape[0], DIFF_V), vx_scr.dtype)
    lv = lam_ref[...]
    lam = (jnp.exp(jnp.sum(lv[0:1] * lv[1:2], axis=-1, keepdims=True))
           - jnp.exp(jnp.sum(lv[2:3] * lv[3:4], axis=-1, keepdims=True)) + lam_init)
    lane = lax.broadcasted_iota(jnp.int32, (qr, 128), 1)
    zero = jnp.zeros((qr, 128), q_ref.dtype)

    def scores(qb, scr):
        r0 = pl.multiple_of(qb * qr, qr)
        q = q_ref[pl.ds(r0, qr), :]
        qz = jnp.concatenate([jnp.where(lane < HEAD_DIM, q, zero),
                              jnp.where(lane >= HEAD_DIM, q, zero)], axis=0)
        for kp in range(nq // 2):
            kblk = k_ref[kp * qr:(kp + 1) * qr, :]
            s = lax.dot_general(qz, kblk, (((1,), (1,)), ((), ())), preferred_element_type=F32)
            for a in range(2):
                d0 = nq - 1 - (2 * qb + a) + 2 * kp
                bias = jnp.concatenate([tb_ref[0, d0], tb_ref[0, d0 + 1]], axis=1)
                for c in range(2):
                    rows = slice(c * qr + a * Q_BLOCK, c * qr + (a + 1) * Q_BLOCK)
                    scr[rows, kp * qr:(kp + 1) * qr] = s[rows] + bias

    def softmax(scr):
        s = scr[...]
        return jnp.exp2(s - jnp.max(s, axis=-1, keepdims=True)).astype(BF16)

    def finish(qb, e):
        r0 = pl.multiple_of(qb * qr, qr)
        pv = jnp.dot(e, vx_scr[...], preferred_element_type=F32)
        pv = pv[:, 0:DIFF_V] / pv[:, DIFF_V:2 * DIFF_V]
        o = pv[:qr] - lam * pv[qr:]
        o = o * lax.rsqrt(jnp.mean(o * o, axis=-1, keepdims=True) + RMS_EPS) * subln_ref[...]
        o_ref[pl.ds(r0, qr), :] = (o * (1.0 - lam_init)).astype(o_ref.dtype)

    def body(i, carry):
        blocks = [i * DIFF_UNROLL + j for j in range(DIFF_UNROLL)]
        for j, qb in enumerate(blocks):
            scores(qb, s_scr.at[j])
        probs = [softmax(s_scr.at[j]) for j in range(DIFF_UNROLL)]
        for qb, e in zip(blocks, probs):
            finish(qb, e)
        return carry

    lax.fori_loop(0, nq // (2 * DIFF_UNROLL), body, 0)


def _diff_attention(za, tb, lam_vecs, subln, batch, seq, layer_idx):
    nq = seq // Q_BLOCK
    lam_init = 0.8 - 0.6 * math.exp(-0.3 * layer_idx)
    kern = functools.partial(_diff_attn_kernel, nq=nq, lam_init=lam_init)
    qcol, kcol, vcol = ZA_AQ // 128, ZA_AK // 128, ZA_AV // 128
    return pl.pallas_call(
        kern,
        grid=(batch, DIFF_HEADS),
        in_specs=[
            pl.BlockSpec((seq, 128), lambda b, h: (b, qcol + h)),
            pl.BlockSpec((seq, 128), lambda b, h: (b, kcol + h)),
            pl.BlockSpec((seq, 128), lambda b, h: (b, vcol + h)),
            pl.BlockSpec((1, 2 * nq - 1, Q_BLOCK, Q_BLOCK), lambda b, h: (h, 0, 0, 0)),
            pl.BlockSpec((4, HEAD_DIM), lambda b, h: (0, 0)),
            pl.BlockSpec((1, DIFF_V), lambda b, h: (0, 0)),
        ],
        out_specs=pl.BlockSpec((seq, 128), lambda b, h: (b, h)),
        out_shape=jax.ShapeDtypeStruct((batch * seq, BRANCH_WIDTH), BF16),
        scratch_shapes=[pltpu.VMEM((DIFF_UNROLL, 2 * DIFF_QROWS, seq), F32),
                        pltpu.VMEM((seq, 2 * DIFF_V), BF16)],
        compiler_params=_cparams(2),
        name="diff_attention",
    )(za, za, za, tb, lam_vecs.astype(F32), subln.astype(F32).reshape(1, DIFF_V))


def _swa_kernel(q_ref, k_ref, v_ref, bias_ref, sink_ref, o_ref, kp_scr, vp_scr, *, nb):
    seq = q_ref.shape[0]
    blk = SWA_BLOCK
    pad = jnp.zeros((blk, kp_scr.shape[1]), kp_scr.dtype)
    for scr, ref in ((kp_scr, k_ref), (vp_scr, v_ref)):
        scr[0:blk] = pad
        scr[blk + seq:2 * blk + seq] = pad
        scr[blk:blk + seq] = ref[...]
    krow = lax.broadcasted_iota(jnp.int32, (3 * blk, blk), 0)
    lane = lax.broadcasted_iota(jnp.int32, (blk, 128), 1)
    low = lane < HEAD_DIM
    zero = jnp.zeros((blk, 128), q_ref.dtype)
    scale = HEAD_DIM ** -0.5
    rep = SWA_HEADS // SWA_KV_HEADS

    def body(n, carry):
        r0 = pl.multiple_of(n * blk, blk)
        q = q_ref[pl.ds(r0, blk), :]
        kw = kp_scr[pl.ds(r0, 3 * blk), :]
        vw = vp_scr[pl.ds(r0, 3 * blk), :]
        outside = ((krow < blk) & (n == 0)) | ((krow >= 2 * blk) & (n == nb - 1))
        heads = range(SWA_HEADS)
        qms = [jnp.where(low if hd % 2 == 0 else jnp.logical_not(low),
                         q[:, (hd // 2) * 128:(hd // 2 + 1) * 128] * scale, zero) for hd in heads]
        ss = [lax.dot_general(kw[:, (hd // rep) * 128:(hd // rep + 1) * 128], qms[hd], (((1,), (1,)), ((), ())),
                              preferred_element_type=F32) for hd in heads]
        ss = [jnp.where(outside, NEG_BIG, ss[hd] + bias_ref[hd]) for hd in heads]
        sinks = [sink_ref[hd][:, 0:1] for hd in heads]
        ms = [jnp.maximum(jnp.max(ss[hd], axis=0, keepdims=True), sinks[hd]) for hd in heads]
        es = [jnp.exp(ss[hd] - ms[hd]) for hd in heads]
        dens = [jnp.sum(es[hd], axis=0, keepdims=True) + jnp.exp(sinks[hd] - ms[hd]) for hd in heads]
        ps = [(es[hd] / dens[hd]).astype(BF16) for hd in heads]
        halves = [lax.dot_general(ps[hd], vw[:, (hd // rep) * 128:(hd // rep + 1) * 128], (((0,), (0,)), ((), ())),
                                  preferred_element_type=F32) for hd in heads]
        o = jnp.concatenate([jnp.where(low, halves[2 * c], halves[2 * c + 1]) for c in range(SWA_HEADS // 2)], axis=1)
        o_ref[pl.ds(r0, blk), :] = o.astype(o_ref.dtype)
        return carry

    lax.fori_loop(0, nb, body, 0)


def _swa_attention(za, bias, sink, batch, seq):
    nb = seq // SWA_BLOCK
    kern = functools.partial(_swa_kernel, nb=nb)
    sink_b = jnp.broadcast_to(sink.astype(F32).reshape(SWA_HEADS, 1, 1), (SWA_HEADS, 1, 128))
    kvw = 2 * SWA_KV_HEADS * HEAD_DIM
    return pl.pallas_call(
        kern,
        grid=(batch,),
        in_specs=[
            pl.BlockSpec((seq, BRANCH_WIDTH), lambda b: (b, ZA_CQ // BRANCH_WIDTH)),
            pl.BlockSpec((seq, kvw), lambda b: (b, ZA_CK // kvw)),
            pl.BlockSpec((seq, kvw), lambda b: (b, ZA_CV // kvw)),
            pl.BlockSpec((SWA_HEADS, 3 * SWA_BLOCK, SWA_BLOCK), lambda b: (0, 0, 0)),
            pl.BlockSpec((SWA_HEADS, 1, 128), lambda b: (0, 0, 0)),
        ],
        out_specs=pl.BlockSpec((seq, BRANCH_WIDTH), lambda b: (b, 0)),
        out_shape=jax.ShapeDtypeStruct((batch * seq, BRANCH_WIDTH), BF16),
        scratch_shapes=[pltpu.VMEM((seq + 2 * SWA_BLOCK, kvw), BF16),
                        pltpu.VMEM((seq + 2 * SWA_BLOCK, kvw), BF16)],
        compiler_params=_cparams(1),
        name="swa_attention",
    )(za, za, za, bias.transpose(0, 2, 1), sink_b)


def _shift_rows(x, d):
    rows = x.shape[0]
    t = lax.broadcasted_iota(jnp.int32, x.shape, 0)
    rolled = pltpu.roll(x, (-d) % rows, 0)
    return jnp.where((t + d >= 0) & (t + d < rows), rolled, 0.0)


def _gdn_conv_kernel(x_ref, w_ref, o_ref):
    j = pl.program_id(1)
    x = x_ref[...]
    w = w_ref[...]
    half = GDN_CONV // 2
    acc = x * w[half:half + 1]
    for tap in range(GDN_CONV):
        if tap != half:
            acc = acc + _shift_rows(x, tap - half) * w[tap:tap + 1]
    h = acc * jax.nn.sigmoid(acc)
    inv = lax.rsqrt(jnp.sum(h * h, axis=-1, keepdims=True) + RMS_EPS)
    factor = jnp.where(j < 2 * GDN_HEADS, inv, 1.0) * jnp.where(j < GDN_HEADS, GDN_DK ** -0.5, 1.0)
    o_ref[...] = h * factor


def _gdn_conv(zb, conv_w, batch, seq):
    c0 = ZB_QKV // 128
    return pl.pallas_call(
        _gdn_conv_kernel,
        grid=(batch, GDN_QKV // 128),
        in_specs=[pl.BlockSpec((seq, 128), lambda b, j: (b, c0 + j)),
                  pl.BlockSpec((GDN_CONV, 128), lambda b, j: (0, j))],
        out_specs=pl.BlockSpec((seq, 128), lambda b, j: (b, j)),
        out_shape=jax.ShapeDtypeStruct((batch * seq, GDN_QKV), F32),
        compiler_params=_cparams(2),
        name="gdn_conv",
    )(zb, conv_w.astype(F32))


def _gdn_gate_kernel(ab_ref, alog_ref, dtb_ref, gc_ref, gd_ref, gl_ref, beta_ref):
    ab = ab_ref[...]
    a = ab[0:GDN_CHAINS]
    seq = a.shape[1]
    xs = a + dtb_ref[...]
    softplus = jnp.maximum(xs, 0.0) + jnp.log1p(jnp.exp(-jnp.abs(xs)))
    g = -jnp.exp(alog_ref[...]) * softplus
    beta_ref[...] = jax.nn.sigmoid(ab[GDN_CHAINS:2 * GDN_CHAINS])
    pos = lax.broadcasted_iota(jnp.int32, g.shape, 1) % GDN_CHUNK
    cf = g
    cr = g
    s = 1
    while s < GDN_CHUNK:
        cf = cf + jnp.where(pos >= s, pltpu.roll(cf, s, 1), 0.0)
        cr = cr + jnp.where(pos < GDN_CHUNK - s, pltpu.roll(cr, seq - s, 1), 0.0)
        s *= 2
    fwd = lax.broadcasted_iota(jnp.int32, g.shape, 0) < GDN_HEADS
    gc_ref[...] = jnp.where(fwd, cf, cr)
    gd_ref[...] = jnp.where(fwd, cr, cf) - g
    gl_ref[...] = cf + cr - g


def _gdn_gates(ab_t, a_log, dt_bias, batch, seq):
    row = pl.BlockSpec((GDN_CHAINS, seq), lambda b: (0, b))
    shp = jax.ShapeDtypeStruct((GDN_CHAINS, batch * seq), F32)
    par = pl.BlockSpec((GDN_CHAINS, 1), lambda b: (0, 0))
    return pl.pallas_call(
        _gdn_gate_kernel,
        grid=(batch,),
        in_specs=[pl.BlockSpec((2 * GDN_CHAINS, seq), lambda b: (0, b)), par, par],
        out_specs=[row, row, row, row],
        out_shape=[shp, shp, shp, shp],
        compiler_params=_cparams(1),
        name="gdn_gates",
    )(ab_t, a_log.astype(F32).reshape(GDN_CHAINS, 1), dt_bias.astype(F32).reshape(GDN_CHAINS, 1))


def _hdot(a, b):
    return jnp.dot(a.astype(BF16), b.astype(BF16), preferred_element_type=F32)


def _gdn_local_kernel(qkv_ref, gcc_ref, gdc_ref, betac_ref, gcr_ref,
                      u_ref, w_ref, qg_ref, qk_ref, kgt_ref):
    g = GDN_GROUP
    c = GDN_CHUNK
    ri = lax.broadcasted_iota(jnp.int32, (g, g), 0)
    ci = lax.broadcasted_iota(jnp.int32, (g, g), 1)
    same = (ri // c) == (ci // c)
    ahead = jnp.where(same, ri - ci, -g)
    behind = jnp.where(same, ci - ri, -g)
    eye_s = (lax.broadcasted_iota(jnp.int32, (c, g), 0)
             == lax.broadcasted_iota(jnp.int32, (c, g), 1) % c).astype(F32)

    def block_diag(xs):
        return jnp.where(same, jnp.concatenate([xs] * (g // c), axis=0), 0.0)

    def row_blocks_sum(xd):
        out = xd[0:c]
        for i in range(1, g // c):
            out = out + xd[i * c:(i + 1) * c]
        return out

    chains = [(h, d) for h in range(GDN_HEADS) for d in range(2)]
    xs_l, xd_l, p_l, rhs_l = [], [], [], []
    for h in range(GDN_HEADS):
        q = qkv_ref[:, h * GDN_DK:(h + 1) * GDN_DK]
        k = qkv_ref[:, (GDN_HEADS + h) * GDN_DK:(GDN_HEADS + h + 1) * GDN_DK]
        v = qkv_ref[:, (2 * GDN_HEADS + h) * GDN_DK:(2 * GDN_HEADS + h + 1) * GDN_DK]
        kbf = k.astype(BF16)
        qk_raw = lax.dot_general(q.astype(BF16), kbf, (((1,), (1,)), ((), ())), preferred_element_type=F32)
        for d in range(2):
            ch = d * GDN_HEADS + h
            lo, hi = ch * GDN_DK, (ch + 1) * GDN_DK
            gcc = gcc_ref[:, ch:ch + 1]
            gdc = gdc_ref[:, ch:ch + 1]
            beta = betac_ref[:, ch:ch + 1]
            gcr = gcr_ref[ch:ch + 1, :]
            order = ahead if d == 0 else behind
            kb = k * beta
            kk = lax.dot_general(kb.astype(BF16), kbf, (((1,), (1,)), ((), ())), preferred_element_type=F32)
            decay = jnp.exp(jnp.where(order >= 0, gcc - gcr, NEG_BIG))
            xd = -jnp.where(order > 0, kk * decay, 0.0)
            qkm = qk_raw * decay
            egc = jnp.exp(gcc)
            qg_ref[:, lo:hi] = (q * egc).astype(qg_ref.dtype)
            qk_ref[:, lo:hi] = (qkm[:, 0:128] + qkm[:, 128:256]).astype(qk_ref.dtype)
            kgt_ref[0, lo:hi, :] = (k * jnp.exp(gdc)).T.astype(kgt_ref.dtype)
            xs = row_blocks_sum(xd)
            xs_l.append(xs)
            xd_l.append(xd)
            p_l.append(eye_s + xs)
            rhs_l.append(jnp.concatenate([v * beta, kb * egc], axis=1).astype(BF16))
    n_ch = len(chains)
    xs_l = [_hdot(xs_l[i], xd_l[i]) for i in range(n_ch)]
    for _ in range(4):
        r_l = [_hdot(jnp.concatenate([p_l[i], xs_l[i]], axis=0), block_diag(xs_l[i])) for i in range(n_ch)]
        p_l = [p_l[i] + r_l[i][0:c] for i in range(n_ch)]
        xs_l = [r_l[i][c:2 * c] for i in range(n_ch)]
    p_l = [p_l[i] + _hdot(p_l[i], block_diag(xs_l[i])) for i in range(n_ch)]
    sol_l = [_hdot(block_diag(p_l[i]), rhs_l[i]) for i in range(n_ch)]
    for i, (h, d) in enumerate(chains):
        ch = d * GDN_HEADS + h
        lo, hi = ch * GDN_DK, (ch + 1) * GDN_DK
        u_ref[:, lo:hi] = sol_l[i][:, 0:GDN_DV]
        w_ref[:, lo:hi] = sol_l[i][:, GDN_DV:].astype(w_ref.dtype)


def _gdn_local(qkvn, gcc, gdc, betac, gcr, batch, seq):
    ng = seq // GDN_GROUP
    wide = GDN_CHAINS * GDN_DK
    colb = pl.BlockSpec((GDN_GROUP, GDN_CHAINS), lambda b, s: (b * ng + s, 0))
    outb = pl.BlockSpec((GDN_GROUP, wide), lambda b, s: (b * ng + s, 0))
    n = batch * seq
    return pl.pallas_call(
        _gdn_local_kernel,
        grid=(batch, ng),
        in_specs=[pl.BlockSpec((GDN_GROUP, GDN_QKV), lambda b, s: (b * ng + s, 0)),
                  colb, colb, colb,
                  pl.BlockSpec((GDN_CHAINS, GDN_GROUP), lambda b, s: (0, b * ng + s))],
        out_specs=[outb, outb, outb, outb,
                   pl.BlockSpec((1, wide, GDN_GROUP), lambda b, s: (b, 0, s))],
        out_shape=[jax.ShapeDtypeStruct((n, wide), F32),
                   jax.ShapeDtypeStruct((n, wide), BF16),
                   jax.ShapeDtypeStruct((n, wide), BF16),
                   jax.ShapeDtypeStruct((n, wide), BF16),
                   jax.ShapeDtypeStruct((batch, wide, seq), BF16)],
        compiler_params=_cparams(2),
        name="gdn_local",
    )(qkvn, gcc, gdc, betac, gcr)


def _gdn_scan_kernel(uf_ref, ub_ref, wf_ref, wb_ref, qgf_ref, qgb_ref, qkf_ref, qkb_ref,
                     kgf_ref, kgb_ref, glf_ref, glb_ref, of_ref, ob_ref, s_scr):
    @pl.when(pl.program_id(1) == 0)
    def _():
        s_scr[...] = jnp.zeros_like(s_scr)

    c = GDN_CHUNK
    nc = GDN_GROUP // c
    zeros = jnp.zeros((c, GDN_DV), BF16)
    dirs = ((uf_ref, wf_ref, qgf_ref, qkf_ref, kgf_ref, glf_ref, of_ref),
            (ub_ref, wb_ref, qgb_ref, qkb_ref, kgb_ref, glb_ref, ob_ref))
    states = [s_scr[ch] for ch in range(GDN_CHAINS)]
    chains = [(d, h) for d in range(2) for h in range(GDN_HEADS)]
    for step in range(nc):
        tops, v2s = [], []
        for d, h in chains:
            u_ref, w_ref, qg_ref = dirs[d][0:3]
            i = step if d == 0 else nc - 1 - step
            rows = slice(i * c, (i + 1) * c)
            lo, hi = h * GDN_DK, (h + 1) * GDN_DK
            sb = states[d * GDN_HEADS + h].astype(BF16)
            top = jnp.dot(jnp.concatenate([w_ref[rows, lo:hi], qg_ref[rows, lo:hi]], axis=0), sb,
                          preferred_element_type=F32)
            vb = (u_ref[rows, lo:hi] - top[0:c]).astype(BF16)
            v2s.append(jnp.concatenate([vb, zeros] if i % 2 == 0 else [zeros, vb], axis=0))
            tops.append(top)
        for n, (d, h) in enumerate(chains):
            qk_ref, kg_ref, gl_ref, o_ref = dirs[d][3:7]
            ch = d * GDN_HEADS + h
            i = step if d == 0 else nc - 1 - step
            rows = slice(i * c, (i + 1) * c)
            pair = slice((i // 2) * 2 * c, (i // 2 + 1) * 2 * c)
            lo, hi = h * GDN_DK, (h + 1) * GDN_DK
            bot = jnp.dot(jnp.concatenate([qk_ref[rows, lo:hi], kg_ref[0, lo:hi, pair]], axis=0), v2s[n],
                          preferred_element_type=F32)
            o_ref[rows, lo:hi] = tops[n][c:2 * c] + bot[0:c]
            decay = jnp.exp(gl_ref[ch:ch + 1, i * c:i * c + 1])
            states[ch] = states[ch] * decay + bot[c:c + GDN_DK]
    for ch in range(GDN_CHAINS):
        s_scr[ch] = states[ch]


def _gdn_scan(u, w, qg, qk, kgt, gl, batch, seq):
    ng = seq // GDN_GROUP
    half = GDN_HEADS * GDN_DK
    fwd = lambda b, s: (b * ng + s, 0)
    bwd = lambda b, s: (b * ng + ng - 1 - s, 1)
    blk = (GDN_GROUP, half)
    n = batch * seq
    pair = lambda arr: (arr, arr)
    out_f = pl.BlockSpec(blk, fwd)
    out_b = pl.BlockSpec(blk, lambda b, s: (b * ng + ng - 1 - s, 0))
    in_specs = []
    for _ in range(4):
        in_specs += [pl.BlockSpec(blk, fwd), pl.BlockSpec(blk, bwd)]
    in_specs += [pl.BlockSpec((1, half, GDN_GROUP), lambda b, s: (b, 0, s)),
                 pl.BlockSpec((1, half, GDN_GROUP), lambda b, s: (b, 1, ng - 1 - s)),
                 pl.BlockSpec((GDN_CHAINS, GDN_GROUP), lambda b, s: (0, b * ng + s)),
                 pl.BlockSpec((GDN_CHAINS, GDN_GROUP), lambda b, s: (0, b * ng + ng - 1 - s))]
    return pl.pallas_call(
        _gdn_scan_kernel,
        grid=(batch, ng),
        in_specs=in_specs,
        out_specs=[out_f, out_b],
        out_shape=[jax.ShapeDtypeStruct((n, half), F32), jax.ShapeDtypeStruct((n, half), F32)],
        scratch_shapes=[pltpu.VMEM((GDN_CHAINS, GDN_DK, GDN_DV), F32)],
        compiler_params=_cparams(2),
        name="gdn_scan",
    )(*pair(u), *pair(w), *pair(qg), *pair(qk), *pair(kgt), *pair(gl))


def _layer_norm(r, g, b):
    mu = jnp.mean(r, axis=-1, keepdims=True)
    var = jnp.mean(jnp.square(r - mu), axis=-1, keepdims=True)
    return (r - mu) * lax.rsqrt(var + LN_EPS) * g + b


def _merge_kernel(ya_ref, of_ref, ob_ref, gate_ref, yc_ref, gz_ref, x_ref, wb_ref, wo_ref,
                  nw_ref, lng_ref, lnb_ref, o_ref):
    segs = []
    for h in range(GDN_HEADS):
        sl = slice(h * GDN_DV, (h + 1) * GDN_DV)
        o = of_ref[:, sl] + ob_ref[:, sl]
        o = o * lax.rsqrt(jnp.mean(o * o, axis=-1, keepdims=True) + RMS_EPS) * nw_ref[...]
        gt = gate_ref[:, sl]
        segs.append(o * (gt * jax.nn.sigmoid(gt)))
    yb = jnp.concatenate(segs, axis=1).astype(BF16)
    merged = None
    for n, y in enumerate((ya_ref[...], yb, yc_ref[...])):
        proj = jnp.dot(y, wb_ref[n], preferred_element_type=F32)
        term = jax.nn.sigmoid(gz_ref[:, n * D_MODEL:(n + 1) * D_MODEL]) * proj
        merged = term if merged is None else merged + term
    hmix = jnp.dot(merged.astype(BF16), wo_ref[...], preferred_element_type=F32)
    o_ref[...] = _layer_norm(DEEPNORM_ALPHA * x_ref[...] + hmix, lng_ref[...], lnb_ref[...])


def _merge(ya, o_f, o_b, zb, yc, x, w_branch, w_out, norm_w, ln_g, ln_b, tm):
    n = x.shape[0]
    row = lambda width, col=0: pl.BlockSpec((tm, width), lambda i, col=col: (i, col))
    full = lambda shape: pl.BlockSpec(shape, lambda i: (0,) * len(shape), pipeline_mode=pl.Buffered(1))
    return pl.pallas_call(
        _merge_kernel,
        grid=(n // tm,),
        in_specs=[row(BRANCH_WIDTH), row(BRANCH_WIDTH), row(BRANCH_WIDTH),
                  row(BRANCH_WIDTH, ZB_GATE // BRANCH_WIDTH), row(BRANCH_WIDTH),
                  row(N_BRANCHES * D_MODEL, ZB_GZ // (N_BRANCHES * D_MODEL)), row(D_MODEL),
                  full((N_BRANCHES, BRANCH_WIDTH, D_MODEL)), full((D_MODEL, D_MODEL)),
                  full((1, GDN_DV)), full((1, D_MODEL)), full((1, D_MODEL))],
        out_specs=row(D_MODEL),
        out_shape=jax.ShapeDtypeStruct((n, D_MODEL), F32),
        compiler_params=_cparams(1),
        name="merge_ln",
    )(ya, o_f, o_b, zb, yc, zb, x, w_branch, w_out, norm_w.astype(F32).reshape(1, GDN_DV),
      ln_g.astype(F32).reshape(1, D_MODEL), ln_b.astype(F32).reshape(1, D_MODEL))


FFN_SPLIT = 2


def _ffn_kernel(xp_ref, x_ref, xn_ref, wup_ref, cw_ref, cb_ref, wdn_ref, lng_ref, lnb_ref, o_ref, *, tiles_per_seq):
    i = pl.program_id(0)
    tm = x_ref.shape[0]
    halo = V7X_SUBLANES
    x = x_ref[...]
    xe = jnp.concatenate([xp_ref[...], x, xn_ref[...]], axis=0).astype(BF16)
    xb = x.astype(BF16)
    rows = lax.broadcasted_iota(jnp.int32, (tm + 2 * halo, 1), 0)
    first = (i % tiles_per_seq) == 0
    last = (i % tiles_per_seq) == tiles_per_seq - 1
    keep = jnp.logical_not(((rows < halo) & first) | ((rows >= tm + halo) & last))
    fc = D_FF // FFN_SPLIT
    ext = tm + 2 * halo
    acc = None
    for c in range(FFN_SPLIT):
        ge = jnp.dot(xe, wup_ref[:, c * fc:(c + 1) * fc], preferred_element_type=F32)
        ge = jnp.where(keep, ge, 0.0)
        up = jnp.dot(xb, wup_ref[:, D_FF + c * fc:D_FF + (c + 1) * fc], preferred_element_type=F32)
        cw = cw_ref[:, c * fc:(c + 1) * fc]
        conv = (pltpu.roll(ge, 1, 0) * cw[0:1] + ge * cw[1:2] + pltpu.roll(ge, ext - 1, 0) * cw[2:3])
        gate = conv[halo:halo + tm] + cb_ref[:, c * fc:(c + 1) * fc]
        act = (gate * jax.nn.sigmoid(gate) * up).astype(BF16)
        part = jnp.dot(act, wdn_ref[c * fc:(c + 1) * fc, :], preferred_element_type=F32)
        acc = part if acc is None else acc + part
    o_ref[...] = _layer_norm(DEEPNORM_ALPHA * x + acc, lng_ref[...], lnb_ref[...])


def _ffn(x, w_up, conv_w, conv_b, w_down, ln_g, ln_b, seq, tm):
    n = x.shape[0]
    halo = V7X_SUBLANES
    per = tm // halo
    nh = n // halo
    full = lambda shape: pl.BlockSpec(shape, lambda i: (0,) * len(shape), pipeline_mode=pl.Buffered(1))
    kern = functools.partial(_ffn_kernel, tiles_per_seq=seq // tm)
    return pl.pallas_call(
        kern,
        grid=(n // tm,),
        in_specs=[pl.BlockSpec((halo, D_MODEL), lambda i: (jnp.maximum(i * per - 1, 0), 0)),
                  pl.BlockSpec((tm, D_MODEL), lambda i: (i, 0)),
                  pl.BlockSpec((halo, D_MODEL), lambda i: (jnp.minimum((i + 1) * per, nh - 1), 0)),
                  full((D_MODEL, 2 * D_FF)), full((FFN_CONV, D_FF)), full((1, D_FF)),
                  full((D_FF, D_MODEL)), full((1, D_MODEL)), full((1, D_MODEL))],
        out_specs=pl.BlockSpec((tm, D_MODEL), lambda i: (i, 0)),
        out_shape=jax.ShapeDtypeStruct((n, D_MODEL), F32),
        compiler_params=_cparams(1),
        name="ffn_ln",
    )(x, x, x, w_up, conv_w.astype(F32), conv_b.astype(F32).reshape(1, D_FF), w_down,
      ln_g.astype(F32).reshape(1, D_MODEL), ln_b.astype(F32).reshape(1, D_MODEL))


def _pack_w_in(w):
    widths = (512, 512, 512, GDN_QKV, 512, 8, 8, 512, 128, 128, N_BRANCHES * D_MODEL)
    offs = [0]
    for wd in widths:
        offs.append(offs[-1] + wd)
    seg = lambda i: w[:, offs[i]:offs[i + 1]]
    aq, ak, av, bqkv, bgate, ba, bb, cq, ck, cv, gz = (seg(i) for i in range(len(widths)))
    dup = lambda t: jnp.concatenate([t[:, :HEAD_DIM], t[:, :HEAD_DIM], t[:, HEAD_DIM:], t[:, HEAD_DIM:]], axis=1)
    w_a = jnp.concatenate([aq * DIFF_QSCALE, ak, av, cq, dup(ck), dup(cv)], axis=1)
    pad = jnp.zeros((w.shape[0], ZB_COLS - ZB_AB - 16), w.dtype)
    w_b = jnp.concatenate([gz, bqkv, bgate, ba, bb, pad], axis=1)
    return w_a.astype(BF16), w_b.astype(BF16)


def kernel(x, rel_bias, w_in, diff_lambda, diff_subln, gdn_conv, gdn_a_log, gdn_dt_bias, gdn_norm, swa_sink,
           w_branch, w_out, ln1_g, ln1_b, ffn_up, ffn_conv, ffn_conv_b, ffn_down, ln2_g, ln2_b):
    batch, seq, d = x.shape
    assert d == D_MODEL and seq % GDN_GROUP == 0 and seq % (DIFF_QROWS * DIFF_UNROLL) == 0
    n = batch * seq
    tm = 256
    tb_diff = _diff_bias_tiles(rel_bias, seq) * LOG2_E
    tb_swa = _swa_bias_tiles(rel_bias)
    xf = x.reshape(n, d)
    for l in range(DEPTH):
        w_a, w_b = _pack_w_in(w_in[l])
        za = _matmul(xf, w_a, BF16, 1024, ZA_COLS // 2, "in_proj_a")
        zb = _matmul(xf, w_b, F32, 1024, ZB_COLS // 3, "in_proj_b")
        ya = _diff_attention(za, tb_diff, diff_lambda[l], diff_subln[l], batch, seq, l)
        yc = _swa_attention(za, tb_swa, swa_sink[l], batch, seq)
        qkvn = _gdn_conv(zb, gdn_conv[l], batch, seq)
        ab_t = zb[:, ZB_AB:ZB_AB + 2 * GDN_CHAINS].T
        gc, gd, gl, beta = _gdn_gates(ab_t, gdn_a_log[l], gdn_dt_bias[l], batch, seq)
        u, w, qg, qk, kgt = _gdn_local(qkvn, gc.T, gd.T, beta.T, gc, batch, seq)
        o_f, o_b = _gdn_scan(u, w, qg, qk, kgt, gl, batch, seq)
        xf = _merge(ya, o_f, o_b, zb, yc, xf, w_branch[l].astype(BF16), w_out[l].astype(BF16),
                    gdn_norm[l], ln1_g[l], ln1_b[l], 2 * tm)
        xf = _ffn(xf, ffn_up[l].astype(BF16), ffn_conv[l], ffn_conv_b[l], ffn_down[l].astype(BF16),
                  ln2_g[l], ln2_b[l], seq, 2 * tm)
    return xf.reshape(batch, seq, d)
```

```python
import functools
import math

import jax
import jax.numpy as jnp
from jax import lax
from jax.experimental import pallas as pl
from jax.experimental.pallas import tpu as pltpu

F32 = jnp.float32
BF16 = jnp.bfloat16

D_MODEL = 1024
DEPTH = 2
HEAD_DIM = 64
DIFF_HEADS = 4
DIFF_V = 2 * HEAD_DIM
Q_BLOCK = 128
GDN_HEADS = 4
GDN_DK = 128
GDN_DV = 128
GDN_QKV = GDN_HEADS * (2 * GDN_DK + GDN_DV)
GDN_CONV = 5
GDN_CHUNK = 64
SWA_HEADS = 8
SWA_KV_HEADS = 2
SWA_WINDOW = 128
SWA_BLOCK = 128
BRANCH_WIDTH = 512
N_BRANCHES = 3
REL_BUCKETS = 32
REL_MAX_DIST = 128
D_FF = 2816
FFN_CONV = 3
DEEPNORM_ALPHA = (2 * DEPTH) ** 0.25
LN_EPS = 1e-5
RMS_EPS = 1e-6

V7X_LANES = 128
V7X_SUBLANES = 8
V7X_VMEM_BYTES = 64 * 1024 * 1024
VMEM_LIMIT = V7X_VMEM_BYTES * 7 // 8

NEG_BIG = -1e30

ZA_AQ, ZA_AK, ZA_AV = 0, 512, 1024
ZA_CQ = 1536
ZA_CK = 2048
ZA_CV = 2304
ZA_COLS = 2560
ZB_GZ = 0
ZB_QKV = 3072
ZB_GATE = 4608
ZB_COLS = 5120
ZC_COLS = 128

GDN_GROUP = 256
GDN_CHAINS = 2 * GDN_HEADS


def _cparams(n_grid):
    return pltpu.CompilerParams(dimension_semantics=("arbitrary",) * n_grid, vmem_limit_bytes=VMEM_LIMIT)


def _t5_bucket(rel):
    nb = REL_BUCKETS // 2
    ret = jnp.where(rel > 0, nb, 0)
    n = jnp.abs(rel)
    max_exact = nb // 2
    large = max_exact + (jnp.log(jnp.maximum(n, 1).astype(jnp.float32) / max_exact)
                         / math.log(REL_MAX_DIST / max_exact) * (nb - max_exact)).astype(jnp.int32)
    large = jnp.minimum(large, nb - 1)
    return ret + jnp.where(n < max_exact, n, large)


def _table_lookup(table, bucket):
    out = jnp.zeros((table.shape[1],) + bucket.shape, F32)
    for i in range(REL_BUCKETS):
        out = jnp.where(bucket[None] == i, table[i].astype(F32).reshape((-1,) + (1,) * bucket.ndim), out)
    return out


def _diff_bias_tiles(rel_bias, seq):
    nq = seq // Q_BLOCK
    d = jnp.arange(2 * nq - 1) - (nq - 1)
    r = jnp.arange(Q_BLOCK)
    rel = d[:, None, None] * Q_BLOCK + r[None, None, :] - r[None, :, None]
    return _table_lookup(rel_bias[:, :DIFF_HEADS], _t5_bucket(rel))


def _swa_bias_tiles(rel_bias):
    kb = 3 * SWA_BLOCK
    rel = jnp.arange(kb)[None, :] - SWA_WINDOW - jnp.arange(SWA_BLOCK)[:, None]
    bias = _table_lookup(rel_bias[:, DIFF_HEADS:], _t5_bucket(rel))
    return jnp.where((jnp.abs(rel) <= SWA_WINDOW)[None], bias, NEG_BIG)


def _matmul_kernel(x_ref, w_ref, o_ref):
    o_ref[...] = jnp.dot(x_ref[...].astype(BF16), w_ref[...],
                         preferred_element_type=F32).astype(o_ref.dtype)


def _matmul(x, w, out_dtype, tm, tn, name):
    m, k = x.shape
    n = w.shape[1]
    return pl.pallas_call(
        _matmul_kernel,
        grid=(n // tn, m // tm),
        in_specs=[pl.BlockSpec((tm, k), lambda j, i: (i, 0)),
                  pl.BlockSpec((k, tn), lambda j, i: (0, j))],
        out_specs=pl.BlockSpec((tm, tn), lambda j, i: (i, j)),
        out_shape=jax.ShapeDtypeStruct((m, n), out_dtype),
        compiler_params=_cparams(2),
        name=name,
    )(x, w)


LOG2_E = math.log2(math.e)
DIFF_QSCALE = HEAD_DIM ** -0.5 * LOG2_E
DIFF_QROWS = 2 * Q_BLOCK
DIFF_UNROLL = 4


def _diff_attn_kernel(q_ref, k_ref, v_ref, tb_ref, lam_ref, subln_ref, o_ref, s_scr, vx_scr, *, nq, lam_init):
    qr = DIFF_QROWS
    vx_scr[:, 0:DIFF_V] = v_ref[...]
    vx_scr[:, DIFF_V:2 * DIFF_V] = jnp.ones((v_ref.shape[0], DIFF_V), vx_scr.dtype)
    lv = lam_ref[...]
    lam = (jnp.exp(jnp.sum(lv[0:1] * lv[1:2], axis=-1, keepdims=True))
           - jnp.exp(jnp.sum(lv[2:3] * lv[3:4], axis=-1, keepdims=True)) + lam_init)
    lane = lax.broadcasted_iota(jnp.int32, (qr, 128), 1)
    zero = jnp.zeros((qr, 128), q_ref.dtype)

    def scores(qb, scr):
        r0 = pl.multiple_of(qb * qr, qr)
        q = q_ref[pl.ds(r0, qr), :]
        qz = jnp.concatenate([jnp.where(lane < HEAD_DIM, q, zero),
                              jnp.where(lane >= HEAD_DIM, q, zero)], axis=0)
        for kp in range(nq // 2):
            kblk = k_ref[kp * qr:(kp + 1) * qr, :]
            s = lax.dot_general(qz, kblk, (((1,), (1,)), ((), ())), preferred_element_type=F32)
            for a in range(2):
                d0 = nq - 1 - (2 * qb + a) + 2 * kp
                bias = jnp.concatenate([tb_ref[0, d0], tb_ref[0, d0 + 1]], axis=1)
                for c in range(2):
                    rows = slice(c * qr + a * Q_BLOCK, c * qr + (a + 1) * Q_BLOCK)
                    scr[rows, kp * qr:(kp + 1) * qr] = s[rows] + bias

    def softmax(scr):
        s = scr[...]
        return jnp.exp2(s - jnp.max(s, axis=-1, keepdims=True)).astype(BF16)

    def finish(qb, e):
        r0 = pl.multiple_of(qb * qr, qr)
        pv = jnp.dot(e, vx_scr[...], preferred_element_type=F32)
        pv = pv[:, 0:DIFF_V] / pv[:, DIFF_V:2 * DIFF_V]
        o = pv[:qr] - lam * pv[qr:]
        o = o * lax.rsqrt(jnp.mean(o * o, axis=-1, keepdims=True) + RMS_EPS) * subln_ref[...]
        o_ref[pl.ds(r0, qr), :] = (o * (1.0 - lam_init)).astype(o_ref.dtype)

    def body(i, carry):
        blocks = [i * DIFF_UNROLL + j for j in range(DIFF_UNROLL)]
        for j, qb in enumerate(blocks):
            scores(qb, s_scr.at[j])
        probs = [softmax(s_scr.at[j]) for j in range(DIFF_UNROLL)]
        for qb, e in zip(blocks, probs):
            finish(qb, e)
        return carry

    lax.fori_loop(0, nq // (2 * DIFF_UNROLL), body, 0)


def _diff_attention(za, tb, lam_vecs, subln, batch, seq, layer_idx):
    nq = seq // Q_BLOCK
    lam_init = 0.8 - 0.6 * math.exp(-0.3 * layer_idx)
    kern = functools.partial(_diff_attn_kernel, nq=nq, lam_init=lam_init)
    qcol, kcol, vcol = ZA_AQ // 128, ZA_AK // 128, ZA_AV // 128
    return pl.pallas_call(
        kern,
        grid=(batch, DIFF_HEADS),
        in_specs=[
            pl.BlockSpec((seq, 128), lambda b, h: (b, qcol + h)),
            pl.BlockSpec((seq, 128), lambda b, h: (b, kcol + h)),
            pl.BlockSpec((seq, 128), lambda b, h: (b, vcol + h)),
            pl.BlockSpec((1, 2 * nq - 1, Q_BLOCK, Q_BLOCK), lambda b, h: (h, 0, 0, 0)),
            pl.BlockSpec((4, HEAD_DIM), lambda b, h: (0, 0)),
            pl.BlockSpec((1, DIFF_V), lambda b, h: (0, 0)),
        ],
        out_specs=pl.BlockSpec((seq, 128), lambda b, h: (b, h)),
        out_shape=jax.ShapeDtypeStruct((batch * seq, BRANCH_WIDTH), BF16),
        scratch_shapes=[pltpu.VMEM((DIFF_UNROLL, 2 * DIFF_QROWS, seq), F32),
                        pltpu.VMEM((seq, 2 * DIFF_V), BF16)],
        compiler_params=_cparams(2),
        name="diff_attention",
    )(za, za, za, tb, lam_vecs.astype(F32), subln.astype(F32).reshape(1, DIFF_V))


def _swa_kernel(q_ref, k_ref, v_ref, bias_ref, sink_ref, o_ref, kp_scr, vp_scr, *, nb):
    seq = q_ref.shape[0]
    blk = SWA_BLOCK
    pad = jnp.zeros((blk, kp_scr.shape[1]), kp_scr.dtype)
    for scr, ref in ((kp_scr, k_ref), (vp_scr, v_ref)):
        scr[0:blk] = pad
        scr[blk + seq:2 * blk + seq] = pad
        scr[blk:blk + seq] = ref[...]
    krow = lax.broadcasted_iota(jnp.int32, (3 * blk, blk), 0)
    lane = lax.broadcasted_iota(jnp.int32, (blk, 128), 1)
    low = lane < HEAD_DIM
    zero = jnp.zeros((blk, 128), q_ref.dtype)
    scale = HEAD_DIM ** -0.5
    rep = SWA_HEADS // SWA_KV_HEADS

    def body(n, carry):
        r0 = pl.multiple_of(n * blk, blk)
        q = q_ref[pl.ds(r0, blk), :]
        kw = kp_scr[pl.ds(r0, 3 * blk), :]
        vw = vp_scr[pl.ds(r0, 3 * blk), :]
        outside = ((krow < blk) & (n == 0)) | ((krow >= 2 * blk) & (n == nb - 1))
        heads = range(SWA_HEADS)
        qms = [jnp.where(low if hd % 2 == 0 else jnp.logical_not(low),
                         q[:, (hd // 2) * 128:(hd // 2 + 1) * 128] * scale, zero) for hd in heads]
        ss = [lax.dot_general(kw[:, (hd // rep) * 128:(hd // rep + 1) * 128], qms[hd], (((1,), (1,)), ((), ())),
                              preferred_element_type=F32) for hd in heads]
        ss = [jnp.where(outside, NEG_BIG, ss[hd] + bias_ref[hd]) for hd in heads]
        sinks = [sink_ref[hd][:, 0:1] for hd in heads]
        ms = [jnp.maximum(jnp.max(ss[hd], axis=0, keepdims=True), sinks[hd]) for hd in heads]
        es = [jnp.exp(ss[hd] - ms[hd]) for hd in heads]
        dens = [jnp.sum(es[hd], axis=0, keepdims=True) + jnp.exp(sinks[hd] - ms[hd]) for hd in heads]
        ps = [(es[hd] / dens[hd]).astype(BF16) for hd in heads]
        halves = [lax.dot_general(ps[hd], vw[:, (hd // rep) * 128:(hd // rep + 1) * 128], (((0,), (0,)), ((), ())),
                                  preferred_element_type=F32) for hd in heads]
        o = jnp.concatenate([jnp.where(low, halves[2 * c], halves[2 * c + 1]) for c in range(SWA_HEADS // 2)], axis=1)
        o_ref[pl.ds(r0, blk), :] = o.astype(o_ref.dtype)
        return carry

    lax.fori_loop(0, nb, body, 0)


def _swa_attention(za, bias, sink, batch, seq):
    nb = seq // SWA_BLOCK
    kern = functools.partial(_swa_kernel, nb=nb)
    sink_b = jnp.broadcast_to(sink.astype(F32).reshape(SWA_HEADS, 1, 1), (SWA_HEADS, 1, 128))
    kvw = 2 * SWA_KV_HEADS * HEAD_DIM
    return pl.pallas_call(
        kern,
        grid=(batch,),
        in_specs=[
            pl.BlockSpec((seq, BRANCH_WIDTH), lambda b: (b, ZA_CQ // BRANCH_WIDTH)),
            pl.BlockSpec((seq, kvw), lambda b: (b, ZA_CK // kvw)),
            pl.BlockSpec((seq, kvw), lambda b: (b, ZA_CV // kvw)),
            pl.BlockSpec((SWA_HEADS, 3 * SWA_BLOCK, SWA_BLOCK), lambda b: (0, 0, 0)),
            pl.BlockSpec((SWA_HEADS, 1, 128), lambda b: (0, 0, 0)),
        ],
        out_specs=pl.BlockSpec((seq, BRANCH_WIDTH), lambda b: (b, 0)),
        out_shape=jax.ShapeDtypeStruct((batch * seq, BRANCH_WIDTH), BF16),
        scratch_shapes=[pltpu.VMEM((seq + 2 * SWA_BLOCK, kvw), BF16),
                        pltpu.VMEM((seq + 2 * SWA_BLOCK, kvw), BF16)],
        compiler_params=_cparams(1),
        name="swa_attention",
    )(za, za, za, bias.transpose(0, 2, 1), sink_b)


def _shift_rows(x, d):
    rows = x.shape[0]
    t = lax.broadcasted_iota(jnp.int32, x.shape, 0)
    rolled = pltpu.roll(x, (-d) % rows, 0)
    return jnp.where((t + d >= 0) & (t + d < rows), rolled, 0.0)


def _gdn_conv_kernel(x_ref, w_ref, o_ref):
    j = pl.program_id(1)
    x = x_ref[...].astype(F32)
    w = w_ref[...]
    half = GDN_CONV // 2
    acc = x * w[half:half + 1]
    for tap in range(GDN_CONV):
        if tap != half:
            acc = acc + _shift_rows(x, tap - half) * w[tap:tap + 1]
    h = acc * jax.nn.sigmoid(acc)
    inv = lax.rsqrt(jnp.sum(h * h, axis=-1, keepdims=True) + RMS_EPS)
    factor = jnp.where(j < 2 * GDN_HEADS, inv, 1.0) * jnp.where(j < GDN_HEADS, GDN_DK ** -0.5, 1.0)
    o_ref[...] = h * factor


def _gdn_conv(zb, conv_w, batch, seq):
    c0 = ZB_QKV // 128
    return pl.pallas_call(
        _gdn_conv_kernel,
        grid=(batch, GDN_QKV // 128),
        in_specs=[pl.BlockSpec((seq, 128), lambda b, j: (b, c0 + j)),
                  pl.BlockSpec((GDN_CONV, 128), lambda b, j: (0, j))],
        out_specs=pl.BlockSpec((seq, 128), lambda b, j: (b, j)),
        out_shape=jax.ShapeDtypeStruct((batch * seq, GDN_QKV), F32),
        compiler_params=_cparams(2),
        name="gdn_conv",
    )(zb, conv_w.astype(F32))


def _gdn_gate_kernel(ab_ref, alog_ref, dtb_ref, gc_ref, gd_ref, gl_ref, beta_ref):
    ab = ab_ref[...]
    a = ab[0:GDN_CHAINS]
    seq = a.shape[1]
    xs = a + dtb_ref[...]
    softplus = jnp.maximum(xs, 0.0) + jnp.log1p(jnp.exp(-jnp.abs(xs)))
    g = -jnp.exp(alog_ref[...]) * softplus
    beta_ref[...] = jax.nn.sigmoid(ab[GDN_CHAINS:2 * GDN_CHAINS])
    pos = lax.broadcasted_iota(jnp.int32, g.shape, 1) % GDN_CHUNK
    cf = g
    cr = g
    s = 1
    while s < GDN_CHUNK:
        cf = cf + jnp.where(pos >= s, pltpu.roll(cf, s, 1), 0.0)
        cr = cr + jnp.where(pos < GDN_CHUNK - s, pltpu.roll(cr, seq - s, 1), 0.0)
        s *= 2
    fwd = lax.broadcasted_iota(jnp.int32, g.shape, 0) < GDN_HEADS
    gc_ref[...] = jnp.where(fwd, cf, cr)
    gd_ref[...] = jnp.where(fwd, cr, cf) - g
    gl_ref[...] = cf + cr - g


def _gdn_gates(ab_t, a_log, dt_bias, batch, seq):
    row = pl.BlockSpec((GDN_CHAINS, seq), lambda b: (0, b))
    shp = jax.ShapeDtypeStruct((GDN_CHAINS, batch * seq), F32)
    par = pl.BlockSpec((GDN_CHAINS, 1), lambda b: (0, 0))
    return pl.pallas_call(
        _gdn_gate_kernel,
        grid=(batch,),
        in_specs=[pl.BlockSpec((2 * GDN_CHAINS, seq), lambda b: (0, b)), par, par],
        out_specs=[row, row, row, row],
        out_shape=[shp, shp, shp, shp],
        compiler_params=_cparams(1),
        name="gdn_gates",
    )(ab_t, a_log.astype(F32).reshape(GDN_CHAINS, 1), dt_bias.astype(F32).reshape(GDN_CHAINS, 1))


def _hdot(a, b):
    return jnp.dot(a.astype(BF16), b.astype(BF16), preferred_element_type=F32)


def _gdn_local_kernel(qkv_ref, gcc_ref, gdc_ref, betac_ref, gcr_ref,
                      u_ref, w_ref, qg_ref, qk_ref, kgt_ref):
    g = GDN_GROUP
    c = GDN_CHUNK
    ri = lax.broadcasted_iota(jnp.int32, (g, g), 0)
    ci = lax.broadcasted_iota(jnp.int32, (g, g), 1)
    same = (ri // c) == (ci // c)
    ahead = jnp.where(same, ri - ci, -g)
    behind = jnp.where(same, ci - ri, -g)
    eye_s = (lax.broadcasted_iota(jnp.int32, (c, g), 0)
             == lax.broadcasted_iota(jnp.int32, (c, g), 1) % c).astype(F32)

    def block_diag(xs):
        return jnp.where(same, jnp.concatenate([xs] * (g // c), axis=0), 0.0)

    def row_blocks_sum(xd):
        out = xd[0:c]
        for i in range(1, g // c):
            out = out + xd[i * c:(i + 1) * c]
        return out

    chains = [(h, d) for h in range(GDN_HEADS) for d in range(2)]
    xs_l, xd_l, p_l, rhs_l = [], [], [], []
    for h in range(GDN_HEADS):
        q = qkv_ref[:, h * GDN_DK:(h + 1) * GDN_DK]
        k = qkv_ref[:, (GDN_HEADS + h) * GDN_DK:(GDN_HEADS + h + 1) * GDN_DK]
        v = qkv_ref[:, (2 * GDN_HEADS + h) * GDN_DK:(2 * GDN_HEADS + h + 1) * GDN_DK]
        kbf = k.astype(BF16)
        qk_raw = lax.dot_general(q.astype(BF16), kbf, (((1,), (1,)), ((), ())), preferred_element_type=F32)
        for d in range(2):
            ch = d * GDN_HEADS + h
            lo, hi = ch * GDN_DK, (ch + 1) * GDN_DK
            gcc = gcc_ref[:, ch:ch + 1]
            gdc = gdc_ref[:, ch:ch + 1]
            beta = betac_ref[:, ch:ch + 1]
            gcr = gcr_ref[ch:ch + 1, :]
            order = ahead if d == 0 else behind
            kb = k * beta
            kk = lax.dot_general(kb.astype(BF16), kbf, (((1,), (1,)), ((), ())), preferred_element_type=F32)
            decay = jnp.exp(jnp.where(order >= 0, gcc - gcr, NEG_BIG))
            xd = -jnp.where(order > 0, kk * decay, 0.0)
            qkm = qk_raw * decay
            egc = jnp.exp(gcc)
            qg_ref[:, lo:hi] = (q * egc).astype(qg_ref.dtype)
            qk_ref[:, lo:hi] = (qkm[:, 0:128] + qkm[:, 128:256]).astype(qk_ref.dtype)
            kgt_ref[0, lo:hi, :] = (k * jnp.exp(gdc)).T.astype(kgt_ref.dtype)
            xs = row_blocks_sum(xd)
            xs_l.append(xs)
            xd_l.append(xd)
            p_l.append(eye_s + xs)
            rhs_l.append(jnp.concatenate([v * beta, kb * egc], axis=1).astype(BF16))
    n_ch = len(chains)
    xs_l = [_hdot(xs_l[i], xd_l[i]) for i in range(n_ch)]
    for _ in range(4):
        r_l = [_hdot(jnp.concatenate([p_l[i], xs_l[i]], axis=0), block_diag(xs_l[i])) for i in range(n_ch)]
        p_l = [p_l[i] + r_l[i][0:c] for i in range(n_ch)]
        xs_l = [r_l[i][c:2 * c] for i in range(n_ch)]
    p_l = [p_l[i] + _hdot(p_l[i], block_diag(xs_l[i])) for i in range(n_ch)]
    sol_l = [_hdot(block_diag(p_l[i]), rhs_l[i]) for i in range(n_ch)]
    for i, (h, d) in enumerate(chains):
        ch = d * GDN_HEADS + h
        lo, hi = ch * GDN_DK, (ch + 1) * GDN_DK
        u_ref[:, lo:hi] = sol_l[i][:, 0:GDN_DV]
        w_ref[:, lo:hi] = sol_l[i][:, GDN_DV:].astype(w_ref.dtype)


def _gdn_local(qkvn, gcc, gdc, betac, gcr, batch, seq):
    ng = seq // GDN_GROUP
    wide = GDN_CHAINS * GDN_DK
    colb = pl.BlockSpec((GDN_GROUP, GDN_CHAINS), lambda b, s: (b * ng + s, 0))
    outb = pl.BlockSpec((GDN_GROUP, wide), lambda b, s: (b * ng + s, 0))
    n = batch * seq
    return pl.pallas_call(
        _gdn_local_kernel,
        grid=(batch, ng),
        in_specs=[pl.BlockSpec((GDN_GROUP, GDN_QKV), lambda b, s: (b * ng + s, 0)),
                  colb, colb, colb,
                  pl.BlockSpec((GDN_CHAINS, GDN_GROUP), lambda b, s: (0, b * ng + s))],
        out_specs=[outb, outb, outb, outb,
                   pl.BlockSpec((1, wide, GDN_GROUP), lambda b, s: (b, 0, s))],
        out_shape=[jax.ShapeDtypeStruct((n, wide), F32),
                   jax.ShapeDtypeStruct((n, wide), BF16),
                   jax.ShapeDtypeStruct((n, wide), BF16),
                   jax.ShapeDtypeStruct((n, wide), BF16),
                   jax.ShapeDtypeStruct((batch, wide, seq), BF16)],
        compiler_params=_cparams(2),
        name="gdn_local",
    )(qkvn, gcc, gdc, betac, gcr)


def _gdn_scan_kernel(uf_ref, ub_ref, wf_ref, wb_ref, qgf_ref, qgb_ref, qkf_ref, qkb_ref,
                     kgf_ref, kgb_ref, glf_ref, glb_ref, of_ref, ob_ref, s_scr):
    @pl.when(pl.program_id(1) == 0)
    def _():
        s_scr[...] = jnp.zeros_like(s_scr)

    c = GDN_CHUNK
    nc = GDN_GROUP // c
    zeros = jnp.zeros((c, GDN_DV), BF16)
    dirs = ((uf_ref, wf_ref, qgf_ref, qkf_ref, kgf_ref, glf_ref, of_ref),
            (ub_ref, wb_ref, qgb_ref, qkb_ref, kgb_ref, glb_ref, ob_ref))
    states = [s_scr[ch] for ch in range(GDN_CHAINS)]
    chains = [(d, h) for d in range(2) for h in range(GDN_HEADS)]
    for step in range(nc):
        tops, v2s = [], []
        for d, h in chains:
            u_ref, w_ref, qg_ref = dirs[d][0:3]
            i = step if d == 0 else nc - 1 - step
            rows = slice(i * c, (i + 1) * c)
            lo, hi = h * GDN_DK, (h + 1) * GDN_DK
            sb = states[d * GDN_HEADS + h].astype(BF16)
            top = jnp.dot(jnp.concatenate([w_ref[rows, lo:hi], qg_ref[rows, lo:hi]], axis=0), sb,
                          preferred_element_type=F32)
            vb = (u_ref[rows, lo:hi] - top[0:c]).astype(BF16)
            v2s.append(jnp.concatenate([vb, zeros] if i % 2 == 0 else [zeros, vb], axis=0))
            tops.append(top)
        for n, (d, h) in enumerate(chains):
            qk_ref, kg_ref, gl_ref, o_ref = dirs[d][3:7]
            ch = d * GDN_HEADS + h
            i = step if d == 0 else nc - 1 - step
            rows = slice(i * c, (i + 1) * c)
            pair = slice((i // 2) * 2 * c, (i // 2 + 1) * 2 * c)
            lo, hi = h * GDN_DK, (h + 1) * GDN_DK
            bot = jnp.dot(jnp.concatenate([qk_ref[rows, lo:hi], kg_ref[0, lo:hi, pair]], axis=0), v2s[n],
                          preferred_element_type=F32)
            o_ref[rows, lo:hi] = tops[n][c:2 * c] + bot[0:c]
            decay = jnp.exp(gl_ref[ch:ch + 1, i * c:i * c + 1])
            states[ch] = states[ch] * decay + bot[c:c + GDN_DK]
    for ch in range(GDN_CHAINS):
        s_scr[ch] = states[ch]


def _gdn_scan(u, w, qg, qk, kgt, gl, batch, seq):
    ng = seq // GDN_GROUP
    half = GDN_HEADS * GDN_DK
    fwd = lambda b, s: (b * ng + s, 0)
    bwd = lambda b, s: (b * ng + ng - 1 - s, 1)
    blk = (GDN_GROUP, half)
    n = batch * seq
    pair = lambda arr: (arr, arr)
    out_f = pl.BlockSpec(blk, fwd)
    out_b = pl.BlockSpec(blk, lambda b, s: (b * ng + ng - 1 - s, 0))
    in_specs = []
    for _ in range(4):
        in_specs += [pl.BlockSpec(blk, fwd), pl.BlockSpec(blk, bwd)]
    in_specs += [pl.BlockSpec((1, half, GDN_GROUP), lambda b, s: (b, 0, s)),
                 pl.BlockSpec((1, half, GDN_GROUP), lambda b, s: (b, 1, ng - 1 - s)),
                 pl.BlockSpec((GDN_CHAINS, GDN_GROUP), lambda b, s: (0, b * ng + s)),
                 pl.BlockSpec((GDN_CHAINS, GDN_GROUP), lambda b, s: (0, b * ng + ng - 1 - s))]
    return pl.pallas_call(
        _gdn_scan_kernel,
        grid=(batch, ng),
        in_specs=in_specs,
        out_specs=[out_f, out_b],
        out_shape=[jax.ShapeDtypeStruct((n, half), F32), jax.ShapeDtypeStruct((n, half), F32)],
        scratch_shapes=[pltpu.VMEM((GDN_CHAINS, GDN_DK, GDN_DV), F32)],
        compiler_params=_cparams(2),
        name="gdn_scan",
    )(*pair(u), *pair(w), *pair(qg), *pair(qk), *pair(kgt), *pair(gl))


def _layer_norm(r, g, b):
    mu = jnp.mean(r, axis=-1, keepdims=True)
    var = jnp.mean(jnp.square(r - mu), axis=-1, keepdims=True)
    return (r - mu) * lax.rsqrt(var + LN_EPS) * g + b


def _merge_kernel(ya_ref, of_ref, ob_ref, gate_ref, yc_ref, gz_ref, x_ref, wb_ref, wo_ref,
                  nw_ref, lng_ref, lnb_ref, o_ref):
    segs = []
    for h in range(GDN_HEADS):
        sl = slice(h * GDN_DV, (h + 1) * GDN_DV)
        o = of_ref[:, sl] + ob_ref[:, sl]
        o = o * lax.rsqrt(jnp.mean(o * o, axis=-1, keepdims=True) + RMS_EPS) * nw_ref[...]
        gt = gate_ref[:, sl].astype(F32)
        segs.append(o * (gt * jax.nn.sigmoid(gt)))
    yb = jnp.concatenate(segs, axis=1).astype(BF16)
    merged = None
    for n, y in enumerate((ya_ref[...], yb, yc_ref[...])):
        proj = jnp.dot(y, wb_ref[n], preferred_element_type=F32)
        term = jax.nn.sigmoid(gz_ref[:, n * D_MODEL:(n + 1) * D_MODEL].astype(F32)) * proj
        merged = term if merged is None else merged + term
    hmix = jnp.dot(merged.astype(BF16), wo_ref[...], preferred_element_type=F32)
    o_ref[...] = _layer_norm(DEEPNORM_ALPHA * x_ref[...] + hmix, lng_ref[...], lnb_ref[...])


def _merge(ya, o_f, o_b, zb, yc, x, w_branch, w_out, norm_w, ln_g, ln_b, tm):
    n = x.shape[0]
    row = lambda width, col=0: pl.BlockSpec((tm, width), lambda i, col=col: (i, col))
    full = lambda shape: pl.BlockSpec(shape, lambda i: (0,) * len(shape), pipeline_mode=pl.Buffered(1))
    return pl.pallas_call(
        _merge_kernel,
        grid=(n // tm,),
        in_specs=[row(BRANCH_WIDTH), row(BRANCH_WIDTH), row(BRANCH_WIDTH),
                  row(BRANCH_WIDTH, ZB_GATE // BRANCH_WIDTH), row(BRANCH_WIDTH),
                  row(N_BRANCHES * D_MODEL, ZB_GZ // (N_BRANCHES * D_MODEL)), row(D_MODEL),
                  full((N_BRANCHES, BRANCH_WIDTH, D_MODEL)), full((D_MODEL, D_MODEL)),
                  full((1, GDN_DV)), full((1, D_MODEL)), full((1, D_MODEL))],
        out_specs=row(D_MODEL),
        out_shape=jax.ShapeDtypeStruct((n, D_MODEL), F32),
        compiler_params=_cparams(1),
        name="merge_ln",
    )(ya, o_f, o_b, zb, yc, zb, x, w_branch, w_out, norm_w.astype(F32).reshape(1, GDN_DV),
      ln_g.astype(F32).reshape(1, D_MODEL), ln_b.astype(F32).reshape(1, D_MODEL))


FFN_SPLIT = 2


def _ffn_kernel(xp_ref, x_ref, xn_ref, wup_ref, cw_ref, cb_ref, wdn_ref, lng_ref, lnb_ref, o_ref, ob_ref, *,
                tiles_per_seq):
    i = pl.program_id(0)
    tm = x_ref.shape[0]
    halo = V7X_SUBLANES
    x = x_ref[...]
    xe = jnp.concatenate([xp_ref[...], x, xn_ref[...]], axis=0).astype(BF16)
    xb = x.astype(BF16)
    rows = lax.broadcasted_iota(jnp.int32, (tm + 2 * halo, 1), 0)
    first = (i % tiles_per_seq) == 0
    last = (i % tiles_per_seq) == tiles_per_seq - 1
    keep = jnp.logical_not(((rows < halo) & first) | ((rows >= tm + halo) & last))
    fc = D_FF // FFN_SPLIT
    ext = tm + 2 * halo
    acc = None
    for c in range(FFN_SPLIT):
        ge = jnp.dot(xe, wup_ref[:, c * fc:(c + 1) * fc], preferred_element_type=F32)
        ge = jnp.where(keep, ge, 0.0)
        up = jnp.dot(xb, wup_ref[:, D_FF + c * fc:D_FF + (c + 1) * fc], preferred_element_type=F32)
        cw = cw_ref[:, c * fc:(c + 1) * fc]
        conv = (pltpu.roll(ge, 1, 0) * cw[0:1] + ge * cw[1:2] + pltpu.roll(ge, ext - 1, 0) * cw[2:3])
        gate = conv[halo:halo + tm] + cb_ref[:, c * fc:(c + 1) * fc]
        act = (gate * jax.nn.sigmoid(gate) * up).astype(BF16)
        part = jnp.dot(act, wdn_ref[c * fc:(c + 1) * fc, :], preferred_element_type=F32)
        acc = part if acc is None else acc + part
    y = _layer_norm(DEEPNORM_ALPHA * x + acc, lng_ref[...], lnb_ref[...])
    o_ref[...] = y
    ob_ref[...] = y.astype(ob_ref.dtype)


def _ffn(x, w_up, conv_w, conv_b, w_down, ln_g, ln_b, seq, tm):
    n = x.shape[0]
    halo = V7X_SUBLANES
    per = tm // halo
    nh = n // halo
    full = lambda shape: pl.BlockSpec(shape, lambda i: (0,) * len(shape), pipeline_mode=pl.Buffered(1))
    kern = functools.partial(_ffn_kernel, tiles_per_seq=seq // tm)
    return pl.pallas_call(
        kern,
        grid=(n // tm,),
        in_specs=[pl.BlockSpec((halo, D_MODEL), lambda i: (jnp.maximum(i * per - 1, 0), 0)),
                  pl.BlockSpec((tm, D_MODEL), lambda i: (i, 0)),
                  pl.BlockSpec((halo, D_MODEL), lambda i: (jnp.minimum((i + 1) * per, nh - 1), 0)),
                  full((D_MODEL, 2 * D_FF)), full((FFN_CONV, D_FF)), full((1, D_FF)),
                  full((D_FF, D_MODEL)), full((1, D_MODEL)), full((1, D_MODEL))],
        out_specs=[pl.BlockSpec((tm, D_MODEL), lambda i: (i, 0)), pl.BlockSpec((tm, D_MODEL), lambda i: (i, 0))],
        out_shape=[jax.ShapeDtypeStruct((n, D_MODEL), F32), jax.ShapeDtypeStruct((n, D_MODEL), BF16)],
        compiler_params=_cparams(1),
        name="ffn_ln",
    )(x, x, x, w_up, conv_w.astype(F32), conv_b.astype(F32).reshape(1, D_FF), w_down,
      ln_g.astype(F32).reshape(1, D_MODEL), ln_b.astype(F32).reshape(1, D_MODEL))


def _pack_w_in(w):
    widths = (512, 512, 512, GDN_QKV, 512, 8, 8, 512, 128, 128, N_BRANCHES * D_MODEL)
    offs = [0]
    for wd in widths:
        offs.append(offs[-1] + wd)
    seg = lambda i: w[:, offs[i]:offs[i + 1]]
    aq, ak, av, bqkv, bgate, ba, bb, cq, ck, cv, gz = (seg(i) for i in range(len(widths)))
    dup = lambda t: jnp.concatenate([t[:, :HEAD_DIM], t[:, :HEAD_DIM], t[:, HEAD_DIM:], t[:, HEAD_DIM:]], axis=1)
    w_a = jnp.concatenate([aq * DIFF_QSCALE, ak, av, cq, dup(ck), dup(cv)], axis=1)
    w_b = jnp.concatenate([gz, bqkv, bgate], axis=1)
    pad = jnp.zeros((w.shape[0], ZC_COLS - 2 * GDN_CHAINS), w.dtype)
    w_c = jnp.concatenate([ba, bb, pad], axis=1)
    return w_a.astype(BF16), w_b.astype(BF16), w_c.astype(BF16)


def kernel(x, rel_bias, w_in, diff_lambda, diff_subln, gdn_conv, gdn_a_log, gdn_dt_bias, gdn_norm, swa_sink,
           w_branch, w_out, ln1_g, ln1_b, ffn_up, ffn_conv, ffn_conv_b, ffn_down, ln2_g, ln2_b):
    batch, seq, d = x.shape
    assert d == D_MODEL and seq % GDN_GROUP == 0 and seq % (DIFF_QROWS * DIFF_UNROLL) == 0
    n = batch * seq
    tm = 256
    tb_diff = _diff_bias_tiles(rel_bias, seq) * LOG2_E
    tb_swa = _swa_bias_tiles(rel_bias)
    xf = x.reshape(n, d)
    xb = xf.astype(BF16)
    for l in range(DEPTH):
        w_a, w_b, w_c = _pack_w_in(w_in[l])
        za = _matmul(xb, w_a, BF16, 1024, ZA_COLS // 2, "in_proj_a")
        zb = _matmul(xb, w_b, BF16, 1024, ZB_COLS // 2, "in_proj_b")
        zc = _matmul(xb, w_c, F32, 1024, ZC_COLS, "in_proj_c")
        ya = _diff_attention(za, tb_diff, diff_lambda[l], diff_subln[l], batch, seq, l)
        yc = _swa_attention(za, tb_swa, swa_sink[l], batch, seq)
        qkvn = _gdn_conv(zb, gdn_conv[l], batch, seq)
        ab_t = zc[:, 0:2 * GDN_CHAINS].T
        gc, gd, gl, beta = _gdn_gates(ab_t, gdn_a_log[l], gdn_dt_bias[l], batch, seq)
        u, w, qg, qk, kgt = _gdn_local(qkvn, gc.T, gd.T, beta.T, gc, batch, seq)
        o_f, o_b = _gdn_scan(u, w, qg, qk, kgt, gl, batch, seq)
        xf = _merge(ya, o_f, o_b, zb, yc, xf, w_branch[l].astype(BF16), w_out[l].astype(BF16),
                    gdn_norm[l], ln1_g[l], ln1_b[l], 2 * tm)
        xf, xb = _ffn(xf, ffn_up[l].astype(BF16), ffn_conv[l], ffn_conv_b[l], ffn_down[l].astype(BF16),
                      ln2_g[l], ln2_b[l], seq, 2 * tm)
    return xf.reshape(batch, seq, d)
```

```python
import functools
import math

import jax
import jax.numpy as jnp
from jax import lax
from jax.experimental import pallas as pl
from jax.experimental.pallas import tpu as pltpu

F32 = jnp.float32
BF16 = jnp.bfloat16

D_MODEL = 1024
DEPTH = 2
HEAD_DIM = 64
DIFF_HEADS = 4
DIFF_V = 2 * HEAD_DIM
Q_BLOCK = 128
GDN_HEADS = 4
GDN_DK = 128
GDN_DV = 128
GDN_QKV = GDN_HEADS * (2 * GDN_DK + GDN_DV)
GDN_CONV = 5
GDN_CHUNK = 64
SWA_HEADS = 8
SWA_KV_HEADS = 2
SWA_WINDOW = 128
SWA_BLOCK = 128
BRANCH_WIDTH = 512
N_BRANCHES = 3
REL_BUCKETS = 32
REL_MAX_DIST = 128
D_FF = 2816
FFN_CONV = 3
DEEPNORM_ALPHA = (2 * DEPTH) ** 0.25
LN_EPS = 1e-5
RMS_EPS = 1e-6

V7X_LANES = 128
V7X_SUBLANES = 8
V7X_VMEM_BYTES = 64 * 1024 * 1024
VMEM_LIMIT = V7X_VMEM_BYTES * 7 // 8

NEG_BIG = -1e30

ZA_AQ, ZA_AK, ZA_AV = 0, 512, 1024
ZA_CQ = 1536
ZA_CK = 2048
ZA_CV = 2304
ZA_COLS = 2560
ZB_GZ = 0
ZB_QKV = 3072
ZB_GATE = 4608
ZB_COLS = 5120
ZC_COLS = 128

GDN_GROUP = 256
GDN_CHAINS = 2 * GDN_HEADS


def _cparams(n_grid):
    return pltpu.CompilerParams(dimension_semantics=("arbitrary",) * n_grid, vmem_limit_bytes=VMEM_LIMIT)


def _t5_bucket(rel):
    nb = REL_BUCKETS // 2
    ret = jnp.where(rel > 0, nb, 0)
    n = jnp.abs(rel)
    max_exact = nb // 2
    large = max_exact + (jnp.log(jnp.maximum(n, 1).astype(jnp.float32) / max_exact)
                         / math.log(REL_MAX_DIST / max_exact) * (nb - max_exact)).astype(jnp.int32)
    large = jnp.minimum(large, nb - 1)
    return ret + jnp.where(n < max_exact, n, large)


def _table_lookup(table, bucket):
    out = jnp.zeros((table.shape[1],) + bucket.shape, F32)
    for i in range(REL_BUCKETS):
        out = jnp.where(bucket[None] == i, table[i].astype(F32).reshape((-1,) + (1,) * bucket.ndim), out)
    return out


def _diff_bias_tiles(rel_bias, seq):
    nq = seq // Q_BLOCK
    d = jnp.arange(2 * nq - 1) - (nq - 1)
    r = jnp.arange(Q_BLOCK)
    rel = d[:, None, None] * Q_BLOCK + r[None, None, :] - r[None, :, None]
    return _table_lookup(rel_bias[:, :DIFF_HEADS], _t5_bucket(rel))


def _swa_bias_tiles(rel_bias):
    kb = 3 * SWA_BLOCK
    rel = jnp.arange(kb)[None, :] - SWA_WINDOW - jnp.arange(SWA_BLOCK)[:, None]
    bias = _table_lookup(rel_bias[:, DIFF_HEADS:], _t5_bucket(rel))
    return jnp.where((jnp.abs(rel) <= SWA_WINDOW)[None], bias, NEG_BIG)


def _matmul_kernel(x_ref, w_ref, o_ref):
    o_ref[...] = jnp.dot(x_ref[...].astype(BF16), w_ref[...],
                         preferred_element_type=F32).astype(o_ref.dtype)


def _matmul(x, w, out_dtype, tm, tn, name):
    m, k = x.shape
    n = w.shape[1]
    return pl.pallas_call(
        _matmul_kernel,
        grid=(n // tn, m // tm),
        in_specs=[pl.BlockSpec((tm, k), lambda j, i: (i, 0)),
                  pl.BlockSpec((k, tn), lambda j, i: (0, j))],
        out_specs=pl.BlockSpec((tm, tn), lambda j, i: (i, j)),
        out_shape=jax.ShapeDtypeStruct((m, n), out_dtype),
        compiler_params=_cparams(2),
        name=name,
    )(x, w)


LOG2_E = math.log2(math.e)
DIFF_QSCALE = HEAD_DIM ** -0.5 * LOG2_E
DIFF_QROWS = 2 * Q_BLOCK
DIFF_UNROLL = 4
DIFF_ONES = 16


def _diff_attn_kernel(q_ref, k_ref, v_ref, tb_ref, lam_ref, subln_ref, o_ref, s_scr, vx_scr, *, nq, lam_init):
    qr = DIFF_QROWS
    seq = v_ref.shape[0]
    vx_scr[0:DIFF_V, :] = v_ref[...].astype(F32).T.astype(vx_scr.dtype)
    vx_scr[DIFF_V:DIFF_V + DIFF_ONES, :] = jnp.ones((DIFF_ONES, seq), vx_scr.dtype)
    lv = lam_ref[...]
    lam = (jnp.exp(jnp.sum(lv[0:1] * lv[1:2], axis=-1, keepdims=True))
           - jnp.exp(jnp.sum(lv[2:3] * lv[3:4], axis=-1, keepdims=True)) + lam_init)
    lane = lax.broadcasted_iota(jnp.int32, (qr, 128), 1)
    zero = jnp.zeros((qr, 128), q_ref.dtype)

    def scores(qb, scr):
        r0 = pl.multiple_of(qb * qr, qr)
        q = q_ref[pl.ds(r0, qr), :]
        qz = jnp.concatenate([jnp.where(lane < HEAD_DIM, q, zero),
                              jnp.where(lane >= HEAD_DIM, q, zero)], axis=0)
        for kp in range(nq // 2):
            kblk = k_ref[kp * qr:(kp + 1) * qr, :]
            st = lax.dot_general(kblk, qz, (((1,), (1,)), ((), ())), preferred_element_type=F32)
            for r in range(2):
                d0 = nq - 1 - 2 * qb + 2 * kp + r
                bias = jnp.concatenate([tb_ref[0, d0], tb_ref[0, d0 - 1]], axis=1)
                rows = slice(r * Q_BLOCK, (r + 1) * Q_BLOCK)
                for c in range(2):
                    scr[kp * qr + r * Q_BLOCK:kp * qr + (r + 1) * Q_BLOCK, c * qr:(c + 1) * qr] = (
                        st[rows, c * qr:(c + 1) * qr] + bias)

    def softmax(scr):
        s = scr[...]
        return jnp.exp2(s - jnp.max(s, axis=0, keepdims=True)).astype(BF16)

    def finish(qb, e):
        r0 = pl.multiple_of(qb * qr, qr)
        pvt = jnp.dot(vx_scr[...], e, preferred_element_type=F32)
        ot = pvt[0:DIFF_V] / pvt[DIFF_V:DIFF_V + 1]
        dt = ot[:, 0:qr] - lam * ot[:, qr:2 * qr]
        dt = dt * lax.rsqrt(jnp.mean(dt * dt, axis=0, keepdims=True) + RMS_EPS)
        o_ref[pl.ds(r0, qr), :] = (dt.T * subln_ref[...] * (1.0 - lam_init)).astype(o_ref.dtype)

    def body(i, carry):
        blocks = [i * DIFF_UNROLL + j for j in range(DIFF_UNROLL)]
        for j, qb in enumerate(blocks):
            scores(qb, s_scr.at[j])
        probs = [softmax(s_scr.at[j]) for j in range(DIFF_UNROLL)]
        for qb, e in zip(blocks, probs):
            finish(qb, e)
        return carry

    lax.fori_loop(0, nq // (2 * DIFF_UNROLL), body, 0)


def _diff_attention(za, tb, lam_vecs, subln, batch, seq, layer_idx):
    nq = seq // Q_BLOCK
    lam_init = 0.8 - 0.6 * math.exp(-0.3 * layer_idx)
    kern = functools.partial(_diff_attn_kernel, nq=nq, lam_init=lam_init)
    qcol, kcol, vcol = ZA_AQ // 128, ZA_AK // 128, ZA_AV // 128
    return pl.pallas_call(
        kern,
        grid=(batch, DIFF_HEADS),
        in_specs=[
            pl.BlockSpec((seq, 128), lambda b, h: (b, qcol + h)),
            pl.BlockSpec((seq, 128), lambda b, h: (b, kcol + h)),
            pl.BlockSpec((seq, 128), lambda b, h: (b, vcol + h)),
            pl.BlockSpec((1, 2 * nq - 1, Q_BLOCK, Q_BLOCK), lambda b, h: (h, 0, 0, 0)),
            pl.BlockSpec((4, HEAD_DIM), lambda b, h: (0, 0)),
            pl.BlockSpec((1, DIFF_V), lambda b, h: (0, 0)),
        ],
        out_specs=pl.BlockSpec((seq, 128), lambda b, h: (b, h)),
        out_shape=jax.ShapeDtypeStruct((batch * seq, BRANCH_WIDTH), BF16),
        scratch_shapes=[pltpu.VMEM((DIFF_UNROLL, seq, 2 * DIFF_QROWS), F32),
                        pltpu.VMEM((DIFF_V + DIFF_ONES, seq), BF16)],
        compiler_params=_cparams(2),
        name="diff_attention",
    )(za, za, za, tb, lam_vecs.astype(F32), subln.astype(F32).reshape(1, DIFF_V))


def _swa_kernel(q_ref, k_ref, v_ref, bias_ref, sink_ref, o_ref, kp_scr, vp_scr, *, nb):
    seq = q_ref.shape[0]
    blk = SWA_BLOCK
    pad = jnp.zeros((blk, kp_scr.shape[1]), kp_scr.dtype)
    for scr, ref in ((kp_scr, k_ref), (vp_scr, v_ref)):
        scr[0:blk] = pad
        scr[blk + seq:2 * blk + seq] = pad
        scr[blk:blk + seq] = ref[...]
    krow = lax.broadcasted_iota(jnp.int32, (3 * blk, blk), 0)
    lane = lax.broadcasted_iota(jnp.int32, (blk, 128), 1)
    low = lane < HEAD_DIM
    zero = jnp.zeros((blk, 128), q_ref.dtype)
    scale = HEAD_DIM ** -0.5
    rep = SWA_HEADS // SWA_KV_HEADS

    def body(n, carry):
        r0 = pl.multiple_of(n * blk, blk)
        q = q_ref[pl.ds(r0, blk), :]
        kw = kp_scr[pl.ds(r0, 3 * blk), :]
        vw = vp_scr[pl.ds(r0, 3 * blk), :]
        outside = ((krow < blk) & (n == 0)) | ((krow >= 2 * blk) & (n == nb - 1))
        heads = range(SWA_HEADS)
        qms = [jnp.where(low if hd % 2 == 0 else jnp.logical_not(low),
                         q[:, (hd // 2) * 128:(hd // 2 + 1) * 128] * scale, zero) for hd in heads]
        ss = [lax.dot_general(kw[:, (hd // rep) * 128:(hd // rep + 1) * 128], qms[hd], (((1,), (1,)), ((), ())),
                              preferred_element_type=F32) for hd in heads]
        ss = [jnp.where(outside, NEG_BIG, ss[hd] + bias_ref[hd]) for hd in heads]
        sinks = [sink_ref[hd][:, 0:1] for hd in heads]
        ms = [jnp.maximum(jnp.max(ss[hd], axis=0, keepdims=True), sinks[hd]) for hd in heads]
        es = [jnp.exp(ss[hd] - ms[hd]) for hd in heads]
        dens = [jnp.sum(es[hd], axis=0, keepdims=True) + jnp.exp(sinks[hd] - ms[hd]) for hd in heads]
        ps = [(es[hd] / dens[hd]).astype(BF16) for hd in heads]
        halves = [lax.dot_general(ps[hd], vw[:, (hd // rep) * 128:(hd // rep + 1) * 128], (((0,), (0,)), ((), ())),
                                  preferred_element_type=F32) for hd in heads]
        o = jnp.concatenate([jnp.where(low, halves[2 * c], halves[2 * c + 1]) for c in range(SWA_HEADS // 2)], axis=1)
        o_ref[pl.ds(r0, blk), :] = o.astype(o_ref.dtype)
        return carry

    lax.fori_loop(0, nb, body, 0)


def _swa_attention(za, bias, sink, batch, seq):
    nb = seq // SWA_BLOCK
    kern = functools.partial(_swa_kernel, nb=nb)
    sink_b = jnp.broadcast_to(sink.astype(F32).reshape(SWA_HEADS, 1, 1), (SWA_HEADS, 1, 128))
    kvw = 2 * SWA_KV_HEADS * HEAD_DIM
    return pl.pallas_call(
        kern,
        grid=(batch,),
        in_specs=[
            pl.BlockSpec((seq, BRANCH_WIDTH), lambda b: (b, ZA_CQ // BRANCH_WIDTH)),
            pl.BlockSpec((seq, kvw), lambda b: (b, ZA_CK // kvw)),
            pl.BlockSpec((seq, kvw), lambda b: (b, ZA_CV // kvw)),
            pl.BlockSpec((SWA_HEADS, 3 * SWA_BLOCK, SWA_BLOCK), lambda b: (0, 0, 0)),
            pl.BlockSpec((SWA_HEADS, 1, 128), lambda b: (0, 0, 0)),
        ],
        out_specs=pl.BlockSpec((seq, BRANCH_WIDTH), lambda b: (b, 0)),
        out_shape=jax.ShapeDtypeStruct((batch * seq, BRANCH_WIDTH), BF16),
        scratch_shapes=[pltpu.VMEM((seq + 2 * SWA_BLOCK, kvw), BF16),
                        pltpu.VMEM((seq + 2 * SWA_BLOCK, kvw), BF16)],
        compiler_params=_cparams(1),
        name="swa_attention",
    )(za, za, za, bias.transpose(0, 2, 1), sink_b)


def _shift_rows(x, d):
    rows = x.shape[0]
    t = lax.broadcasted_iota(jnp.int32, x.shape, 0)
    rolled = pltpu.roll(x, (-d) % rows, 0)
    return jnp.where((t + d >= 0) & (t + d < rows), rolled, 0.0)


def _gdn_conv_kernel(x_ref, w_ref, o_ref):
    j = pl.program_id(1)
    x = x_ref[...].astype(F32)
    w = w_ref[...]
    half = GDN_CONV // 2
    acc = x * w[half:half + 1]
    for tap in range(GDN_CONV):
        if tap != half:
            acc = acc + _shift_rows(x, tap - half) * w[tap:tap + 1]
    h = acc * jax.nn.sigmoid(acc)
    inv = lax.rsqrt(jnp.sum(h * h, axis=-1, keepdims=True) + RMS_EPS)
    factor = jnp.where(j < 2 * GDN_HEADS, inv, 1.0) * jnp.where(j < GDN_HEADS, GDN_DK ** -0.5, 1.0)
    o_ref[...] = h * factor


def _gdn_conv(zb, conv_w, batch, seq):
    c0 = ZB_QKV // 128
    return pl.pallas_call(
        _gdn_conv_kernel,
        grid=(batch, GDN_QKV // 128),
        in_specs=[pl.BlockSpec((seq, 128), lambda b, j: (b, c0 + j)),
                  pl.BlockSpec((GDN_CONV, 128), lambda b, j: (0, j))],
        out_specs=pl.BlockSpec((seq, 128), lambda b, j: (b, j)),
        out_shape=jax.ShapeDtypeStruct((batch * seq, GDN_QKV), F32),
        compiler_params=_cparams(2),
        name="gdn_conv",
    )(zb, conv_w.astype(F32))


def _gdn_gate_kernel(ab_ref, alog_ref, dtb_ref, gc_ref, gd_ref, gl_ref, beta_ref):
    ab = ab_ref[...]
    a = ab[0:GDN_CHAINS]
    seq = a.shape[1]
    xs = a + dtb_ref[...]
    softplus = jnp.maximum(xs, 0.0) + jnp.log1p(jnp.exp(-jnp.abs(xs)))
    g = -jnp.exp(alog_ref[...]) * softplus
    beta_ref[...] = jax.nn.sigmoid(ab[GDN_CHAINS:2 * GDN_CHAINS])
    pos = lax.broadcasted_iota(jnp.int32, g.shape, 1) % GDN_CHUNK
    cf = g
    cr = g
    s = 1
    while s < GDN_CHUNK:
        cf = cf + jnp.where(pos >= s, pltpu.roll(cf, s, 1), 0.0)
        cr = cr + jnp.where(pos < GDN_CHUNK - s, pltpu.roll(cr, seq - s, 1), 0.0)
        s *= 2
    fwd = lax.broadcasted_iota(jnp.int32, g.shape, 0) < GDN_HEADS
    gc_ref[...] = jnp.where(fwd, cf, cr)
    gd_ref[...] = jnp.where(fwd, cr, cf) - g
    gl_ref[...] = cf + cr - g


def _gdn_gates(ab_t, a_log, dt_bias, batch, seq):
    row = pl.BlockSpec((GDN_CHAINS, seq), lambda b: (0, b))
    shp = jax.ShapeDtypeStruct((GDN_CHAINS, batch * seq), F32)
    par = pl.BlockSpec((GDN_CHAINS, 1), lambda b: (0, 0))
    return pl.pallas_call(
        _gdn_gate_kernel,
        grid=(batch,),
        in_specs=[pl.BlockSpec((2 * GDN_CHAINS, seq), lambda b: (0, b)), par, par],
        out_specs=[row, row, row, row],
        out_shape=[shp, shp, shp, shp],
        compiler_params=_cparams(1),
        name="gdn_gates",
    )(ab_t, a_log.astype(F32).reshape(GDN_CHAINS, 1), dt_bias.astype(F32).reshape(GDN_CHAINS, 1))


def _hdot(a, b):
    return jnp.dot(a.astype(BF16), b.astype(BF16), preferred_element_type=F32)


def _gdn_local_kernel(qkv_ref, gcc_ref, gdc_ref, betac_ref, gcr_ref,
                      u_ref, w_ref, qg_ref, qk_ref, kgt_ref):
    g = GDN_GROUP
    c = GDN_CHUNK
    ri = lax.broadcasted_iota(jnp.int32, (g, g), 0)
    ci = lax.broadcasted_iota(jnp.int32, (g, g), 1)
    same = (ri // c) == (ci // c)
    ahead = jnp.where(same, ri - ci, -g)
    behind = jnp.where(same, ci - ri, -g)
    eye_s = (lax.broadcasted_iota(jnp.int32, (c, g), 0)
             == lax.broadcasted_iota(jnp.int32, (c, g), 1) % c).astype(F32)

    def block_diag(xs):
        return jnp.where(same, jnp.concatenate([xs] * (g // c), axis=0), 0.0)

    def row_blocks_sum(xd):
        out = xd[0:c]
        for i in range(1, g // c):
            out = out + xd[i * c:(i + 1) * c]
        return out

    chains = [(h, d) for h in range(GDN_HEADS) for d in range(2)]
    xs_l, xd_l, p_l, rhs_l = [], [], [], []
    for h in range(GDN_HEADS):
        q = qkv_ref[:, h * GDN_DK:(h + 1) * GDN_DK]
        k = qkv_ref[:, (GDN_HEADS + h) * GDN_DK:(GDN_HEADS + h + 1) * GDN_DK]
        v = qkv_ref[:, (2 * GDN_HEADS + h) * GDN_DK:(2 * GDN_HEADS + h + 1) * GDN_DK]
        kbf = k.astype(BF16)
        qk_raw = lax.dot_general(q.astype(BF16), kbf, (((1,), (1,)), ((), ())), preferred_element_type=F32)
        for d in range(2):
            ch = d * GDN_HEADS + h
            lo, hi = ch * GDN_DK, (ch + 1) * GDN_DK
            gcc = gcc_ref[:, ch:ch + 1]
            gdc = gdc_ref[:, ch:ch + 1]
            beta = betac_ref[:, ch:ch + 1]
            gcr = gcr_ref[ch:ch + 1, :]
            order = ahead if d == 0 else behind
            kb = k * beta
            kk = lax.dot_general(kb.astype(BF16), kbf, (((1,), (1,)), ((), ())), preferred_element_type=F32)
            decay = jnp.exp(jnp.where(order >= 0, gcc - gcr, NEG_BIG))
            xd = -jnp.where(order > 0, kk * decay, 0.0)
            qkm = qk_raw * decay
            egc = jnp.exp(gcc)
            qg_ref[:, lo:hi] = (q * egc).astype(qg_ref.dtype)
            qk_ref[:, lo:hi] = (qkm[:, 0:128] + qkm[:, 128:256]).astype(qk_ref.dtype)
            kgt_ref[0, lo:hi, :] = (k * jnp.exp(gdc)).T.astype(kgt_ref.dtype)
            xs = row_blocks_sum(xd)
            xs_l.append(xs)
            xd_l.append(xd)
            p_l.append(eye_s + xs)
            rhs_l.append(jnp.concatenate([v * beta, kb * egc], axis=1).astype(BF16))
    n_ch = len(chains)
    xs_l = [_hdot(xs_l[i], xd_l[i]) for i in range(n_ch)]
    for _ in range(4):
        r_l = [_hdot(jnp.concatenate([p_l[i], xs_l[i]], axis=0), block_diag(xs_l[i])) for i in range(n_ch)]
        p_l = [p_l[i] + r_l[i][0:c] for i in range(n_ch)]
        xs_l = [r_l[i][c:2 * c] for i in range(n_ch)]
    p_l = [p_l[i] + _hdot(p_l[i], block_diag(xs_l[i])) for i in range(n_ch)]
    sol_l = [_hdot(block_diag(p_l[i]), rhs_l[i]) for i in range(n_ch)]
    for i, (h, d) in enumerate(chains):
        ch = d * GDN_HEADS + h
        lo, hi = ch * GDN_DK, (ch + 1) * GDN_DK
        u_ref[:, lo:hi] = sol_l[i][:, 0:GDN_DV]
        w_ref[:, lo:hi] = sol_l[i][:, GDN_DV:].astype(w_ref.dtype)


def _gdn_local(qkvn, gcc, gdc, betac, gcr, batch, seq):
    ng = seq // GDN_GROUP
    wide = GDN_CHAINS * GDN_DK
    colb = pl.BlockSpec((GDN_GROUP, GDN_CHAINS), lambda b, s: (b * ng + s, 0))
    outb = pl.BlockSpec((GDN_GROUP, wide), lambda b, s: (b * ng + s, 0))
    n = batch * seq
    return pl.pallas_call(
        _gdn_local_kernel,
        grid=(batch, ng),
        in_specs=[pl.BlockSpec((GDN_GROUP, GDN_QKV), lambda b, s: (b * ng + s, 0)),
                  colb, colb, colb,
                  pl.BlockSpec((GDN_CHAINS, GDN_GROUP), lambda b, s: (0, b * ng + s))],
        out_specs=[outb, outb, outb, outb,
                   pl.BlockSpec((1, wide, GDN_GROUP), lambda b, s: (b, 0, s))],
        out_shape=[jax.ShapeDtypeStruct((n, wide), F32),
                   jax.ShapeDtypeStruct((n, wide), BF16),
                   jax.ShapeDtypeStruct((n, wide), BF16),
                   jax.ShapeDtypeStruct((n, wide), BF16),
                   jax.ShapeDtypeStruct((batch, wide, seq), BF16)],
        compiler_params=_cparams(2),
        name="gdn_local",
    )(qkvn, gcc, gdc, betac, gcr)


def _gdn_scan_kernel(uf_ref, ub_ref, wf_ref, wb_ref, qgf_ref, qgb_ref, qkf_ref, qkb_ref,
                     kgf_ref, kgb_ref, glf_ref, glb_ref, of_ref, ob_ref, s_scr):
    @pl.when(pl.program_id(1) == 0)
    def _():
        s_scr[...] = jnp.zeros_like(s_scr)

    c = GDN_CHUNK
    nc = GDN_GROUP // c
    zeros = jnp.zeros((c, GDN_DV), BF16)
    dirs = ((uf_ref, wf_ref, qgf_ref, qkf_ref, kgf_ref, glf_ref, of_ref),
            (ub_ref, wb_ref, qgb_ref, qkb_ref, kgb_ref, glb_ref, ob_ref))
    states = [s_scr[ch] for ch in range(GDN_CHAINS)]
    chains = [(d, h) for d in range(2) for h in range(GDN_HEADS)]
    for step in range(nc):
        tops, v2s = [], []
        for d, h in chains:
            u_ref, w_ref, qg_ref = dirs[d][0:3]
            i = step if d == 0 else nc - 1 - step
            rows = slice(i * c, (i + 1) * c)
            lo, hi = h * GDN_DK, (h + 1) * GDN_DK
            sb = states[d * GDN_HEADS + h].astype(BF16)
            top = jnp.dot(jnp.concatenate([w_ref[rows, lo:hi], qg_ref[rows, lo:hi]], axis=0), sb,
                          preferred_element_type=F32)
            vb = (u_ref[rows, lo:hi] - top[0:c]).astype(BF16)
            v2s.append(jnp.concatenate([vb, zeros] if i % 2 == 0 else [zeros, vb], axis=0))
            tops.append(top)
        for n, (d, h) in enumerate(chains):
            qk_ref, kg_ref, gl_ref, o_ref = dirs[d][3:7]
            ch = d * GDN_HEADS + h
            i = step if d == 0 else nc - 1 - step
            rows = slice(i * c, (i + 1) * c)
            pair = slice((i // 2) * 2 * c, (i // 2 + 1) * 2 * c)
            lo, hi = h * GDN_DK, (h + 1) * GDN_DK
            bot = jnp.dot(jnp.concatenate([qk_ref[rows, lo:hi], kg_ref[0, lo:hi, pair]], axis=0), v2s[n],
                          preferred_element_type=F32)
            o_ref[rows, lo:hi] = tops[n][c:2 * c] + bot[0:c]
            decay = jnp.exp(gl_ref[ch:ch + 1, i * c:i * c + 1])
            states[ch] = states[ch] * decay + bot[c:c + GDN_DK]
    for ch in range(GDN_CHAINS):
        s_scr[ch] = states[ch]


def _gdn_scan(u, w, qg, qk, kgt, gl, batch, seq):
    ng = seq // GDN_GROUP
    half = GDN_HEADS * GDN_DK
    fwd = lambda b, s: (b * ng + s, 0)
    bwd = lambda b, s: (b * ng + ng - 1 - s, 1)
    blk = (GDN_GROUP, half)
    n = batch * seq
    pair = lambda arr: (arr, arr)
    out_f = pl.BlockSpec(blk, fwd)
    out_b = pl.BlockSpec(blk, lambda b, s: (b * ng + ng - 1 - s, 0))
    in_specs = []
    for _ in range(4):
        in_specs += [pl.BlockSpec(blk, fwd), pl.BlockSpec(blk, bwd)]
    in_specs += [pl.BlockSpec((1, half, GDN_GROUP), lambda b, s: (b, 0, s)),
                 pl.BlockSpec((1, half, GDN_GROUP), lambda b, s: (b, 1, ng - 1 - s)),
                 pl.BlockSpec((GDN_CHAINS, GDN_GROUP), lambda b, s: (0, b * ng + s)),
                 pl.BlockSpec((GDN_CHAINS, GDN_GROUP), lambda b, s: (0, b * ng + ng - 1 - s))]
    return pl.pallas_call(
        _gdn_scan_kernel,
        grid=(batch, ng),
        in_specs=in_specs,
        out_specs=[out_f, out_b],
        out_shape=[jax.ShapeDtypeStruct((n, half), F32), jax.ShapeDtypeStruct((n, half), F32)],
        scratch_shapes=[pltpu.VMEM((GDN_CHAINS, GDN_DK, GDN_DV), F32)],
        compiler_params=_cparams(2),
        name="gdn_scan",
    )(*pair(u), *pair(w), *pair(qg), *pair(qk), *pair(kgt), *pair(gl))


def _layer_norm(r, g, b):
    mu = jnp.mean(r, axis=-1, keepdims=True)
    var = jnp.mean(jnp.square(r - mu), axis=-1, keepdims=True)
    return (r - mu) * lax.rsqrt(var + LN_EPS) * g + b


def _merge_kernel(ya_ref, of_ref, ob_ref, gate_ref, yc_ref, gz_ref, x_ref, wb_ref, wo_ref,
                  nw_ref, lng_ref, lnb_ref, o_ref):
    segs = []
    for h in range(GDN_HEADS):
        sl = slice(h * GDN_DV, (h + 1) * GDN_DV)
        o = of_ref[:, sl] + ob_ref[:, sl]
        o = o * lax.rsqrt(jnp.mean(o * o, axis=-1, keepdims=True) + RMS_EPS) * nw_ref[...]
        gt = gate_ref[:, sl].astype(F32)
        segs.append(o * (gt * jax.nn.sigmoid(gt)))
    yb = jnp.concatenate(segs, axis=1).astype(BF16)
    merged = None
    for n, y in enumerate((ya_ref[...], yb, yc_ref[...])):
        proj = jnp.dot(y, wb_ref[n], preferred_element_type=F32)
        term = jax.nn.sigmoid(gz_ref[:, n * D_MODEL:(n + 1) * D_MODEL].astype(F32)) * proj
        merged = term if merged is None else merged + term
    hmix = jnp.dot(merged.astype(BF16), wo_ref[...], preferred_element_type=F32)
    o_ref[...] = _layer_norm(DEEPNORM_ALPHA * x_ref[...] + hmix, lng_ref[...], lnb_ref[...])


def _merge(ya, o_f, o_b, zb, yc, x, w_branch, w_out, norm_w, ln_g, ln_b, tm):
    n = x.shape[0]
    row = lambda width, col=0: pl.BlockSpec((tm, width), lambda i, col=col: (i, col))
    full = lambda shape: pl.BlockSpec(shape, lambda i: (0,) * len(shape), pipeline_mode=pl.Buffered(1))
    return pl.pallas_call(
        _merge_kernel,
        grid=(n // tm,),
        in_specs=[row(BRANCH_WIDTH), row(BRANCH_WIDTH), row(BRANCH_WIDTH),
                  row(BRANCH_WIDTH, ZB_GATE // BRANCH_WIDTH), row(BRANCH_WIDTH),
                  row(N_BRANCHES * D_MODEL, ZB_GZ // (N_BRANCHES * D_MODEL)), row(D_MODEL),
                  full((N_BRANCHES, BRANCH_WIDTH, D_MODEL)), full((D_MODEL, D_MODEL)),
                  full((1, GDN_DV)), full((1, D_MODEL)), full((1, D_MODEL))],
        out_specs=row(D_MODEL),
        out_shape=jax.ShapeDtypeStruct((n, D_MODEL), F32),
        compiler_params=_cparams(1),
        name="merge_ln",
    )(ya, o_f, o_b, zb, yc, zb, x, w_branch, w_out, norm_w.astype(F32).reshape(1, GDN_DV),
      ln_g.astype(F32).reshape(1, D_MODEL), ln_b.astype(F32).reshape(1, D_MODEL))


FFN_SPLIT = 2


def _ffn_kernel(xp_ref, x_ref, xn_ref, wup_ref, cw_ref, cb_ref, wdn_ref, lng_ref, lnb_ref, o_ref, ob_ref, *,
                tiles_per_seq):
    i = pl.program_id(0)
    tm = x_ref.shape[0]
    halo = V7X_SUBLANES
    x = x_ref[...]
    xe = jnp.concatenate([xp_ref[...], x, xn_ref[...]], axis=0).astype(BF16)
    xb = x.astype(BF16)
    rows = lax.broadcasted_iota(jnp.int32, (tm + 2 * halo, 1), 0)
    first = (i % tiles_per_seq) == 0
    last = (i % tiles_per_seq) == tiles_per_seq - 1
    keep = jnp.logical_not(((rows < halo) & first) | ((rows >= tm + halo) & last))
    fc = D_FF // FFN_SPLIT
    ext = tm + 2 * halo
    acc = None
    for c in range(FFN_SPLIT):
        ge = jnp.dot(xe, wup_ref[:, c * fc:(c + 1) * fc], preferred_element_type=F32)
        ge = jnp.where(keep, ge, 0.0)
        up = jnp.dot(xb, wup_ref[:, D_FF + c * fc:D_FF + (c + 1) * fc], preferred_element_type=F32)
        cw = cw_ref[:, c * fc:(c + 1) * fc]
        conv = (pltpu.roll(ge, 1, 0) * cw[0:1] + ge * cw[1:2] + pltpu.roll(ge, ext - 1, 0) * cw[2:3])
        gate = conv[halo:halo + tm] + cb_ref[:, c * fc:(c + 1) * fc]
        act = (gate * jax.nn.sigmoid(gate) * up).astype(BF16)
        part = jnp.dot(act, wdn_ref[c * fc:(c + 1) * fc, :], preferred_element_type=F32)
        acc = part if acc is None else acc + part
    y = _layer_norm(DEEPNORM_ALPHA * x + acc, lng_ref[...], lnb_ref[...])
    o_ref[...] = y
    ob_ref[...] = y.astype(ob_ref.dtype)


def _ffn(x, w_up, conv_w, conv_b, w_down, ln_g, ln_b, seq, tm):
    n = x.shape[0]
    halo = V7X_SUBLANES
    per = tm // halo
    nh = n // halo
    full = lambda shape: pl.BlockSpec(shape, lambda i: (0,) * len(shape), pipeline_mode=pl.Buffered(1))
    kern = functools.partial(_ffn_kernel, tiles_per_seq=seq // tm)
    return pl.pallas_call(
        kern,
        grid=(n // tm,),
        in_specs=[pl.BlockSpec((halo, D_MODEL), lambda i: (jnp.maximum(i * per - 1, 0), 0)),
                  pl.BlockSpec((tm, D_MODEL), lambda i: (i, 0)),
                  pl.BlockSpec((halo, D_MODEL), lambda i: (jnp.minimum((i + 1) * per, nh - 1), 0)),
                  full((D_MODEL, 2 * D_FF)), full((FFN_CONV, D_FF)), full((1, D_FF)),
                  full((D_FF, D_MODEL)), full((1, D_MODEL)), full((1, D_MODEL))],
        out_specs=[pl.BlockSpec((tm, D_MODEL), lambda i: (i, 0)), pl.BlockSpec((tm, D_MODEL), lambda i: (i, 0))],
        out_shape=[jax.ShapeDtypeStruct((n, D_MODEL), F32), jax.ShapeDtypeStruct((n, D_MODEL), BF16)],
        compiler_params=_cparams(1),
        name="ffn_ln",
    )(x, x, x, w_up, conv_w.astype(F32), conv_b.astype(F32).reshape(1, D_FF), w_down,
      ln_g.astype(F32).reshape(1, D_MODEL), ln_b.astype(F32).reshape(1, D_MODEL))


def _pack_w_in(w):
    widths = (512, 512, 512, GDN_QKV, 512, 8, 8, 512, 128, 128, N_BRANCHES * D_MODEL)
    offs = [0]
    for wd in widths:
        offs.append(offs[-1] + wd)
    seg = lambda i: w[:, offs[i]:offs[i + 1]]
    aq, ak, av, bqkv, bgate, ba, bb, cq, ck, cv, gz = (seg(i) for i in range(len(widths)))
    dup = lambda t: jnp.concatenate([t[:, :HEAD_DIM], t[:, :HEAD_DIM], t[:, HEAD_DIM:], t[:, HEAD_DIM:]], axis=1)
    w_a = jnp.concatenate([aq * DIFF_QSCALE, ak, av, cq, dup(ck), dup(cv)], axis=1)
    w_b = jnp.concatenate([gz, bqkv, bgate], axis=1)
    pad = jnp.zeros((w.shape[0], ZC_COLS - 2 * GDN_CHAINS), w.dtype)
    w_c = jnp.concatenate([ba, bb, pad], axis=1)
    return w_a.astype(BF16), w_b.astype(BF16), w_c.astype(BF16)


def kernel(x, rel_bias, w_in, diff_lambda, diff_subln, gdn_conv, gdn_a_log, gdn_dt_bias, gdn_norm, swa_sink,
           w_branch, w_out, ln1_g, ln1_b, ffn_up, ffn_conv, ffn_conv_b, ffn_down, ln2_g, ln2_b):
    batch, seq, d = x.shape
    assert d == D_MODEL and seq % GDN_GROUP == 0 and seq % (DIFF_QROWS * DIFF_UNROLL) == 0
    n = batch * seq
    tm = 256
    tb_diff = (_diff_bias_tiles(rel_bias, seq) * LOG2_E).transpose(0, 1, 3, 2)
    tb_swa = _swa_bias_tiles(rel_bias)
    xf = x.reshape(n, d)
    xb = xf.astype(BF16)
    for l in range(DEPTH):
        w_a, w_b, w_c = _pack_w_in(w_in[l])
        za = _matmul(xb, w_a, BF16, 1024, ZA_COLS // 2, "in_proj_a")
        zb = _matmul(xb, w_b, BF16, 1024, ZB_COLS // 2, "in_proj_b")
        zc = _matmul(xb, w_c, F32, 1024, ZC_COLS, "in_proj_c")
        ya = _diff_attention(za, tb_diff, diff_lambda[l], diff_subln[l], batch, seq, l)
        yc = _swa_attention(za, tb_swa, swa_sink[l], batch, seq)
        qkvn = _gdn_conv(zb, gdn_conv[l], batch, seq)
        ab_t = zc[:, 0:2 * GDN_CHAINS].T
        gc, gd, gl, beta = _gdn_gates(ab_t, gdn_a_log[l], gdn_dt_bias[l], batch, seq)
        u, w, qg, qk, kgt = _gdn_local(qkvn, gc.T, gd.T, beta.T, gc, batch, seq)
        o_f, o_b = _gdn_scan(u, w, qg, qk, kgt, gl, batch, seq)
        xf = _merge(ya, o_f, o_b, zb, yc, xf, w_branch[l].astype(BF16), w_out[l].astype(BF16),
                    gdn_norm[l], ln1_g[l], ln1_b[l], 2 * tm)
        xf, xb = _ffn(xf, ffn_up[l].astype(BF16), ffn_conv[l], ffn_conv_b[l], ffn_down[l].astype(BF16),
                      ln2_g[l], ln2_b[l], seq, 2 * tm)
    return xf.reshape(batch, seq, d)
```

```python
import functools
import math

import jax
import jax.numpy as jnp
from jax import lax
from jax.experimental import pallas as pl
from jax.experimental.pallas import tpu as pltpu

F32 = jnp.float32
BF16 = jnp.bfloat16

D_MODEL = 1024
DEPTH = 2
HEAD_DIM = 64
DIFF_HEADS = 4
DIFF_V = 2 * HEAD_DIM
Q_BLOCK = 128
GDN_HEADS = 4
GDN_DK = 128
GDN_DV = 128
GDN_QKV = GDN_HEADS * (2 * GDN_DK + GDN_DV)
GDN_CONV = 5
GDN_CHUNK = 64
SWA_HEADS = 8
SWA_KV_HEADS = 2
SWA_WINDOW = 128
SWA_BLOCK = 128
BRANCH_WIDTH = 512
N_BRANCHES = 3
REL_BUCKETS = 32
REL_MAX_DIST = 128
D_FF = 2816
FFN_CONV = 3
DEEPNORM_ALPHA = (2 * DEPTH) ** 0.25
LN_EPS = 1e-5
RMS_EPS = 1e-6

V7X_LANES = 128
V7X_SUBLANES = 8
V7X_VMEM_BYTES = 64 * 1024 * 1024
VMEM_LIMIT = V7X_VMEM_BYTES * 7 // 8

NEG_BIG = -1e30

ZA_AQ, ZA_AK, ZA_AV = 0, 512, 1024
ZA_CQ = 1536
ZA_CK = 2048
ZA_CV = 2304
ZA_COLS = 2560
ZB_GZ = 0
ZB_QKV = 3072
ZB_GATE = 4608
ZB_COLS = 5120
ZC_COLS = 128

GDN_GROUP = 256
GDN_CHAINS = 2 * GDN_HEADS


def _cparams(n_grid):
    return pltpu.CompilerParams(dimension_semantics=("arbitrary",) * n_grid, vmem_limit_bytes=VMEM_LIMIT)


def _t5_bucket(rel):
    nb = REL_BUCKETS // 2
    ret = jnp.where(rel > 0, nb, 0)
    n = jnp.abs(rel)
    max_exact = nb // 2
    large = max_exact + (jnp.log(jnp.maximum(n, 1).astype(jnp.float32) / max_exact)
                         / math.log(REL_MAX_DIST / max_exact) * (nb - max_exact)).astype(jnp.int32)
    large = jnp.minimum(large, nb - 1)
    return ret + jnp.where(n < max_exact, n, large)


def _table_lookup(table, bucket):
    out = jnp.zeros((table.shape[1],) + bucket.shape, F32)
    for i in range(REL_BUCKETS):
        out = jnp.where(bucket[None] == i, table[i].astype(F32).reshape((-1,) + (1,) * bucket.ndim), out)
    return out


def _diff_bias_tiles(rel_bias, seq):
    nq = seq // Q_BLOCK
    d = jnp.arange(2 * nq - 1) - (nq - 1)
    r = jnp.arange(Q_BLOCK)
    rel = d[:, None, None] * Q_BLOCK + r[None, None, :] - r[None, :, None]
    return _table_lookup(rel_bias[:, :DIFF_HEADS], _t5_bucket(rel))


def _swa_bias_tiles(rel_bias):
    kb = 3 * SWA_BLOCK
    rel = jnp.arange(kb)[None, :] - SWA_WINDOW - jnp.arange(SWA_BLOCK)[:, None]
    bias = _table_lookup(rel_bias[:, DIFF_HEADS:], _t5_bucket(rel))
    return jnp.where((jnp.abs(rel) <= SWA_WINDOW)[None], bias, NEG_BIG)


def _matmul_kernel(x_ref, w_ref, o_ref):
    o_ref[...] = jnp.dot(x_ref[...].astype(BF16), w_ref[...],
                         preferred_element_type=F32).astype(o_ref.dtype)


def _matmul(x, w, out_dtype, tm, tn, name):
    m, k = x.shape
    n = w.shape[1]
    return pl.pallas_call(
        _matmul_kernel,
        grid=(n // tn, m // tm),
        in_specs=[pl.BlockSpec((tm, k), lambda j, i: (i, 0)),
                  pl.BlockSpec((k, tn), lambda j, i: (0, j))],
        out_specs=pl.BlockSpec((tm, tn), lambda j, i: (i, j)),
        out_shape=jax.ShapeDtypeStruct((m, n), out_dtype),
        compiler_params=_cparams(2),
        name=name,
    )(x, w)


LOG2_E = math.log2(math.e)
DIFF_QSCALE = HEAD_DIM ** -0.5 * LOG2_E
DIFF_QROWS = 2 * Q_BLOCK
DIFF_UNROLL = 4


def _diff_attn_kernel(q_ref, k_ref, v_ref, tb_ref, lam_ref, subln_ref, o_ref, s_scr, vx_scr, *, nq, lam_init):
    qr = DIFF_QROWS
    vx_scr[:, 0:DIFF_V] = v_ref[...]
    vx_scr[:, DIFF_V:2 * DIFF_V] = jnp.ones((v_ref.shape[0], DIFF_V), vx_scr.dtype)
    lv = lam_ref[...]
    lam = (jnp.exp(jnp.sum(lv[0:1] * lv[1:2], axis=-1, keepdims=True))
           - jnp.exp(jnp.sum(lv[2:3] * lv[3:4], axis=-1, keepdims=True)) + lam_init)
    lane = lax.broadcasted_iota(jnp.int32, (qr, 128), 1)
    zero = jnp.zeros((qr, 128), q_ref.dtype)

    def scores(qb, scr):
        r0 = pl.multiple_of(qb * qr, qr)
        q = q_ref[pl.ds(r0, qr), :]
        qz = jnp.concatenate([jnp.where(lane < HEAD_DIM, q, zero),
                              jnp.where(lane >= HEAD_DIM, q, zero)], axis=0)
        for kp in range(nq // 2):
            kblk = k_ref[kp * qr:(kp + 1) * qr, :]
            s = lax.dot_general(qz, kblk, (((1,), (1,)), ((), ())), preferred_element_type=F32)
            for a in range(2):
                d0 = nq - 1 - (2 * qb + a) + 2 * kp
                bias = jnp.concatenate([tb_ref[0, d0], tb_ref[0, d0 + 1]], axis=1)
                for c in range(2):
                    rows = slice(c * qr + a * Q_BLOCK, c * qr + (a + 1) * Q_BLOCK)
                    scr[rows, kp * qr:(kp + 1) * qr] = s[rows] + bias

    def softmax(scr):
        s = scr[...]
        return jnp.exp2(s - jnp.max(s, axis=-1, keepdims=True)).astype(BF16)

    def finish(qb, e):
        r0 = pl.multiple_of(qb * qr, qr)
        pv = jnp.dot(e, vx_scr[...], preferred_element_type=F32)
        pv = pv[:, 0:DIFF_V] / pv[:, DIFF_V:2 * DIFF_V]
        o = pv[:qr] - lam * pv[qr:]
        o = o * lax.rsqrt(jnp.mean(o * o, axis=-1, keepdims=True) + RMS_EPS) * subln_ref[...]
        o_ref[pl.ds(r0, qr), :] = (o * (1.0 - lam_init)).astype(o_ref.dtype)

    def body(i, carry):
        blocks = [i * DIFF_UNROLL + j for j in range(DIFF_UNROLL)]
        for j, qb in enumerate(blocks):
            scores(qb, s_scr.at[j])
        probs = [softmax(s_scr.at[j]) for j in range(DIFF_UNROLL)]
        for qb, e in zip(blocks, probs):
            finish(qb, e)
        return carry

    lax.fori_loop(0, nq // (2 * DIFF_UNROLL), body, 0)


def _diff_attention(za, tb, lam_vecs, subln, batch, seq, layer_idx):
    nq = seq // Q_BLOCK
    lam_init = 0.8 - 0.6 * math.exp(-0.3 * layer_idx)
    kern = functools.partial(_diff_attn_kernel, nq=nq, lam_init=lam_init)
    qcol, kcol, vcol = ZA_AQ // 128, ZA_AK // 128, ZA_AV // 128
    return pl.pallas_call(
        kern,
        grid=(batch, DIFF_HEADS),
        in_specs=[
            pl.BlockSpec((seq, 128), lambda b, h: (b, qcol + h)),
            pl.BlockSpec((seq, 128), lambda b, h: (b, kcol + h)),
            pl.BlockSpec((seq, 128), lambda b, h: (b, vcol + h)),
            pl.BlockSpec((1, 2 * nq - 1, Q_BLOCK, Q_BLOCK), lambda b, h: (h, 0, 0, 0)),
            pl.BlockSpec((4, HEAD_DIM), lambda b, h: (0, 0)),
            pl.BlockSpec((1, DIFF_V), lambda b, h: (0, 0)),
        ],
        out_specs=pl.BlockSpec((seq, 128), lambda b, h: (b, h)),
        out_shape=jax.ShapeDtypeStruct((batch * seq, BRANCH_WIDTH), BF16),
        scratch_shapes=[pltpu.VMEM((DIFF_UNROLL, 2 * DIFF_QROWS, seq), F32),
                        pltpu.VMEM((seq, 2 * DIFF_V), BF16)],
        compiler_params=_cparams(2),
        name="diff_attention",
    )(za, za, za, tb, lam_vecs.astype(F32), subln.astype(F32).reshape(1, DIFF_V))


def _swa_kernel(q_ref, k_ref, v_ref, bias_ref, sink_ref, o_ref, kp_scr, vp_scr, *, nb):
    seq = q_ref.shape[0]
    blk = SWA_BLOCK
    pad = jnp.zeros((blk, kp_scr.shape[1]), kp_scr.dtype)
    for scr, ref in ((kp_scr, k_ref), (vp_scr, v_ref)):
        scr[0:blk] = pad
        scr[blk + seq:2 * blk + seq] = pad
        scr[blk:blk + seq] = ref[...]
    krow = lax.broadcasted_iota(jnp.int32, (3 * blk, blk), 0)
    lane = lax.broadcasted_iota(jnp.int32, (blk, 128), 1)
    low = lane < HEAD_DIM
    zero = jnp.zeros((blk, 128), q_ref.dtype)
    scale = HEAD_DIM ** -0.5
    rep = SWA_HEADS // SWA_KV_HEADS

    def body(n, carry):
        r0 = pl.multiple_of(n * blk, blk)
        q = q_ref[pl.ds(r0, blk), :]
        kw = kp_scr[pl.ds(r0, 3 * blk), :]
        vw = vp_scr[pl.ds(r0, 3 * blk), :]
        outside = ((krow < blk) & (n == 0)) | ((krow >= 2 * blk) & (n == nb - 1))
        heads = range(SWA_HEADS)
        qms = [jnp.where(low if hd % 2 == 0 else jnp.logical_not(low),
                         q[:, (hd // 2) * 128:(hd // 2 + 1) * 128] * scale, zero) for hd in heads]
        ss = [lax.dot_general(kw[:, (hd // rep) * 128:(hd // rep + 1) * 128], qms[hd], (((1,), (1,)), ((), ())),
                              preferred_element_type=F32) for hd in heads]
        ss = [jnp.where(outside, NEG_BIG, ss[hd] + bias_ref[hd]) for hd in heads]
        sinks = [sink_ref[hd][:, 0:1] for hd in heads]
        ms = [jnp.maximum(jnp.max(ss[hd], axis=0, keepdims=True), sinks[hd]) for hd in heads]
        es = [jnp.exp(ss[hd] - ms[hd]) for hd in heads]
        dens = [jnp.sum(es[hd], axis=0, keepdims=True) + jnp.exp(sinks[hd] - ms[hd]) for hd in heads]
        ps = [(es[hd] / dens[hd]).astype(BF16) for hd in heads]
        halves = [lax.dot_general(ps[hd], vw[:, (hd // rep) * 128:(hd // rep + 1) * 128], (((0,), (0,)), ((), ())),
                                  preferred_element_type=F32) for hd in heads]
        o = jnp.concatenate([jnp.where(low, halves[2 * c], halves[2 * c + 1]) for c in range(SWA_HEADS // 2)], axis=1)
        o_ref[pl.ds(r0, blk), :] = o.astype(o_ref.dtype)
        return carry

    lax.fori_loop(0, nb, body, 0)


def _swa_attention(za, bias, sink, batch, seq):
    nb = seq // SWA_BLOCK
    kern = functools.partial(_swa_kernel, nb=nb)
    sink_b = jnp.broadcast_to(sink.astype(F32).reshape(SWA_HEADS, 1, 1), (SWA_HEADS, 1, 128))
    kvw = 2 * SWA_KV_HEADS * HEAD_DIM
    return pl.pallas_call(
        kern,
        grid=(batch,),
        in_specs=[
            pl.BlockSpec((seq, BRANCH_WIDTH), lambda b: (b, ZA_CQ // BRANCH_WIDTH)),
            pl.BlockSpec((seq, kvw), lambda b: (b, ZA_CK // kvw)),
            pl.BlockSpec((seq, kvw), lambda b: (b, ZA_CV // kvw)),
            pl.BlockSpec((SWA_HEADS, 3 * SWA_BLOCK, SWA_BLOCK), lambda b: (0, 0, 0)),
            pl.BlockSpec((SWA_HEADS, 1, 128), lambda b: (0, 0, 0)),
        ],
        out_specs=pl.BlockSpec((seq, BRANCH_WIDTH), lambda b: (b, 0)),
        out_shape=jax.ShapeDtypeStruct((batch * seq, BRANCH_WIDTH), BF16),
        scratch_shapes=[pltpu.VMEM((seq + 2 * SWA_BLOCK, kvw), BF16),
                        pltpu.VMEM((seq + 2 * SWA_BLOCK, kvw), BF16)],
        compiler_params=_cparams(1),
        name="swa_attention",
    )(za, za, za, bias.transpose(0, 2, 1), sink_b)


def _shift_rows(x, d):
    rows = x.shape[0]
    t = lax.broadcasted_iota(jnp.int32, x.shape, 0)
    rolled = pltpu.roll(x, (-d) % rows, 0)
    return jnp.where((t + d >= 0) & (t + d < rows), rolled, 0.0)


def _gdn_conv_kernel(x_ref, w_ref, o_ref):
    j = pl.program_id(1)
    x = x_ref[...].astype(F32)
    w = w_ref[...]
    half = GDN_CONV // 2
    acc = x * w[half:half + 1]
    for tap in range(GDN_CONV):
        if tap != half:
            acc = acc + _shift_rows(x, tap - half) * w[tap:tap + 1]
    h = acc * jax.nn.sigmoid(acc)
    inv = lax.rsqrt(jnp.sum(h * h, axis=-1, keepdims=True) + RMS_EPS)
    factor = jnp.where(j < 2 * GDN_HEADS, inv, 1.0) * jnp.where(j < GDN_HEADS, GDN_DK ** -0.5, 1.0)
    o_ref[...] = h * factor


def _gdn_conv(zb, conv_w, batch, seq):
    c0 = ZB_QKV // 128
    return pl.pallas_call(
        _gdn_conv_kernel,
        grid=(batch, GDN_QKV // 128),
        in_specs=[pl.BlockSpec((seq, 128), lambda b, j: (b, c0 + j)),
                  pl.BlockSpec((GDN_CONV, 128), lambda b, j: (0, j))],
        out_specs=pl.BlockSpec((seq, 128), lambda b, j: (b, j)),
        out_shape=jax.ShapeDtypeStruct((batch * seq, GDN_QKV), F32),
        compiler_params=_cparams(2),
        name="gdn_conv",
    )(zb, conv_w.astype(F32))


def _gdn_gate_kernel(ab_ref, alog_ref, dtb_ref, gc_ref, gd_ref, gl_ref, beta_ref):
    ab = ab_ref[...]
    a = ab[0:GDN_CHAINS]
    seq = a.shape[1]
    xs = a + dtb_ref[...]
    softplus = jnp.maximum(xs, 0.0) + jnp.log1p(jnp.exp(-jnp.abs(xs)))
    g = -jnp.exp(alog_ref[...]) * softplus
    beta_ref[...] = jax.nn.sigmoid(ab[GDN_CHAINS:2 * GDN_CHAINS])
    pos = lax.broadcasted_iota(jnp.int32, g.shape, 1) % GDN_CHUNK
    cf = g
    cr = g
    s = 1
    while s < GDN_CHUNK:
        cf = cf + jnp.where(pos >= s, pltpu.roll(cf, s, 1), 0.0)
        cr = cr + jnp.where(pos < GDN_CHUNK - s, pltpu.roll(cr, seq - s, 1), 0.0)
        s *= 2
    fwd = lax.broadcasted_iota(jnp.int32, g.shape, 0) < GDN_HEADS
    gc_ref[...] = jnp.where(fwd, cf, cr)
    gd_ref[...] = jnp.where(fwd, cr, cf) - g
    gl_ref[...] = cf + cr - g


def _gdn_gates(ab_t, a_log, dt_bias, batch, seq):
    row = pl.BlockSpec((GDN_CHAINS, seq), lambda b: (0, b))
    shp = jax.ShapeDtypeStruct((GDN_CHAINS, batch * seq), F32)
    par = pl.BlockSpec((GDN_CHAINS, 1), lambda b: (0, 0))
    return pl.pallas_call(
        _gdn_gate_kernel,
        grid=(batch,),
        in_specs=[pl.BlockSpec((2 * GDN_CHAINS, seq), lambda b: (0, b)), par, par],
        out_specs=[row, row, row, row],
        out_shape=[shp, shp, shp, shp],
        compiler_params=_cparams(1),
        name="gdn_gates",
    )(ab_t, a_log.astype(F32).reshape(GDN_CHAINS, 1), dt_bias.astype(F32).reshape(GDN_CHAINS, 1))


def _hdot(a, b):
    return jnp.dot(a.astype(BF16), b.astype(BF16), preferred_element_type=F32)


def _gdn_local_kernel(qkv_ref, gcc_ref, gdc_ref, betac_ref, gcr_ref,
                      u_ref, w_ref, qg_ref, qk_ref, kgt_ref):
    g = GDN_GROUP
    c = GDN_CHUNK
    ri = lax.broadcasted_iota(jnp.int32, (g, g), 0)
    ci = lax.broadcasted_iota(jnp.int32, (g, g), 1)
    same = (ri // c) == (ci // c)
    ahead = jnp.where(same, ri - ci, -g)
    behind = jnp.where(same, ci - ri, -g)
    eye_s = (lax.broadcasted_iota(jnp.int32, (c, g), 0)
             == lax.broadcasted_iota(jnp.int32, (c, g), 1) % c).astype(F32)

    def block_diag(xs):
        return jnp.where(same, jnp.concatenate([xs] * (g // c), axis=0), 0.0)

    def row_blocks_sum(xd):
        out = xd[0:c]
        for i in range(1, g // c):
            out = out + xd[i * c:(i + 1) * c]
        return out

    chains = [(h, d) for h in range(GDN_HEADS) for d in range(2)]
    xs_l, xd_l, p_l, rhs_l = [], [], [], []
    for h in range(GDN_HEADS):
        q = qkv_ref[:, h * GDN_DK:(h + 1) * GDN_DK]
        k = qkv_ref[:, (GDN_HEADS + h) * GDN_DK:(GDN_HEADS + h + 1) * GDN_DK]
        v = qkv_ref[:, (2 * GDN_HEADS + h) * GDN_DK:(2 * GDN_HEADS + h + 1) * GDN_DK]
        kbf = k.astype(BF16)
        qk_raw = lax.dot_general(q.astype(BF16), kbf, (((1,), (1,)), ((), ())), preferred_element_type=F32)
        for d in range(2):
            ch = d * GDN_HEADS + h
            lo, hi = ch * GDN_DK, (ch + 1) * GDN_DK
            gcc = gcc_ref[:, ch:ch + 1]
            gdc = gdc_ref[:, ch:ch + 1]
            beta = betac_ref[:, ch:ch + 1]
            gcr = gcr_ref[ch:ch + 1, :]
            order = ahead if d == 0 else behind
            kb = k * beta
            kk = lax.dot_general(kb.astype(BF16), kbf, (((1,), (1,)), ((), ())), preferred_element_type=F32)
            decay = jnp.exp(jnp.where(order >= 0, gcc - gcr, NEG_BIG))
            xd = -jnp.where(order > 0, kk * decay, 0.0)
            qkm = qk_raw * decay
            egc = jnp.exp(gcc)
            qg_ref[:, lo:hi] = (q * egc).astype(qg_ref.dtype)
            qk_ref[:, lo:hi] = (qkm[:, 0:128] + qkm[:, 128:256]).astype(qk_ref.dtype)
            kgt_ref[0, lo:hi, :] = (k * jnp.exp(gdc)).T.astype(kgt_ref.dtype)
            xs = row_blocks_sum(xd)
            xs_l.append(xs)
            xd_l.append(xd)
            p_l.append(eye_s + xs)
            rhs_l.append(jnp.concatenate([v * beta, kb * egc], axis=1).astype(BF16))
    n_ch = len(chains)
    xs_l = [_hdot(xs_l[i], xd_l[i]) for i in range(n_ch)]
    for _ in range(4):
        r_l = [_hdot(jnp.concatenate([p_l[i], xs_l[i]], axis=0), block_diag(xs_l[i])) for i in range(n_ch)]
        p_l = [p_l[i] + r_l[i][0:c] for i in range(n_ch)]
        xs_l = [r_l[i][c:2 * c] for i in range(n_ch)]
    p_l = [p_l[i] + _hdot(p_l[i], block_diag(xs_l[i])) for i in range(n_ch)]
    sol_l = [_hdot(block_diag(p_l[i]), rhs_l[i]) for i in range(n_ch)]
    for i, (h, d) in enumerate(chains):
        ch = d * GDN_HEADS + h
        lo, hi = ch * GDN_DK, (ch + 1) * GDN_DK
        u_ref[:, lo:hi] = sol_l[i][:, 0:GDN_DV]
        w_ref[:, lo:hi] = sol_l[i][:, GDN_DV:].astype(w_ref.dtype)


def _gdn_local(qkvn, gcc, gdc, betac, gcr, batch, seq):
    ng = seq // GDN_GROUP
    wide = GDN_CHAINS * GDN_DK
    colb = pl.BlockSpec((GDN_GROUP, GDN_CHAINS), lambda b, s: (b * ng + s, 0))
    outb = pl.BlockSpec((GDN_GROUP, wide), lambda b, s: (b * ng + s, 0))
    n = batch * seq
    return pl.pallas_call(
        _gdn_local_kernel,
        grid=(batch, ng),
        in_specs=[pl.BlockSpec((GDN_GROUP, GDN_QKV), lambda b, s: (b * ng + s, 0)),
                  colb, colb, colb,
                  pl.BlockSpec((GDN_CHAINS, GDN_GROUP), lambda b, s: (0, b * ng + s))],
        out_specs=[outb, outb, outb, outb,
                   pl.BlockSpec((1, wide, GDN_GROUP), lambda b, s: (b, 0, s))],
        out_shape=[jax.ShapeDtypeStruct((n, wide), F32),
                   jax.ShapeDtypeStruct((n, wide), BF16),
                   jax.ShapeDtypeStruct((n, wide), BF16),
                   jax.ShapeDtypeStruct((n, wide), BF16),
                   jax.ShapeDtypeStruct((batch, wide, seq), BF16)],
        compiler_params=_cparams(2),
        name="gdn_local",
    )(qkvn, gcc, gdc, betac, gcr)


def _gdn_scan_kernel(uf_ref, ub_ref, wf_ref, wb_ref, qgf_ref, qgb_ref, qkf_ref, qkb_ref,
                     kgf_ref, kgb_ref, glf_ref, glb_ref, of_ref, ob_ref, s_scr):
    @pl.when(pl.program_id(1) == 0)
    def _():
        s_scr[...] = jnp.zeros_like(s_scr)

    c = GDN_CHUNK
    nc = GDN_GROUP // c
    zeros = jnp.zeros((c, GDN_DV), BF16)
    dirs = ((uf_ref, wf_ref, qgf_ref, qkf_ref, kgf_ref, glf_ref, of_ref),
            (ub_ref, wb_ref, qgb_ref, qkb_ref, kgb_ref, glb_ref, ob_ref))
    states = [s_scr[ch] for ch in range(GDN_CHAINS)]
    chains = [(d, h) for d in range(2) for h in range(GDN_HEADS)]
    for step in range(nc):
        tops, v2s = [], []
        for d, h in chains:
            u_ref, w_ref, qg_ref = dirs[d][0:3]
            i = step if d == 0 else nc - 1 - step
            rows = slice(i * c, (i + 1) * c)
            lo, hi = h * GDN_DK, (h + 1) * GDN_DK
            sb = states[d * GDN_HEADS + h].astype(BF16)
            top = jnp.dot(jnp.concatenate([w_ref[rows, lo:hi], qg_ref[rows, lo:hi]], axis=0), sb,
                          preferred_element_type=F32)
            vb = (u_ref[rows, lo:hi] - top[0:c]).astype(BF16)
            v2s.append(jnp.concatenate([vb, zeros] if i % 2 == 0 else [zeros, vb], axis=0))
            tops.append(top)
        for n, (d, h) in enumerate(chains):
            qk_ref, kg_ref, gl_ref, o_ref = dirs[d][3:7]
            ch = d * GDN_HEADS + h
            i = step if d == 0 else nc - 1 - step
            rows = slice(i * c, (i + 1) * c)
            pair = slice((i // 2) * 2 * c, (i // 2 + 1) * 2 * c)
            lo, hi = h * GDN_DK, (h + 1) * GDN_DK
            bot = jnp.dot(jnp.concatenate([qk_ref[rows, lo:hi], kg_ref[0, lo:hi, pair]], axis=0), v2s[n],
                          preferred_element_type=F32)
            o_ref[rows, lo:hi] = tops[n][c:2 * c] + bot[0:c]
            decay = jnp.exp(gl_ref[ch:ch + 1, i * c:i * c + 1])
            states[ch] = states[ch] * decay + bot[c:c + GDN_DK]
    for ch in range(GDN_CHAINS):
        s_scr[ch] = states[ch]


def _gdn_scan(u, w, qg, qk, kgt, gl, batch, seq):
    ng = seq // GDN_GROUP
    half = GDN_HEADS * GDN_DK
    fwd = lambda b, s: (b * ng + s, 0)
    bwd = lambda b, s: (b * ng + ng - 1 - s, 1)
    blk = (GDN_GROUP, half)
    n = batch * seq
    pair = lambda arr: (arr, arr)
    out_f = pl.BlockSpec(blk, fwd)
    out_b = pl.BlockSpec(blk, lambda b, s: (b * ng + ng - 1 - s, 0))
    in_specs = []
    for _ in range(4):
        in_specs += [pl.BlockSpec(blk, fwd), pl.BlockSpec(blk, bwd)]
    in_specs += [pl.BlockSpec((1, half, GDN_GROUP), lambda b, s: (b, 0, s)),
                 pl.BlockSpec((1, half, GDN_GROUP), lambda b, s: (b, 1, ng - 1 - s)),
                 pl.BlockSpec((GDN_CHAINS, GDN_GROUP), lambda b, s: (0, b * ng + s)),
                 pl.BlockSpec((GDN_CHAINS, GDN_GROUP), lambda b, s: (0, b * ng + ng - 1 - s))]
    return pl.pallas_call(
        _gdn_scan_kernel,
        grid=(batch, ng),
        in_specs=in_specs,
        out_specs=[out_f, out_b],
        out_shape=[jax.ShapeDtypeStruct((n, half), F32), jax.ShapeDtypeStruct((n, half), F32)],
        scratch_shapes=[pltpu.VMEM((GDN_CHAINS, GDN_DK, GDN_DV), F32)],
        compiler_params=_cparams(2),
        name="gdn_scan",
    )(*pair(u), *pair(w), *pair(qg), *pair(qk), *pair(kgt), *pair(gl))


def _gdn_fused_kernel(qkvf_ref, qkvb_ref, gccf_ref, gccb_ref, gdcf_ref, gdcb_ref, betaf_ref, betab_ref,
                      gcrf_ref, gcrb_ref, glf_ref, glb_ref, of_ref, ob_ref, s_scr):
    @pl.when(pl.program_id(1) == 0)
    def _():
        s_scr[...] = jnp.zeros_like(s_scr)

    g = GDN_GROUP
    c = GDN_CHUNK
    nc = g // c
    ri = lax.broadcasted_iota(jnp.int32, (g, g), 0)
    ci = lax.broadcasted_iota(jnp.int32, (g, g), 1)
    same = (ri // c) == (ci // c)
    ahead = jnp.where(same, ri - ci, -g)
    behind = jnp.where(same, ci - ri, -g)
    eye_s = (lax.broadcasted_iota(jnp.int32, (c, g), 0)
             == lax.broadcasted_iota(jnp.int32, (c, g), 1) % c).astype(F32)

    def block_diag(xs):
        return jnp.where(same, jnp.concatenate([xs] * nc, axis=0), 0.0)

    def row_blocks_sum(xd):
        out = xd[0:c]
        for i in range(1, nc):
            out = out + xd[i * c:(i + 1) * c]
        return out

    dirs = ((qkvf_ref, gccf_ref, gdcf_ref, betaf_ref, gcrf_ref, glf_ref, of_ref, ahead),
            (qkvb_ref, gccb_ref, gdcb_ref, betab_ref, gcrb_ref, glb_ref, ob_ref, behind))
    chains = [(d, h) for d in range(2) for h in range(GDN_HEADS)]
    n_ch = len(chains)
    xs_l, xd_l, p_l, rhs_l, qg_l, qk_l, kgt_l = [], [], [], [], [], [], []
    for n, (d, h) in enumerate(chains):
        qkv_ref, gcc_ref, gdc_ref, beta_ref, gcr_ref = dirs[d][0:5]
        order = dirs[d][7]
        q = qkv_ref[:, h * GDN_DK:(h + 1) * GDN_DK]
        k = qkv_ref[:, (GDN_HEADS + h) * GDN_DK:(GDN_HEADS + h + 1) * GDN_DK]
        v = qkv_ref[:, (2 * GDN_HEADS + h) * GDN_DK:(2 * GDN_HEADS + h + 1) * GDN_DK]
        kbf = k.astype(BF16)
        qk_raw = lax.dot_general(q.astype(BF16), kbf, (((1,), (1,)), ((), ())), preferred_element_type=F32)
        gcc = gcc_ref[:, n:n + 1]
        gdc = gdc_ref[:, n:n + 1]
        beta = beta_ref[:, n:n + 1]
        gcr = gcr_ref[n:n + 1, :]
        kb = k * beta
        kk = lax.dot_general(kb.astype(BF16), kbf, (((1,), (1,)), ((), ())), preferred_element_type=F32)
        decay = jnp.exp(jnp.where(order >= 0, gcc - gcr, NEG_BIG))
        xd = -jnp.where(order > 0, kk * decay, 0.0)
        qkm = qk_raw * decay
        egc = jnp.exp(gcc)
        qg_l.append((q * egc).astype(BF16))
        qk_l.append((qkm[:, 0:128] + qkm[:, 128:256]).astype(BF16))
        kgt_l.append((k * jnp.exp(gdc)).T.astype(BF16))
        xs = row_blocks_sum(xd)
        xs_l.append(xs)
        xd_l.append(xd)
        p_l.append(eye_s + xs)
        rhs_l.append(jnp.concatenate([v * beta, kb * egc], axis=1).astype(BF16))
    xs_l = [_hdot(xs_l[i], xd_l[i]) for i in range(n_ch)]
    for _ in range(4):
        r_l = [_hdot(jnp.concatenate([p_l[i], xs_l[i]], axis=0), block_diag(xs_l[i])) for i in range(n_ch)]
        p_l = [p_l[i] + r_l[i][0:c] for i in range(n_ch)]
        xs_l = [r_l[i][c:2 * c] for i in range(n_ch)]
    p_l = [p_l[i] + _hdot(p_l[i], block_diag(xs_l[i])) for i in range(n_ch)]
    sol_l = [_hdot(block_diag(p_l[i]), rhs_l[i]) for i in range(n_ch)]
    u_l = [sol[:, 0:GDN_DV] for sol in sol_l]
    w_l = [sol[:, GDN_DV:].astype(BF16) for sol in sol_l]

    zeros = jnp.zeros((c, GDN_DV), BF16)
    states = [s_scr[n] for n in range(n_ch)]
    for step in range(nc):
        tops, v2s = [], []
        for n, (d, h) in enumerate(chains):
            i = step if d == 0 else nc - 1 - step
            rows = slice(i * c, (i + 1) * c)
            sb = states[n].astype(BF16)
            top = jnp.dot(jnp.concatenate([w_l[n][rows], qg_l[n][rows]], axis=0), sb, preferred_element_type=F32)
            vb = (u_l[n][rows] - top[0:c]).astype(BF16)
            v2s.append(jnp.concatenate([vb, zeros] if i % 2 == 0 else [zeros, vb], axis=0))
            tops.append(top)
        for n, (d, h) in enumerate(chains):
            gl_ref, o_ref = dirs[d][5:7]
            i = step if d == 0 else nc - 1 - step
            rows = slice(i * c, (i + 1) * c)
            pair = slice((i // 2) * 2 * c, (i // 2 + 1) * 2 * c)
            bot = jnp.dot(jnp.concatenate([qk_l[n][rows], kgt_l[n][:, pair]], axis=0), v2s[n],
                          preferred_element_type=F32)
            o_ref[rows, h * GDN_DV:(h + 1) * GDN_DV] = tops[n][c:2 * c] + bot[0:c]
            decay = jnp.exp(gl_ref[n:n + 1, i * c:i * c + 1])
            states[n] = states[n] * decay + bot[c:c + GDN_DK]
    for n in range(n_ch):
        s_scr[n] = states[n]


def _gdn_fused(qkvn, gcc, gdc, betac, gcr, gl, batch, seq):
    ng = seq // GDN_GROUP
    half = GDN_HEADS * GDN_DV
    n = batch * seq
    fwd = lambda b, s: (b * ng + s, 0)
    bwd = lambda b, s: (b * ng + ng - 1 - s, 0)
    fwd_t = lambda b, s: (0, b * ng + s)
    bwd_t = lambda b, s: (0, b * ng + ng - 1 - s)
    both = lambda shape, f, r: [pl.BlockSpec(shape, f), pl.BlockSpec(shape, r)]
    col = (GDN_GROUP, GDN_CHAINS)
    row = (GDN_CHAINS, GDN_GROUP)
    return pl.pallas_call(
        _gdn_fused_kernel,
        grid=(batch, ng),
        in_specs=(both((GDN_GROUP, GDN_QKV), fwd, bwd) + both(col, fwd, bwd) + both(col, fwd, bwd)
                  + both(col, fwd, bwd) + both(row, fwd_t, bwd_t) + both(row, fwd_t, bwd_t)),
        out_specs=both((GDN_GROUP, half), fwd, bwd),
        out_shape=[jax.ShapeDtypeStruct((n, half), F32), jax.ShapeDtypeStruct((n, half), F32)],
        scratch_shapes=[pltpu.VMEM((GDN_CHAINS, GDN_DK, GDN_DV), F32)],
        compiler_params=_cparams(2),
        name="gdn_fused",
    )(qkvn, qkvn, gcc, gcc, gdc, gdc, betac, betac, gcr, gcr, gl, gl)


def _layer_norm(r, g, b):
    mu = jnp.mean(r, axis=-1, keepdims=True)
    var = jnp.mean(jnp.square(r - mu), axis=-1, keepdims=True)
    return (r - mu) * lax.rsqrt(var + LN_EPS) * g + b


def _merge_kernel(ya_ref, of_ref, ob_ref, gate_ref, yc_ref, gz_ref, x_ref, wb_ref, wo_ref,
                  nw_ref, lng_ref, lnb_ref, o_ref):
    segs = []
    for h in range(GDN_HEADS):
        sl = slice(h * GDN_DV, (h + 1) * GDN_DV)
        o = of_ref[:, sl] + ob_ref[:, sl]
        o = o * lax.rsqrt(jnp.mean(o * o, axis=-1, keepdims=True) + RMS_EPS) * nw_ref[...]
        gt = gate_ref[:, sl].astype(F32)
        segs.append(o * (gt * jax.nn.sigmoid(gt)))
    yb = jnp.concatenate(segs, axis=1).astype(BF16)
    merged = None
    for n, y in enumerate((ya_ref[...], yb, yc_ref[...])):
        proj = jnp.dot(y, wb_ref[n], preferred_element_type=F32)
        term = jax.nn.sigmoid(gz_ref[:, n * D_MODEL:(n + 1) * D_MODEL].astype(F32)) * proj
        merged = term if merged is None else merged + term
    hmix = jnp.dot(merged.astype(BF16), wo_ref[...], preferred_element_type=F32)
    o_ref[...] = _layer_norm(DEEPNORM_ALPHA * x_ref[...] + hmix, lng_ref[...], lnb_ref[...])


def _merge(ya, o_f, o_b, zb, yc, x, w_branch, w_out, norm_w, ln_g, ln_b, tm):
    n = x.shape[0]
    row = lambda width, col=0: pl.BlockSpec((tm, width), lambda i, col=col: (i, col))
    full = lambda shape: pl.BlockSpec(shape, lambda i: (0,) * len(shape), pipeline_mode=pl.Buffered(1))
    return pl.pallas_call(
        _merge_kernel,
        grid=(n // tm,),
        in_specs=[row(BRANCH_WIDTH), row(BRANCH_WIDTH), row(BRANCH_WIDTH),
                  row(BRANCH_WIDTH, ZB_GATE // BRANCH_WIDTH), row(BRANCH_WIDTH),
                  row(N_BRANCHES * D_MODEL, ZB_GZ // (N_BRANCHES * D_MODEL)), row(D_MODEL),
                  full((N_BRANCHES, BRANCH_WIDTH, D_MODEL)), full((D_MODEL, D_MODEL)),
                  full((1, GDN_DV)), full((1, D_MODEL)), full((1, D_MODEL))],
        out_specs=row(D_MODEL),
        out_shape=jax.ShapeDtypeStruct((n, D_MODEL), F32),
        compiler_params=_cparams(1),
        name="merge_ln",
    )(ya, o_f, o_b, zb, yc, zb, x, w_branch, w_out, norm_w.astype(F32).reshape(1, GDN_DV),
      ln_g.astype(F32).reshape(1, D_MODEL), ln_b.astype(F32).reshape(1, D_MODEL))


FFN_SPLIT = 2


def _ffn_kernel(xp_ref, x_ref, xn_ref, wup_ref, cw_ref, cb_ref, wdn_ref, lng_ref, lnb_ref, o_ref, ob_ref, *,
                tiles_per_seq):
    i = pl.program_id(0)
    tm = x_ref.shape[0]
    halo = V7X_SUBLANES
    x = x_ref[...]
    xe = jnp.concatenate([xp_ref[...], x, xn_ref[...]], axis=0).astype(BF16)
    xb = x.astype(BF16)
    rows = lax.broadcasted_iota(jnp.int32, (tm + 2 * halo, 1), 0)
    first = (i % tiles_per_seq) == 0
    last = (i % tiles_per_seq) == tiles_per_seq - 1
    keep = jnp.logical_not(((rows < halo) & first) | ((rows >= tm + halo) & last))
    fc = D_FF // FFN_SPLIT
    ext = tm + 2 * halo
    acc = None
    for c in range(FFN_SPLIT):
        ge = jnp.dot(xe, wup_ref[:, c * fc:(c + 1) * fc], preferred_element_type=F32)
        ge = jnp.where(keep, ge, 0.0)
        up = jnp.dot(xb, wup_ref[:, D_FF + c * fc:D_FF + (c + 1) * fc], preferred_element_type=F32)
        cw = cw_ref[:, c * fc:(c + 1) * fc]
        conv = (pltpu.roll(ge, 1, 0) * cw[0:1] + ge * cw[1:2] + pltpu.roll(ge, ext - 1, 0) * cw[2:3])
        gate = conv[halo:halo + tm] + cb_ref[:, c * fc:(c + 1) * fc]
        act = (gate * jax.nn.sigmoid(gate) * up).astype(BF16)
        part = jnp.dot(act, wdn_ref[c * fc:(c + 1) * fc, :], preferred_element_type=F32)
        acc = part if acc is None else acc + part
    y = _layer_norm(DEEPNORM_ALPHA * x + acc, lng_ref[...], lnb_ref[...])
    o_ref[...] = y
    ob_ref[...] = y.astype(ob_ref.dtype)


def _ffn(x, w_up, conv_w, conv_b, w_down, ln_g, ln_b, seq, tm):
    n = x.shape[0]
    halo = V7X_SUBLANES
    per = tm // halo
    nh = n // halo
    full = lambda shape: pl.BlockSpec(shape, lambda i: (0,) * len(shape), pipeline_mode=pl.Buffered(1))
    kern = functools.partial(_ffn_kernel, tiles_per_seq=seq // tm)
    return pl.pallas_call(
        kern,
        grid=(n // tm,),
        in_specs=[pl.BlockSpec((halo, D_MODEL), lambda i: (jnp.maximum(i * per - 1, 0), 0)),
                  pl.BlockSpec((tm, D_MODEL), lambda i: (i, 0)),
                  pl.BlockSpec((halo, D_MODEL), lambda i: (jnp.minimum((i + 1) * per, nh - 1), 0)),
                  full((D_MODEL, 2 * D_FF)), full((FFN_CONV, D_FF)), full((1, D_FF)),
                  full((D_FF, D_MODEL)), full((1, D_MODEL)), full((1, D_MODEL))],
        out_specs=[pl.BlockSpec((tm, D_MODEL), lambda i: (i, 0)), pl.BlockSpec((tm, D_MODEL), lambda i: (i, 0))],
        out_shape=[jax.ShapeDtypeStruct((n, D_MODEL), F32), jax.ShapeDtypeStruct((n, D_MODEL), BF16)],
        compiler_params=_cparams(1),
        name="ffn_ln",
    )(x, x, x, w_up, conv_w.astype(F32), conv_b.astype(F32).reshape(1, D_FF), w_down,
      ln_g.astype(F32).reshape(1, D_MODEL), ln_b.astype(F32).reshape(1, D_MODEL))


def _pack_w_in(w):
    widths = (512, 512, 512, GDN_QKV, 512, 8, 8, 512, 128, 128, N_BRANCHES * D_MODEL)
    offs = [0]
    for wd in widths:
        offs.append(offs[-1] + wd)
    seg = lambda i: w[:, offs[i]:offs[i + 1]]
    aq, ak, av, bqkv, bgate, ba, bb, cq, ck, cv, gz = (seg(i) for i in range(len(widths)))
    dup = lambda t: jnp.concatenate([t[:, :HEAD_DIM], t[:, :HEAD_DIM], t[:, HEAD_DIM:], t[:, HEAD_DIM:]], axis=1)
    w_a = jnp.concatenate([aq * DIFF_QSCALE, ak, av, cq, dup(ck), dup(cv)], axis=1)
    w_b = jnp.concatenate([gz, bqkv, bgate], axis=1)
    pad = jnp.zeros((w.shape[0], ZC_COLS - 2 * GDN_CHAINS), w.dtype)
    w_c = jnp.concatenate([ba, bb, pad], axis=1)
    return w_a.astype(BF16), w_b.astype(BF16), w_c.astype(BF16)


def kernel(x, rel_bias, w_in, diff_lambda, diff_subln, gdn_conv, gdn_a_log, gdn_dt_bias, gdn_norm, swa_sink,
           w_branch, w_out, ln1_g, ln1_b, ffn_up, ffn_conv, ffn_conv_b, ffn_down, ln2_g, ln2_b):
    batch, seq, d = x.shape
    assert d == D_MODEL and seq % GDN_GROUP == 0 and seq % (DIFF_QROWS * DIFF_UNROLL) == 0
    n = batch * seq
    tm = 256
    tb_diff = _diff_bias_tiles(rel_bias, seq) * LOG2_E
    tb_swa = _swa_bias_tiles(rel_bias)
    xf = x.reshape(n, d)
    xb = xf.astype(BF16)
    for l in range(DEPTH):
        w_a, w_b, w_c = _pack_w_in(w_in[l])
        za = _matmul(xb, w_a, BF16, 1024, ZA_COLS // 2, "in_proj_a")
        zb = _matmul(xb, w_b, BF16, 1024, ZB_COLS // 2, "in_proj_b")
        zc = _matmul(xb, w_c, F32, 1024, ZC_COLS, "in_proj_c")
        ya = _diff_attention(za, tb_diff, diff_lambda[l], diff_subln[l], batch, seq, l)
        yc = _swa_attention(za, tb_swa, swa_sink[l], batch, seq)
        qkvn = _gdn_conv(zb, gdn_conv[l], batch, seq)
        ab_t = zc[:, 0:2 * GDN_CHAINS].T
        gc, gd, gl, beta = _gdn_gates(ab_t, gdn_a_log[l], gdn_dt_bias[l], batch, seq)
        o_f, o_b = _gdn_fused(qkvn, gc.T, gd.T, beta.T, gc, gl, batch, seq)
        xf = _merge(ya, o_f, o_b, zb, yc, xf, w_branch[l].astype(BF16), w_out[l].astype(BF16),
                    gdn_norm[l], ln1_g[l], ln1_b[l], 2 * tm)
        xf, xb = _ffn(xf, ffn_up[l].astype(BF16), ffn_conv[l], ffn_conv_b[l], ffn_down[l].astype(BF16),
                      ln2_g[l], ln2_b[l], seq, 2 * tm)
    return xf.reshape(batch, seq, d)
```

```python
import functools
import math

import jax
import jax.numpy as jnp
from jax import lax
from jax.experimental import pallas as pl
from jax.experimental.pallas import tpu as pltpu

F32 = jnp.float32
BF16 = jnp.bfloat16

D_MODEL = 1024
DEPTH = 2
HEAD_DIM = 64
DIFF_HEADS = 4
DIFF_V = 2 * HEAD_DIM
Q_BLOCK = 128
GDN_HEADS = 4
GDN_DK = 128
GDN_DV = 128
GDN_QKV = GDN_HEADS * (2 * GDN_DK + GDN_DV)
GDN_CONV = 5
GDN_CHUNK = 64
SWA_HEADS = 8
SWA_KV_HEADS = 2
SWA_WINDOW = 128
SWA_BLOCK = 128
BRANCH_WIDTH = 512
N_BRANCHES = 3
REL_BUCKETS = 32
REL_MAX_DIST = 128
D_FF = 2816
FFN_CONV = 3
DEEPNORM_ALPHA = (2 * DEPTH) ** 0.25
LN_EPS = 1e-5
RMS_EPS = 1e-6

V7X_LANES = 128
V7X_SUBLANES = 8
V7X_VMEM_BYTES = 64 * 1024 * 1024
VMEM_LIMIT = V7X_VMEM_BYTES * 7 // 8

NEG_BIG = -1e30

ZA_AQ, ZA_AK, ZA_AV = 0, 512, 1024
ZA_CQ = 1536
ZA_CK = 2048
ZA_CV = 2304
ZA_COLS = 2560
ZB_GZ = 0
ZB_QKV = 3072
ZB_GATE = 4608
ZB_COLS = 5120
ZC_COLS = 128

GDN_GROUP = 256
GDN_CHAINS = 2 * GDN_HEADS


def _cparams(n_grid):
    return pltpu.CompilerParams(dimension_semantics=("arbitrary",) * n_grid, vmem_limit_bytes=VMEM_LIMIT)


def _t5_bucket(rel):
    nb = REL_BUCKETS // 2
    ret = jnp.where(rel > 0, nb, 0)
    n = jnp.abs(rel)
    max_exact = nb // 2
    large = max_exact + (jnp.log(jnp.maximum(n, 1).astype(jnp.float32) / max_exact)
                         / math.log(REL_MAX_DIST / max_exact) * (nb - max_exact)).astype(jnp.int32)
    large = jnp.minimum(large, nb - 1)
    return ret + jnp.where(n < max_exact, n, large)


def _table_lookup(table, bucket):
    out = jnp.zeros((table.shape[1],) + bucket.shape, F32)
    for i in range(REL_BUCKETS):
        out = jnp.where(bucket[None] == i, table[i].astype(F32).reshape((-1,) + (1,) * bucket.ndim), out)
    return out


def _diff_bias_tiles(rel_bias, seq):
    nq = seq // Q_BLOCK
    d = jnp.arange(2 * nq - 1) - (nq - 1)
    r = jnp.arange(Q_BLOCK)
    rel = d[:, None, None] * Q_BLOCK + r[None, None, :] - r[None, :, None]
    return _table_lookup(rel_bias[:, :DIFF_HEADS], _t5_bucket(rel))


def _swa_bias_tiles(rel_bias):
    kb = 3 * SWA_BLOCK
    rel = jnp.arange(kb)[None, :] - SWA_WINDOW - jnp.arange(SWA_BLOCK)[:, None]
    bias = _table_lookup(rel_bias[:, DIFF_HEADS:], _t5_bucket(rel))
    return jnp.where((jnp.abs(rel) <= SWA_WINDOW)[None], bias, NEG_BIG)


def _matmul_kernel(x_ref, w_ref, o_ref):
    o_ref[...] = jnp.dot(x_ref[...].astype(BF16), w_ref[...],
                         preferred_element_type=F32).astype(o_ref.dtype)


def _matmul(x, w, out_dtype, tm, tn, name):
    m, k = x.shape
    n = w.shape[1]
    return pl.pallas_call(
        _matmul_kernel,
        grid=(n // tn, m // tm),
        in_specs=[pl.BlockSpec((tm, k), lambda j, i: (i, 0)),
                  pl.BlockSpec((k, tn), lambda j, i: (0, j))],
        out_specs=pl.BlockSpec((tm, tn), lambda j, i: (i, j)),
        out_shape=jax.ShapeDtypeStruct((m, n), out_dtype),
        compiler_params=_cparams(2),
        name=name,
    )(x, w)


LOG2_E = math.log2(math.e)
DIFF_QSCALE = HEAD_DIM ** -0.5 * LOG2_E
DIFF_QROWS = 2 * Q_BLOCK
DIFF_UNROLL = 4


def _diff_attn_kernel(q_ref, k_ref, v_ref, tb_ref, lam_ref, subln_ref, o_ref, s_scr, vx_scr, *, nq, lam_init):
    qr = DIFF_QROWS
    vx_scr[:, 0:DIFF_V] = v_ref[...]
    vx_scr[:, DIFF_V:2 * DIFF_V] = jnp.ones((v_ref.shape[0], DIFF_V), vx_scr.dtype)
    lv = lam_ref[...]
    lam = (jnp.exp(jnp.sum(lv[0:1] * lv[1:2], axis=-1, keepdims=True))
           - jnp.exp(jnp.sum(lv[2:3] * lv[3:4], axis=-1, keepdims=True)) + lam_init)
    lane = lax.broadcasted_iota(jnp.int32, (qr, 128), 1)
    zero = jnp.zeros((qr, 128), q_ref.dtype)

    def scores(qb, scr):
        r0 = pl.multiple_of(qb * qr, qr)
        q = q_ref[pl.ds(r0, qr), :]
        qz = jnp.concatenate([jnp.where(lane < HEAD_DIM, q, zero),
                              jnp.where(lane >= HEAD_DIM, q, zero)], axis=0)
        for kp in range(nq // 2):
            kblk = k_ref[kp * qr:(kp + 1) * qr, :]
            s = lax.dot_general(qz, kblk, (((1,), (1,)), ((), ())), preferred_element_type=F32)
            for a in range(2):
                d0 = nq - 1 - (2 * qb + a) + 2 * kp
                bias = jnp.concatenate([tb_ref[0, d0], tb_ref[0, d0 + 1]], axis=1)
                for c in range(2):
                    rows = slice(c * qr + a * Q_BLOCK, c * qr + (a + 1) * Q_BLOCK)
                    scr[rows, kp * qr:(kp + 1) * qr] = s[rows] + bias

    def softmax(scr):
        s = scr[...]
        return jnp.exp2(s - jnp.max(s, axis=-1, keepdims=True)).astype(BF16)

    def finish(qb, e):
        r0 = pl.multiple_of(qb * qr, qr)
        pv = jnp.dot(e, vx_scr[...], preferred_element_type=F32)
        pv = pv[:, 0:DIFF_V] / pv[:, DIFF_V:2 * DIFF_V]
        o = pv[:qr] - lam * pv[qr:]
        o = o * lax.rsqrt(jnp.mean(o * o, axis=-1, keepdims=True) + RMS_EPS) * subln_ref[...]
        o_ref[pl.ds(r0, qr), :] = (o * (1.0 - lam_init)).astype(o_ref.dtype)

    def body(i, carry):
        blocks = [i * DIFF_UNROLL + j for j in range(DIFF_UNROLL)]
        for j, qb in enumerate(blocks):
            scores(qb, s_scr.at[j])
        probs = [softmax(s_scr.at[j]) for j in range(DIFF_UNROLL)]
        for qb, e in zip(blocks, probs):
            finish(qb, e)
        return carry

    lax.fori_loop(0, nq // (2 * DIFF_UNROLL), body, 0)


def _diff_attention(za, tb, lam_vecs, subln, batch, seq, layer_idx):
    nq = seq // Q_BLOCK
    lam_init = 0.8 - 0.6 * math.exp(-0.3 * layer_idx)
    kern = functools.partial(_diff_attn_kernel, nq=nq, lam_init=lam_init)
    qcol, kcol, vcol = ZA_AQ // 128, ZA_AK // 128, ZA_AV // 128
    return pl.pallas_call(
        kern,
        grid=(batch, DIFF_HEADS),
        in_specs=[
            pl.BlockSpec((seq, 128), lambda b, h: (b, qcol + h)),
            pl.BlockSpec((seq, 128), lambda b, h: (b, kcol + h)),
            pl.BlockSpec((seq, 128), lambda b, h: (b, vcol + h)),
            pl.BlockSpec((1, 2 * nq - 1, Q_BLOCK, Q_BLOCK), lambda b, h: (h, 0, 0, 0)),
            pl.BlockSpec((4, HEAD_DIM), lambda b, h: (0, 0)),
            pl.BlockSpec((1, DIFF_V), lambda b, h: (0, 0)),
        ],
        out_specs=pl.BlockSpec((seq, 128), lambda b, h: (b, h)),
        out_shape=jax.ShapeDtypeStruct((batch * seq, BRANCH_WIDTH), BF16),
        scratch_shapes=[pltpu.VMEM((DIFF_UNROLL, 2 * DIFF_QROWS, seq), F32),
                        pltpu.VMEM((seq, 2 * DIFF_V), BF16)],
        compiler_params=_cparams(2),
        name="diff_attention",
    )(za, za, za, tb, lam_vecs.astype(F32), subln.astype(F32).reshape(1, DIFF_V))


def _swa_kernel(q_ref, k_ref, v_ref, bias_ref, sink_ref, o_ref, kp_scr, vp_scr, *, nb):
    seq = q_ref.shape[0]
    blk = SWA_BLOCK
    pad = jnp.zeros((blk, kp_scr.shape[1]), kp_scr.dtype)
    for scr, ref in ((kp_scr, k_ref), (vp_scr, v_ref)):
        scr[0:blk] = pad
        scr[blk + seq:2 * blk + seq] = pad
        scr[blk:blk + seq] = ref[...]
    krow = lax.broadcasted_iota(jnp.int32, (3 * blk, blk), 0)
    lane = lax.broadcasted_iota(jnp.int32, (blk, 128), 1)
    low = lane < HEAD_DIM
    zero = jnp.zeros((blk, 128), q_ref.dtype)
    scale = HEAD_DIM ** -0.5
    rep = SWA_HEADS // SWA_KV_HEADS

    def body(n, carry):
        r0 = pl.multiple_of(n * blk, blk)
        q = q_ref[pl.ds(r0, blk), :]
        kw = kp_scr[pl.ds(r0, 3 * blk), :]
        vw = vp_scr[pl.ds(r0, 3 * blk), :]
        outside = ((krow < blk) & (n == 0)) | ((krow >= 2 * blk) & (n == nb - 1))
        heads = range(SWA_HEADS)
        qms = [jnp.where(low if hd % 2 == 0 else jnp.logical_not(low),
                         q[:, (hd // 2) * 128:(hd // 2 + 1) * 128] * scale, zero) for hd in heads]
        ss = [lax.dot_general(kw[:, (hd // rep) * 128:(hd // rep + 1) * 128], qms[hd], (((1,), (1,)), ((), ())),
                              preferred_element_type=F32) for hd in heads]
        ss = [jnp.where(outside, NEG_BIG, ss[hd] + bias_ref[hd]) for hd in heads]
        sinks = [sink_ref[hd][:, 0:1] for hd in heads]
        ms = [jnp.maximum(jnp.max(ss[hd], axis=0, keepdims=True), sinks[hd]) for hd in heads]
        es = [jnp.exp(ss[hd] - ms[hd]) for hd in heads]
        dens = [jnp.sum(es[hd], axis=0, keepdims=True) + jnp.exp(sinks[hd] - ms[hd]) for hd in heads]
        ps = [(es[hd] / dens[hd]).astype(BF16) for hd in heads]
        halves = [lax.dot_general(ps[hd], vw[:, (hd // rep) * 128:(hd // rep + 1) * 128], (((0,), (0,)), ((), ())),
                                  preferred_element_type=F32) for hd in heads]
        o = jnp.concatenate([jnp.where(low, halves[2 * c], halves[2 * c + 1]) for c in range(SWA_HEADS // 2)], axis=1)
        o_ref[pl.ds(r0, blk), :] = o.astype(o_ref.dtype)
        return carry

    lax.fori_loop(0, nb, body, 0)


def _swa_attention(za, bias, sink, batch, seq):
    nb = seq // SWA_BLOCK
    kern = functools.partial(_swa_kernel, nb=nb)
    sink_b = jnp.broadcast_to(sink.astype(F32).reshape(SWA_HEADS, 1, 1), (SWA_HEADS, 1, 128))
    kvw = 2 * SWA_KV_HEADS * HEAD_DIM
    return pl.pallas_call(
        kern,
        grid=(batch,),
        in_specs=[
            pl.BlockSpec((seq, BRANCH_WIDTH), lambda b: (b, ZA_CQ // BRANCH_WIDTH)),
            pl.BlockSpec((seq, kvw), lambda b: (b, ZA_CK // kvw)),
            pl.BlockSpec((seq, kvw), lambda b: (b, ZA_CV // kvw)),
            pl.BlockSpec((SWA_HEADS, 3 * SWA_BLOCK, SWA_BLOCK), lambda b: (0, 0, 0)),
            pl.BlockSpec((SWA_HEADS, 1, 128), lambda b: (0, 0, 0)),
        ],
        out_specs=pl.BlockSpec((seq, BRANCH_WIDTH), lambda b: (b, 0)),
        out_shape=jax.ShapeDtypeStruct((batch * seq, BRANCH_WIDTH), BF16),
        scratch_shapes=[pltpu.VMEM((seq + 2 * SWA_BLOCK, kvw), BF16),
                        pltpu.VMEM((seq + 2 * SWA_BLOCK, kvw), BF16)],
        compiler_params=_cparams(1),
        name="swa_attention",
    )(za, za, za, bias.transpose(0, 2, 1), sink_b)


def _shift_rows(x, d):
    rows = x.shape[0]
    t = lax.broadcasted_iota(jnp.int32, x.shape, 0)
    rolled = pltpu.roll(x, (-d) % rows, 0)
    return jnp.where((t + d >= 0) & (t + d < rows), rolled, 0.0)


def _gdn_conv_kernel(x_ref, w_ref, o_ref):
    j = pl.program_id(1)
    x = x_ref[...].astype(F32)
    w = w_ref[...]
    half = GDN_CONV // 2
    acc = x * w[half:half + 1]
    for tap in range(GDN_CONV):
        if tap != half:
            acc = acc + _shift_rows(x, tap - half) * w[tap:tap + 1]
    h = acc * jax.nn.sigmoid(acc)
    inv = lax.rsqrt(jnp.sum(h * h, axis=-1, keepdims=True) + RMS_EPS)
    factor = jnp.where(j < 2 * GDN_HEADS, inv, 1.0) * jnp.where(j < GDN_HEADS, GDN_DK ** -0.5, 1.0)
    o_ref[...] = h * factor


def _gdn_conv(zb, conv_w, batch, seq):
    c0 = ZB_QKV // 128
    return pl.pallas_call(
        _gdn_conv_kernel,
        grid=(batch, GDN_QKV // 128),
        in_specs=[pl.BlockSpec((seq, 128), lambda b, j: (b, c0 + j)),
                  pl.BlockSpec((GDN_CONV, 128), lambda b, j: (0, j))],
        out_specs=pl.BlockSpec((seq, 128), lambda b, j: (b, j)),
        out_shape=jax.ShapeDtypeStruct((batch * seq, GDN_QKV), F32),
        compiler_params=_cparams(2),
        name="gdn_conv",
    )(zb, conv_w.astype(F32))


def _gdn_gate_kernel(ab_ref, alog_ref, dtb_ref, gc_ref, gd_ref, gl_ref, beta_ref):
    ab = ab_ref[...]
    a = ab[0:GDN_CHAINS]
    seq = a.shape[1]
    xs = a + dtb_ref[...]
    softplus = jnp.maximum(xs, 0.0) + jnp.log1p(jnp.exp(-jnp.abs(xs)))
    g = -jnp.exp(alog_ref[...]) * softplus
    beta_ref[...] = jax.nn.sigmoid(ab[GDN_CHAINS:2 * GDN_CHAINS])
    pos = lax.broadcasted_iota(jnp.int32, g.shape, 1) % GDN_CHUNK
    cf = g
    cr = g
    s = 1
    while s < GDN_CHUNK:
        cf = cf + jnp.where(pos >= s, pltpu.roll(cf, s, 1), 0.0)
        cr = cr + jnp.where(pos < GDN_CHUNK - s, pltpu.roll(cr, seq - s, 1), 0.0)
        s *= 2
    fwd = lax.broadcasted_iota(jnp.int32, g.shape, 0) < GDN_HEADS
    gc_ref[...] = jnp.where(fwd, cf, cr)
    gd_ref[...] = jnp.where(fwd, cr, cf) - g
    gl_ref[...] = cf + cr - g


def _gdn_gates(ab_t, a_log, dt_bias, batch, seq):
    row = pl.BlockSpec((GDN_CHAINS, seq), lambda b: (0, b))
    shp = jax.ShapeDtypeStruct((GDN_CHAINS, batch * seq), F32)
    par = pl.BlockSpec((GDN_CHAINS, 1), lambda b: (0, 0))
    return pl.pallas_call(
        _gdn_gate_kernel,
        grid=(batch,),
        in_specs=[pl.BlockSpec((2 * GDN_CHAINS, seq), lambda b: (0, b)), par, par],
        out_specs=[row, row, row, row],
        out_shape=[shp, shp, shp, shp],
        compiler_params=_cparams(1),
        name="gdn_gates",
    )(ab_t, a_log.astype(F32).reshape(GDN_CHAINS, 1), dt_bias.astype(F32).reshape(GDN_CHAINS, 1))


def _hdot(a, b):
    return jnp.dot(a.astype(BF16), b.astype(BF16), preferred_element_type=F32)


def _gdn_fused_kernel(qkvf_ref, qkvb_ref, gccf_ref, gccb_ref, gdcf_ref, gdcb_ref, betaf_ref, betab_ref,
                      gcrf_ref, gcrb_ref, glf_ref, glb_ref, of_ref, ob_ref, s_scr):
    @pl.when(pl.program_id(1) == 0)
    def _():
        s_scr[...] = jnp.zeros_like(s_scr)

    g = GDN_GROUP
    c = GDN_CHUNK
    nc = g // c
    ri = lax.broadcasted_iota(jnp.int32, (g, g), 0)
    ci = lax.broadcasted_iota(jnp.int32, (g, g), 1)
    same = (ri // c) == (ci // c)
    ahead = jnp.where(same, ri - ci, -g)
    behind = jnp.where(same, ci - ri, -g)
    eye_s = (lax.broadcasted_iota(jnp.int32, (c, g), 0)
             == lax.broadcasted_iota(jnp.int32, (c, g), 1) % c).astype(F32)

    def block_diag(xs):
        return jnp.where(same, jnp.concatenate([xs] * nc, axis=0), 0.0)

    def row_blocks_sum(xd):
        out = xd[0:c]
        for i in range(1, nc):
            out = out + xd[i * c:(i + 1) * c]
        return out

    dirs = ((qkvf_ref, gccf_ref, gdcf_ref, betaf_ref, gcrf_ref, glf_ref, of_ref, ahead),
            (qkvb_ref, gccb_ref, gdcb_ref, betab_ref, gcrb_ref, glb_ref, ob_ref, behind))
    chains = [(d, h) for d in range(2) for h in range(GDN_HEADS)]
    n_ch = len(chains)
    xs_l, xd_l, p_l, rhs_l, qg_l, qk_l, kgt_l = [], [], [], [], [], [], []
    for n, (d, h) in enumerate(chains):
        qkv_ref, gcc_ref, gdc_ref, beta_ref, gcr_ref = dirs[d][0:5]
        order = dirs[d][7]
        q = qkv_ref[:, h * GDN_DK:(h + 1) * GDN_DK]
        k = qkv_ref[:, (GDN_HEADS + h) * GDN_DK:(GDN_HEADS + h + 1) * GDN_DK]
        v = qkv_ref[:, (2 * GDN_HEADS + h) * GDN_DK:(2 * GDN_HEADS + h + 1) * GDN_DK]
        kbf = k.astype(BF16)
        qk_raw = lax.dot_general(q.astype(BF16), kbf, (((1,), (1,)), ((), ())), preferred_element_type=F32)
        gcc = gcc_ref[:, n:n + 1]
        gdc = gdc_ref[:, n:n + 1]
        beta = beta_ref[:, n:n + 1]
        gcr = gcr_ref[n:n + 1, :]
        kb = k * beta
        kk = lax.dot_general(kb.astype(BF16), kbf, (((1,), (1,)), ((), ())), preferred_element_type=F32)
        decay = jnp.exp(jnp.where(order >= 0, gcc - gcr, NEG_BIG))
        xd = -jnp.where(order > 0, kk * decay, 0.0)
        qkm = qk_raw * decay
        egc = jnp.exp(gcc)
        qg_l.append((q * egc).astype(BF16))
        qk_l.append((qkm[:, 0:128] + qkm[:, 128:256]).astype(BF16))
        kgt_l.append((k * jnp.exp(gdc)).T.astype(BF16))
        xs = row_blocks_sum(xd)
        xs_l.append(xs)
        xd_l.append(xd)
        p_l.append(eye_s + xs)
        rhs_l.append(jnp.concatenate([v * beta, kb * egc], axis=1).astype(BF16))
    xs_l = [_hdot(xs_l[i], xd_l[i]) for i in range(n_ch)]
    for _ in range(4):
        r_l = [_hdot(jnp.concatenate([p_l[i], xs_l[i]], axis=0), block_diag(xs_l[i])) for i in range(n_ch)]
        p_l = [p_l[i] + r_l[i][0:c] for i in range(n_ch)]
        xs_l = [r_l[i][c:2 * c] for i in range(n_ch)]
    p_l = [p_l[i] + _hdot(p_l[i], block_diag(xs_l[i])) for i in range(n_ch)]
    sol_l = [_hdot(block_diag(p_l[i]), rhs_l[i]) for i in range(n_ch)]
    u_l = [sol[:, 0:GDN_DV] for sol in sol_l]
    w_l = [sol[:, GDN_DV:].astype(BF16) for sol in sol_l]

    zeros = jnp.zeros((c, GDN_DV), BF16)
    states = [s_scr[n] for n in range(n_ch)]
    for step in range(nc):
        tops, v2s = [], []
        for n, (d, h) in enumerate(chains):
            i = step if d == 0 else nc - 1 - step
            rows = slice(i * c, (i + 1) * c)
            sb = states[n].astype(BF16)
            top = jnp.dot(jnp.concatenate([w_l[n][rows], qg_l[n][rows]], axis=0), sb, preferred_element_type=F32)
            vb = (u_l[n][rows] - top[0:c]).astype(BF16)
            v2s.append(jnp.concatenate([vb, zeros] if i % 2 == 0 else [zeros, vb], axis=0))
            tops.append(top)
        for n, (d, h) in enumerate(chains):
            gl_ref, o_ref = dirs[d][5:7]
            i = step if d == 0 else nc - 1 - step
            rows = slice(i * c, (i + 1) * c)
            pair = slice((i // 2) * 2 * c, (i // 2 + 1) * 2 * c)
            bot = jnp.dot(jnp.concatenate([qk_l[n][rows], kgt_l[n][:, pair]], axis=0), v2s[n],
                          preferred_element_type=F32)
            o_ref[rows, h * GDN_DV:(h + 1) * GDN_DV] = tops[n][c:2 * c] + bot[0:c]
            decay = jnp.exp(gl_ref[n:n + 1, i * c:i * c + 1])
            states[n] = states[n] * decay + bot[c:c + GDN_DK]
    for n in range(n_ch):
        s_scr[n] = states[n]


def _gdn_fused(qkvn, gcc, gdc, betac, gcr, gl, batch, seq):
    ng = seq // GDN_GROUP
    half = GDN_HEADS * GDN_DV
    n = batch * seq
    fwd = lambda b, s: (b * ng + s, 0)
    bwd = lambda b, s: (b * ng + ng - 1 - s, 0)
    fwd_t = lambda b, s: (0, b * ng + s)
    bwd_t = lambda b, s: (0, b * ng + ng - 1 - s)
    both = lambda shape, f, r: [pl.BlockSpec(shape, f), pl.BlockSpec(shape, r)]
    col = (GDN_GROUP, GDN_CHAINS)
    row = (GDN_CHAINS, GDN_GROUP)
    return pl.pallas_call(
        _gdn_fused_kernel,
        grid=(batch, ng),
        in_specs=(both((GDN_GROUP, GDN_QKV), fwd, bwd) + both(col, fwd, bwd) + both(col, fwd, bwd)
                  + both(col, fwd, bwd) + both(row, fwd_t, bwd_t) + both(row, fwd_t, bwd_t)),
        out_specs=both((GDN_GROUP, half), fwd, bwd),
        out_shape=[jax.ShapeDtypeStruct((n, half), F32), jax.ShapeDtypeStruct((n, half), F32)],
        scratch_shapes=[pltpu.VMEM((GDN_CHAINS, GDN_DK, GDN_DV), F32)],
        compiler_params=_cparams(2),
        name="gdn_fused",
    )(qkvn, qkvn, gcc, gcc, gdc, gdc, betac, betac, gcr, gcr, gl, gl)


def _layer_norm(r, g, b):
    mu = jnp.mean(r, axis=-1, keepdims=True)
    var = jnp.mean(jnp.square(r - mu), axis=-1, keepdims=True)
    return (r - mu) * lax.rsqrt(var + LN_EPS) * g + b


def _merge_kernel(ya_ref, of_ref, ob_ref, gate_ref, yc_ref, gz_ref, x_ref, wb_ref, wo_ref,
                  nw_ref, lng_ref, lnb_ref, o_ref):
    segs = []
    for h in range(GDN_HEADS):
        sl = slice(h * GDN_DV, (h + 1) * GDN_DV)
        o = of_ref[:, sl] + ob_ref[:, sl]
        o = o * lax.rsqrt(jnp.mean(o * o, axis=-1, keepdims=True) + RMS_EPS) * nw_ref[...]
        gt = gate_ref[:, sl].astype(F32)
        segs.append(o * (gt * jax.nn.sigmoid(gt)))
    yb = jnp.concatenate(segs, axis=1).astype(BF16)
    merged = None
    for n, y in enumerate((ya_ref[...], yb, yc_ref[...])):
        proj = jnp.dot(y, wb_ref[n], preferred_element_type=F32)
        term = jax.nn.sigmoid(gz_ref[:, n * D_MODEL:(n + 1) * D_MODEL].astype(F32)) * proj
        merged = term if merged is None else merged + term
    hmix = jnp.dot(merged.astype(BF16), wo_ref[...], preferred_element_type=F32)
    o_ref[...] = _layer_norm(DEEPNORM_ALPHA * x_ref[...] + hmix, lng_ref[...], lnb_ref[...])


def _merge(ya, o_f, o_b, zb, yc, x, w_branch, w_out, norm_w, ln_g, ln_b, tm):
    n = x.shape[0]
    row = lambda width, col=0: pl.BlockSpec((tm, width), lambda i, col=col: (i, col))
    full = lambda shape: pl.BlockSpec(shape, lambda i: (0,) * len(shape), pipeline_mode=pl.Buffered(1))
    return pl.pallas_call(
        _merge_kernel,
        grid=(n // tm,),
        in_specs=[row(BRANCH_WIDTH), row(BRANCH_WIDTH), row(BRANCH_WIDTH),
                  row(BRANCH_WIDTH, ZB_GATE // BRANCH_WIDTH), row(BRANCH_WIDTH),
                  row(N_BRANCHES * D_MODEL, ZB_GZ // (N_BRANCHES * D_MODEL)), row(D_MODEL),
                  full((N_BRANCHES, BRANCH_WIDTH, D_MODEL)), full((D_MODEL, D_MODEL)),
                  full((1, GDN_DV)), full((1, D_MODEL)), full((1, D_MODEL))],
        out_specs=row(D_MODEL),
        out_shape=jax.ShapeDtypeStruct((n, D_MODEL), F32),
        compiler_params=_cparams(1),
        name="merge_ln",
    )(ya, o_f, o_b, zb, yc, zb, x, w_branch, w_out, norm_w.astype(F32).reshape(1, GDN_DV),
      ln_g.astype(F32).reshape(1, D_MODEL), ln_b.astype(F32).reshape(1, D_MODEL))


FFN_SPLIT = 2


def _ffn_kernel(xp_ref, x_ref, xn_ref, wup_ref, cw_ref, cb_ref, wdn_ref, lng_ref, lnb_ref, o_ref, ob_ref, *,
                tiles_per_seq):
    i = pl.program_id(0)
    tm = x_ref.shape[0]
    halo = V7X_SUBLANES
    x = x_ref[...]
    xe = jnp.concatenate([xp_ref[...], x, xn_ref[...]], axis=0).astype(BF16)
    xb = x.astype(BF16)
    rows = lax.broadcasted_iota(jnp.int32, (tm + 2 * halo, 1), 0)
    first = (i % tiles_per_seq) == 0
    last = (i % tiles_per_seq) == tiles_per_seq - 1
    keep = jnp.logical_not(((rows < halo) & first) | ((rows >= tm + halo) & last))
    fc = D_FF // FFN_SPLIT
    ext = tm + 2 * halo
    acc = None
    for c in range(FFN_SPLIT):
        ge = jnp.dot(xe, wup_ref[:, c * fc:(c + 1) * fc], preferred_element_type=F32)
        ge = jnp.where(keep, ge, 0.0)
        up = jnp.dot(xb, wup_ref[:, D_FF + c * fc:D_FF + (c + 1) * fc], preferred_element_type=F32)
        cw = cw_ref[:, c * fc:(c + 1) * fc]
        conv = (pltpu.roll(ge, 1, 0) * cw[0:1] + ge * cw[1:2] + pltpu.roll(ge, ext - 1, 0) * cw[2:3])
        gate = conv[halo:halo + tm] + cb_ref[:, c * fc:(c + 1) * fc]
        act = (gate * jax.nn.sigmoid(gate) * up).astype(BF16)
        part = jnp.dot(act, wdn_ref[c * fc:(c + 1) * fc, :], preferred_element_type=F32)
        acc = part if acc is None else acc + part
    y = _layer_norm(DEEPNORM_ALPHA * x + acc, lng_ref[...], lnb_ref[...])
    o_ref[...] = y
    ob_ref[...] = y.astype(ob_ref.dtype)


def _ffn(x, w_up, conv_w, conv_b, w_down, ln_g, ln_b, seq, tm):
    n = x.shape[0]
    halo = V7X_SUBLANES
    per = tm // halo
    nh = n // halo
    full = lambda shape: pl.BlockSpec(shape, lambda i: (0,) * len(shape), pipeline_mode=pl.Buffered(1))
    kern = functools.partial(_ffn_kernel, tiles_per_seq=seq // tm)
    return pl.pallas_call(
        kern,
        grid=(n // tm,),
        in_specs=[pl.BlockSpec((halo, D_MODEL), lambda i: (jnp.maximum(i * per - 1, 0), 0)),
                  pl.BlockSpec((tm, D_MODEL), lambda i: (i, 0)),
                  pl.BlockSpec((halo, D_MODEL), lambda i: (jnp.minimum((i + 1) * per, nh - 1), 0)),
                  full((D_MODEL, 2 * D_FF)), full((FFN_CONV, D_FF)), full((1, D_FF)),
                  full((D_FF, D_MODEL)), full((1, D_MODEL)), full((1, D_MODEL))],
        out_specs=[pl.BlockSpec((tm, D_MODEL), lambda i: (i, 0)), pl.BlockSpec((tm, D_MODEL), lambda i: (i, 0))],
        out_shape=[jax.ShapeDtypeStruct((n, D_MODEL), F32), jax.ShapeDtypeStruct((n, D_MODEL), BF16)],
        compiler_params=_cparams(1),
        name="ffn_ln",
    )(x, x, x, w_up, conv_w.astype(F32), conv_b.astype(F32).reshape(1, D_FF), w_down,
      ln_g.astype(F32).reshape(1, D_MODEL), ln_b.astype(F32).reshape(1, D_MODEL))


def _pack_w_in(w):
    widths = (512, 512, 512, GDN_QKV, 512, 8, 8, 512, 128, 128, N_BRANCHES * D_MODEL)
    offs = [0]
    for wd in widths:
        offs.append(offs[-1] + wd)
    seg = lambda i: w[:, offs[i]:offs[i + 1]]
    aq, ak, av, bqkv, bgate, ba, bb, cq, ck, cv, gz = (seg(i) for i in range(len(widths)))
    dup = lambda t: jnp.concatenate([t[:, :HEAD_DIM], t[:, :HEAD_DIM], t[:, HEAD_DIM:], t[:, HEAD_DIM:]], axis=1)
    w_a = jnp.concatenate([aq * DIFF_QSCALE, ak, av, cq, dup(ck), dup(cv)], axis=1)
    w_b = jnp.concatenate([gz, bqkv, bgate], axis=1)
    pad = jnp.zeros((w.shape[0], ZC_COLS - 2 * GDN_CHAINS), w.dtype)
    w_c = jnp.concatenate([ba, bb, pad], axis=1)
    return w_a.astype(BF16), w_b.astype(BF16), w_c.astype(BF16)


def kernel(x, rel_bias, w_in, diff_lambda, diff_subln, gdn_conv, gdn_a_log, gdn_dt_bias, gdn_norm, swa_sink,
           w_branch, w_out, ln1_g, ln1_b, ffn_up, ffn_conv, ffn_conv_b, ffn_down, ln2_g, ln2_b):
    batch, seq, d = x.shape
    assert d == D_MODEL and seq % GDN_GROUP == 0 and seq % (DIFF_QROWS * DIFF_UNROLL) == 0
    n = batch * seq
    tm = 256
    tb_diff = _diff_bias_tiles(rel_bias, seq) * LOG2_E
    tb_swa = _swa_bias_tiles(rel_bias)
    xf = x.reshape(n, d)
    xb = xf.astype(BF16)
    for l in range(DEPTH):
        w_a, w_b, w_c = _pack_w_in(w_in[l])
        za = _matmul(xb, w_a, BF16, 1024, ZA_COLS // 2, "in_proj_a")
        zb = _matmul(xb, w_b, BF16, 1024, ZB_COLS // 2, "in_proj_b")
        zc = _matmul(xb, w_c, F32, 1024, ZC_COLS, "in_proj_c")
        ya = _diff_attention(za, tb_diff, diff_lambda[l], diff_subln[l], batch, seq, l)
        yc = _swa_attention(za, tb_swa, swa_sink[l], batch, seq)
        qkvn = _gdn_conv(zb, gdn_conv[l], batch, seq)
        ab_t = zc[:, 0:2 * GDN_CHAINS].T
        gc, gd, gl, beta = _gdn_gates(ab_t, gdn_a_log[l], gdn_dt_bias[l], batch, seq)
        o_f, o_b = _gdn_fused(qkvn, gc.T, gd.T, beta.T, gc, gl, batch, seq)
        xf = _merge(ya, o_f, o_b, zb, yc, xf, w_branch[l].astype(BF16), w_out[l].astype(BF16),
                    gdn_norm[l], ln1_g[l], ln1_b[l], 2 * tm)
        xf, xb = _ffn(xf, ffn_up[l].astype(BF16), ffn_conv[l], ffn_conv_b[l], ffn_down[l].astype(BF16),
                      ln2_g[l], ln2_b[l], seq, 2 * tm)
    return xf.reshape(batch, seq, d)
```

```python
import functools
import math

import jax
import jax.numpy as jnp
from jax import lax
from jax.experimental import pallas as pl
from jax.experimental.pallas import tpu as pltpu

F32 = jnp.float32
BF16 = jnp.bfloat16

D_MODEL = 1024
DEPTH = 2
HEAD_DIM = 64
DIFF_HEADS = 4
DIFF_V = 2 * HEAD_DIM
Q_BLOCK = 128
GDN_HEADS = 4
GDN_DK = 128
GDN_DV = 128
GDN_QKV = GDN_HEADS * (2 * GDN_DK + GDN_DV)
GDN_CONV = 5
GDN_CHUNK = 64
SWA_HEADS = 8
SWA_KV_HEADS = 2
SWA_WINDOW = 128
SWA_BLOCK = 128
BRANCH_WIDTH = 512
N_BRANCHES = 3
REL_BUCKETS = 32
REL_MAX_DIST = 128
D_FF = 2816
FFN_CONV = 3
DEEPNORM_ALPHA = (2 * DEPTH) ** 0.25
LN_EPS = 1e-5
RMS_EPS = 1e-6

V7X_LANES = 128
V7X_SUBLANES = 8
V7X_VMEM_BYTES = 64 * 1024 * 1024
VMEM_LIMIT = V7X_VMEM_BYTES * 7 // 8

NEG_BIG = -1e30

ZA_AQ, ZA_AK, ZA_AV = 0, 512, 1024
ZA_CQ = 1536
ZA_CK = 2048
ZA_CV = 2304
ZA_COLS = 2560
ZB_GZ = 0
ZB_QKV = 3072
ZB_GATE = 4608
ZB_COLS = 5120
ZC_COLS = 128

GDN_GROUP = 256
GDN_CHAINS = 2 * GDN_HEADS


def _cparams(n_grid):
    return pltpu.CompilerParams(dimension_semantics=("arbitrary",) * n_grid, vmem_limit_bytes=VMEM_LIMIT)


def _t5_bucket(rel):
    nb = REL_BUCKETS // 2
    ret = jnp.where(rel > 0, nb, 0)
    n = jnp.abs(rel)
    max_exact = nb // 2
    large = max_exact + (jnp.log(jnp.maximum(n, 1).astype(jnp.float32) / max_exact)
                         / math.log(REL_MAX_DIST / max_exact) * (nb - max_exact)).astype(jnp.int32)
    large = jnp.minimum(large, nb - 1)
    return ret + jnp.where(n < max_exact, n, large)


def _table_lookup(table, bucket):
    out = jnp.zeros((table.shape[1],) + bucket.shape, F32)
    for i in range(REL_BUCKETS):
        out = jnp.where(bucket[None] == i, table[i].astype(F32).reshape((-1,) + (1,) * bucket.ndim), out)
    return out


def _diff_bias_tiles(rel_bias, seq):
    nq = seq // Q_BLOCK
    d = jnp.arange(2 * nq - 1) - (nq - 1)
    r = jnp.arange(Q_BLOCK)
    rel = d[:, None, None] * Q_BLOCK + r[None, None, :] - r[None, :, None]
    return _table_lookup(rel_bias[:, :DIFF_HEADS], _t5_bucket(rel))


def _swa_bias_tiles(rel_bias):
    kb = 3 * SWA_BLOCK
    rel = jnp.arange(kb)[None, :] - SWA_WINDOW - jnp.arange(SWA_BLOCK)[:, None]
    bias = _table_lookup(rel_bias[:, DIFF_HEADS:], _t5_bucket(rel))
    return jnp.where((jnp.abs(rel) <= SWA_WINDOW)[None], bias, NEG_BIG)


def _matmul_kernel(x_ref, w_ref, o_ref):
    o_ref[...] = jnp.dot(x_ref[...].astype(BF16), w_ref[...],
                         preferred_element_type=F32).astype(o_ref.dtype)


def _matmul(x, w, out_dtype, tm, tn, name):
    m, k = x.shape
    n = w.shape[1]
    return pl.pallas_call(
        _matmul_kernel,
        grid=(n // tn, m // tm),
        in_specs=[pl.BlockSpec((tm, k), lambda j, i: (i, 0)),
                  pl.BlockSpec((k, tn), lambda j, i: (0, j))],
        out_specs=pl.BlockSpec((tm, tn), lambda j, i: (i, j)),
        out_shape=jax.ShapeDtypeStruct((m, n), out_dtype),
        compiler_params=_cparams(2),
        name=name,
    )(x, w)


LOG2_E = math.log2(math.e)
DIFF_QSCALE = HEAD_DIM ** -0.5 * LOG2_E
DIFF_QROWS = 2 * Q_BLOCK
DIFF_UNROLL = 4


def _diff_attn_kernel(q_ref, k_ref, v_ref, tb_ref, lam_ref, subln_ref, o_ref, s_scr, vx_scr, *, nq, lam_init):
    qr = DIFF_QROWS
    vx_scr[:, 0:DIFF_V] = v_ref[...]
    vx_scr[:, DIFF_V:2 * DIFF_V] = jnp.ones((v_ref.shape[0], DIFF_V), vx_scr.dtype)
    lv = lam_ref[...]
    lam = (jnp.exp(jnp.sum(lv[0:1] * lv[1:2], axis=-1, keepdims=True))
           - jnp.exp(jnp.sum(lv[2:3] * lv[3:4], axis=-1, keepdims=True)) + lam_init)
    lane = lax.broadcasted_iota(jnp.int32, (qr, 128), 1)
    zero = jnp.zeros((qr, 128), q_ref.dtype)

    def scores(qb, scr):
        r0 = pl.multiple_of(qb * qr, qr)
        q = q_ref[pl.ds(r0, qr), :]
        qz = jnp.concatenate([jnp.where(lane < HEAD_DIM, q, zero),
                              jnp.where(lane >= HEAD_DIM, q, zero)], axis=0)
        for kp in range(nq // 2):
            kblk = k_ref[kp * qr:(kp + 1) * qr, :]
            s = lax.dot_general(qz, kblk, (((1,), (1,)), ((), ())), preferred_element_type=F32)
            for a in range(2):
                d0 = nq - 1 - (2 * qb + a) + 2 * kp
                bias = jnp.concatenate([tb_ref[0, d0], tb_ref[0, d0 + 1]], axis=1)
                for c in range(2):
                    rows = slice(c * qr + a * Q_BLOCK, c * qr + (a + 1) * Q_BLOCK)
                    scr[rows, kp * qr:(kp + 1) * qr] = s[rows] + bias

    def softmax(scr):
        s = scr[...]
        return jnp.exp2(s - jnp.max(s, axis=-1, keepdims=True)).astype(BF16)

    def finish(qb, e):
        r0 = pl.multiple_of(qb * qr, qr)
        pv = jnp.dot(e, vx_scr[...], preferred_element_type=F32)
        pv = pv[:, 0:DIFF_V] / pv[:, DIFF_V:2 * DIFF_V]
        o = pv[:qr] - lam * pv[qr:]
        o = o * lax.rsqrt(jnp.mean(o * o, axis=-1, keepdims=True) + RMS_EPS) * subln_ref[...]
        o_ref[pl.ds(r0, qr), :] = (o * (1.0 - lam_init)).astype(o_ref.dtype)

    def body(i, carry):
        blocks = [i * DIFF_UNROLL + j for j in range(DIFF_UNROLL)]
        for j, qb in enumerate(blocks):
            scores(qb, s_scr.at[j])
        probs = [softmax(s_scr.at[j]) for j in range(DIFF_UNROLL)]
        for qb, e in zip(blocks, probs):
            finish(qb, e)
        return carry

    lax.fori_loop(0, nq // (2 * DIFF_UNROLL), body, 0)


def _diff_attention(za, tb, lam_vecs, subln, batch, seq, layer_idx):
    nq = seq // Q_BLOCK
    lam_init = 0.8 - 0.6 * math.exp(-0.3 * layer_idx)
    kern = functools.partial(_diff_attn_kernel, nq=nq, lam_init=lam_init)
    qcol, kcol, vcol = ZA_AQ // 128, ZA_AK // 128, ZA_AV // 128
    return pl.pallas_call(
        kern,
        grid=(batch, DIFF_HEADS),
        in_specs=[
            pl.BlockSpec((seq, 128), lambda b, h: (b, qcol + h)),
            pl.BlockSpec((seq, 128), lambda b, h: (b, kcol + h)),
            pl.BlockSpec((seq, 128), lambda b, h: (b, vcol + h)),
            pl.BlockSpec((1, 2 * nq - 1, Q_BLOCK, Q_BLOCK), lambda b, h: (h, 0, 0, 0)),
            pl.BlockSpec((4, HEAD_DIM), lambda b, h: (0, 0)),
            pl.BlockSpec((1, DIFF_V), lambda b, h: (0, 0)),
        ],
        out_specs=pl.BlockSpec((seq, 128), lambda b, h: (b, h)),
        out_shape=jax.ShapeDtypeStruct((batch * seq, BRANCH_WIDTH), BF16),
        scratch_shapes=[pltpu.VMEM((DIFF_UNROLL, 2 * DIFF_QROWS, seq), F32),
                        pltpu.VMEM((seq, 2 * DIFF_V), BF16)],
        compiler_params=_cparams(2),
        name="diff_attention",
    )(za, za, za, tb, lam_vecs.astype(F32), subln.astype(F32).reshape(1, DIFF_V))


def _swa_kernel(q_ref, k_ref, v_ref, bias_ref, sink_ref, o_ref, kp_scr, vp_scr, *, nb):
    seq = q_ref.shape[0]
    blk = SWA_BLOCK
    pad = jnp.zeros((blk, kp_scr.shape[1]), kp_scr.dtype)
    for scr, ref in ((kp_scr, k_ref), (vp_scr, v_ref)):
        scr[0:blk] = pad
        scr[blk + seq:2 * blk + seq] = pad
        scr[blk:blk + seq] = ref[...]
    krow = lax.broadcasted_iota(jnp.int32, (3 * blk, blk), 0)
    lane = lax.broadcasted_iota(jnp.int32, (blk, 128), 1)
    low = lane < HEAD_DIM
    zero = jnp.zeros((blk, 128), q_ref.dtype)
    scale = HEAD_DIM ** -0.5
    rep = SWA_HEADS // SWA_KV_HEADS

    def body(n, carry):
        r0 = pl.multiple_of(n * blk, blk)
        q = q_ref[pl.ds(r0, blk), :]
        kw = kp_scr[pl.ds(r0, 3 * blk), :]
        vw = vp_scr[pl.ds(r0, 3 * blk), :]
        outside = ((krow < blk) & (n == 0)) | ((krow >= 2 * blk) & (n == nb - 1))
        heads = range(SWA_HEADS)
        qms = [jnp.where(low if hd % 2 == 0 else jnp.logical_not(low),
                         q[:, (hd // 2) * 128:(hd // 2 + 1) * 128] * scale, zero) for hd in heads]
        ss = [lax.dot_general(kw[:, (hd // rep) * 128:(hd // rep + 1) * 128], qms[hd], (((1,), (1,)), ((), ())),
                              preferred_element_type=F32) for hd in heads]
        ss = [jnp.where(outside, NEG_BIG, ss[hd] + bias_ref[hd]) for hd in heads]
        sinks = [sink_ref[hd][:, 0:1] for hd in heads]
        ms = [jnp.maximum(jnp.max(ss[hd], axis=0, keepdims=True), sinks[hd]) for hd in heads]
        es = [jnp.exp(ss[hd] - ms[hd]) for hd in heads]
        dens = [jnp.sum(es[hd], axis=0, keepdims=True) + jnp.exp(sinks[hd] - ms[hd]) for hd in heads]
        ps = [(es[hd] / dens[hd]).astype(BF16) for hd in heads]
        halves = [lax.dot_general(ps[hd], vw[:, (hd // rep) * 128:(hd // rep + 1) * 128], (((0,), (0,)), ((), ())),
                                  preferred_element_type=F32) for hd in heads]
        o = jnp.concatenate([jnp.where(low, halves[2 * c], halves[2 * c + 1]) for c in range(SWA_HEADS // 2)], axis=1)
        o_ref[pl.ds(r0, blk), :] = o.astype(o_ref.dtype)
        return carry

    lax.fori_loop(0, nb, body, 0)


def _swa_attention(za, bias, sink, batch, seq):
    nb = seq // SWA_BLOCK
    kern = functools.partial(_swa_kernel, nb=nb)
    sink_b = jnp.broadcast_to(sink.astype(F32).reshape(SWA_HEADS, 1, 1), (SWA_HEADS, 1, 128))
    kvw = 2 * SWA_KV_HEADS * HEAD_DIM
    return pl.pallas_call(
        kern,
        grid=(batch,),
        in_specs=[
            pl.BlockSpec((seq, BRANCH_WIDTH), lambda b: (b, ZA_CQ // BRANCH_WIDTH)),
            pl.BlockSpec((seq, kvw), lambda b: (b, ZA_CK // kvw)),
            pl.BlockSpec((seq, kvw), lambda b: (b, ZA_CV // kvw)),
            pl.BlockSpec((SWA_HEADS, 3 * SWA_BLOCK, SWA_BLOCK), lambda b: (0, 0, 0)),
            pl.BlockSpec((SWA_HEADS, 1, 128), lambda b: (0, 0, 0)),
        ],
        out_specs=pl.BlockSpec((seq, BRANCH_WIDTH), lambda b: (b, 0)),
        out_shape=jax.ShapeDtypeStruct((batch * seq, BRANCH_WIDTH), BF16),
        scratch_shapes=[pltpu.VMEM((seq + 2 * SWA_BLOCK, kvw), BF16),
                        pltpu.VMEM((seq + 2 * SWA_BLOCK, kvw), BF16)],
        compiler_params=_cparams(1),
        name="swa_attention",
    )(za, za, za, bias.transpose(0, 2, 1), sink_b)


def _shift_rows(x, d):
    rows = x.shape[0]
    t = lax.broadcasted_iota(jnp.int32, x.shape, 0)
    rolled = pltpu.roll(x, (-d) % rows, 0)
    return jnp.where((t + d >= 0) & (t + d < rows), rolled, 0.0)


def _gdn_conv_kernel(x_ref, w_ref, o_ref):
    j = pl.program_id(1)
    x = x_ref[...].astype(F32)
    w = w_ref[...]
    half = GDN_CONV // 2
    acc = x * w[half:half + 1]
    for tap in range(GDN_CONV):
        if tap != half:
            acc = acc + _shift_rows(x, tap - half) * w[tap:tap + 1]
    h = acc * jax.nn.sigmoid(acc)
    inv = lax.rsqrt(jnp.sum(h * h, axis=-1, keepdims=True) + RMS_EPS)
    factor = jnp.where(j < 2 * GDN_HEADS, inv, 1.0) * jnp.where(j < GDN_HEADS, GDN_DK ** -0.5, 1.0)
    o_ref[...] = (h * factor).astype(o_ref.dtype)


def _gdn_conv(zb, conv_w, batch, seq):
    c0 = ZB_QKV // 128
    return pl.pallas_call(
        _gdn_conv_kernel,
        grid=(batch, GDN_QKV // 128),
        in_specs=[pl.BlockSpec((seq, 128), lambda b, j: (b, c0 + j)),
                  pl.BlockSpec((GDN_CONV, 128), lambda b, j: (0, j))],
        out_specs=pl.BlockSpec((seq, 128), lambda b, j: (b, j)),
        out_shape=jax.ShapeDtypeStruct((batch * seq, GDN_QKV), BF16),
        compiler_params=_cparams(2),
        name="gdn_conv",
    )(zb, conv_w.astype(F32))


def _gdn_gate_kernel(ab_ref, alog_ref, dtb_ref, gc_ref, gd_ref, gl_ref, beta_ref):
    ab = ab_ref[...]
    a = ab[0:GDN_CHAINS]
    seq = a.shape[1]
    xs = a + dtb_ref[...]
    softplus = jnp.maximum(xs, 0.0) + jnp.log1p(jnp.exp(-jnp.abs(xs)))
    g = -jnp.exp(alog_ref[...]) * softplus
    beta_ref[...] = jax.nn.sigmoid(ab[GDN_CHAINS:2 * GDN_CHAINS])
    pos = lax.broadcasted_iota(jnp.int32, g.shape, 1) % GDN_CHUNK
    cf = g
    cr = g
    s = 1
    while s < GDN_CHUNK:
        cf = cf + jnp.where(pos >= s, pltpu.roll(cf, s, 1), 0.0)
        cr = cr + jnp.where(pos < GDN_CHUNK - s, pltpu.roll(cr, seq - s, 1), 0.0)
        s *= 2
    fwd = lax.broadcasted_iota(jnp.int32, g.shape, 0) < GDN_HEADS
    gc_ref[...] = jnp.where(fwd, cf, cr)
    gd_ref[...] = jnp.where(fwd, cr, cf) - g
    gl_ref[...] = cf + cr - g


def _gdn_gates(ab_t, a_log, dt_bias, batch, seq):
    row = pl.BlockSpec((GDN_CHAINS, seq), lambda b: (0, b))
    shp = jax.ShapeDtypeStruct((GDN_CHAINS, batch * seq), F32)
    par = pl.BlockSpec((GDN_CHAINS, 1), lambda b: (0, 0))
    return pl.pallas_call(
        _gdn_gate_kernel,
        grid=(batch,),
        in_specs=[pl.BlockSpec((2 * GDN_CHAINS, seq), lambda b: (0, b)), par, par],
        out_specs=[row, row, row, row],
        out_shape=[shp, shp, shp, shp],
        compiler_params=_cparams(1),
        name="gdn_gates",
    )(ab_t, a_log.astype(F32).reshape(GDN_CHAINS, 1), dt_bias.astype(F32).reshape(GDN_CHAINS, 1))


def _hdot(a, b):
    return jnp.dot(a.astype(BF16), b.astype(BF16), preferred_element_type=F32)


def _gdn_fused_kernel(qkvf_ref, qkvb_ref, gccf_ref, gccb_ref, gdcf_ref, gdcb_ref, betaf_ref, betab_ref,
                      gcrf_ref, gcrb_ref, glf_ref, glb_ref, of_ref, ob_ref, s_scr):
    @pl.when(pl.program_id(1) == 0)
    def _():
        s_scr[...] = jnp.zeros_like(s_scr)

    g = GDN_GROUP
    c = GDN_CHUNK
    nc = g // c
    ri = lax.broadcasted_iota(jnp.int32, (g, g), 0)
    ci = lax.broadcasted_iota(jnp.int32, (g, g), 1)
    same = (ri // c) == (ci // c)
    ahead = jnp.where(same, ri - ci, -g)
    behind = jnp.where(same, ci - ri, -g)
    eye_s = (lax.broadcasted_iota(jnp.int32, (c, g), 0)
             == lax.broadcasted_iota(jnp.int32, (c, g), 1) % c).astype(F32)

    def block_diag(xs):
        return jnp.where(same, jnp.concatenate([xs] * nc, axis=0), 0.0)

    def row_blocks_sum(xd):
        out = xd[0:c]
        for i in range(1, nc):
            out = out + xd[i * c:(i + 1) * c]
        return out

    dirs = ((qkvf_ref, gccf_ref, gdcf_ref, betaf_ref, gcrf_ref, glf_ref, of_ref, ahead),
            (qkvb_ref, gccb_ref, gdcb_ref, betab_ref, gcrb_ref, glb_ref, ob_ref, behind))
    chains = [(d, h) for d in range(2) for h in range(GDN_HEADS)]
    n_ch = len(chains)
    xs_l, xd_l, p_l, rhs_l, qg_l, qk_l, kgt_l = [], [], [], [], [], [], []
    for n, (d, h) in enumerate(chains):
        qkv_ref, gcc_ref, gdc_ref, beta_ref, gcr_ref = dirs[d][0:5]
        order = dirs[d][7]
        qbf = qkv_ref[:, h * GDN_DK:(h + 1) * GDN_DK]
        kbf = qkv_ref[:, (GDN_HEADS + h) * GDN_DK:(GDN_HEADS + h + 1) * GDN_DK]
        q = qbf.astype(F32)
        k = kbf.astype(F32)
        v = qkv_ref[:, (2 * GDN_HEADS + h) * GDN_DK:(2 * GDN_HEADS + h + 1) * GDN_DK].astype(F32)
        qk_raw = lax.dot_general(qbf, kbf, (((1,), (1,)), ((), ())), preferred_element_type=F32)
        gcc = gcc_ref[:, n:n + 1]
        gdc = gdc_ref[:, n:n + 1]
        beta = beta_ref[:, n:n + 1]
        gcr = gcr_ref[n:n + 1, :]
        kb = k * beta
        kk = lax.dot_general(kb.astype(BF16), kbf, (((1,), (1,)), ((), ())), preferred_element_type=F32)
        decay = jnp.exp(jnp.where(order >= 0, gcc - gcr, NEG_BIG))
        xd = -jnp.where(order > 0, kk * decay, 0.0)
        qkm = qk_raw * decay
        egc = jnp.exp(gcc)
        qg_l.append((q * egc).astype(BF16))
        qk_l.append((qkm[:, 0:128] + qkm[:, 128:256]).astype(BF16))
        kgt_l.append((k * jnp.exp(gdc)).T.astype(BF16))
        xs = row_blocks_sum(xd)
        xs_l.append(xs)
        xd_l.append(xd)
        p_l.append(eye_s + xs)
        rhs_l.append(jnp.concatenate([v * beta, kb * egc], axis=1).astype(BF16))
    xs_l = [_hdot(xs_l[i], xd_l[i]) for i in range(n_ch)]
    for _ in range(4):
        r_l = [_hdot(jnp.concatenate([p_l[i], xs_l[i]], axis=0), block_diag(xs_l[i])) for i in range(n_ch)]
        p_l = [p_l[i] + r_l[i][0:c] for i in range(n_ch)]
        xs_l = [r_l[i][c:2 * c] for i in range(n_ch)]
    p_l = [p_l[i] + _hdot(p_l[i], block_diag(xs_l[i])) for i in range(n_ch)]
    sol_l = [_hdot(block_diag(p_l[i]), rhs_l[i]) for i in range(n_ch)]
    u_l = [sol[:, 0:GDN_DV] for sol in sol_l]
    w_l = [sol[:, GDN_DV:].astype(BF16) for sol in sol_l]

    zeros = jnp.zeros((c, GDN_DV), BF16)
    states = [s_scr[n] for n in range(n_ch)]
    for step in range(nc):
        tops, v2s = [], []
        for n, (d, h) in enumerate(chains):
            i = step if d == 0 else nc - 1 - step
            rows = slice(i * c, (i + 1) * c)
            sb = states[n].astype(BF16)
            top = jnp.dot(jnp.concatenate([w_l[n][rows], qg_l[n][rows]], axis=0), sb, preferred_element_type=F32)
            vb = (u_l[n][rows] - top[0:c]).astype(BF16)
            v2s.append(jnp.concatenate([vb, zeros] if i % 2 == 0 else [zeros, vb], axis=0))
            tops.append(top)
        for n, (d, h) in enumerate(chains):
            gl_ref, o_ref = dirs[d][5:7]
            i = step if d == 0 else nc - 1 - step
            rows = slice(i * c, (i + 1) * c)
            pair = slice((i // 2) * 2 * c, (i // 2 + 1) * 2 * c)
            bot = jnp.dot(jnp.concatenate([qk_l[n][rows], kgt_l[n][:, pair]], axis=0), v2s[n],
                          preferred_element_type=F32)
            o_ref[rows, h * GDN_DV:(h + 1) * GDN_DV] = (tops[n][c:2 * c] + bot[0:c]).astype(o_ref.dtype)
            decay = jnp.exp(gl_ref[n:n + 1, i * c:i * c + 1])
            states[n] = states[n] * decay + bot[c:c + GDN_DK]
    for n in range(n_ch):
        s_scr[n] = states[n]


def _gdn_fused(qkvn, gcc, gdc, betac, gcr, gl, batch, seq):
    ng = seq // GDN_GROUP
    half = GDN_HEADS * GDN_DV
    n = batch * seq
    fwd = lambda b, s: (b * ng + s, 0)
    bwd = lambda b, s: (b * ng + ng - 1 - s, 0)
    fwd_t = lambda b, s: (0, b * ng + s)
    bwd_t = lambda b, s: (0, b * ng + ng - 1 - s)
    both = lambda shape, f, r: [pl.BlockSpec(shape, f), pl.BlockSpec(shape, r)]
    col = (GDN_GROUP, GDN_CHAINS)
    row = (GDN_CHAINS, GDN_GROUP)
    return pl.pallas_call(
        _gdn_fused_kernel,
        grid=(batch, ng),
        in_specs=(both((GDN_GROUP, GDN_QKV), fwd, bwd) + both(col, fwd, bwd) + both(col, fwd, bwd)
                  + both(col, fwd, bwd) + both(row, fwd_t, bwd_t) + both(row, fwd_t, bwd_t)),
        out_specs=both((GDN_GROUP, half), fwd, bwd),
        out_shape=[jax.ShapeDtypeStruct((n, half), BF16), jax.ShapeDtypeStruct((n, half), BF16)],
        scratch_shapes=[pltpu.VMEM((GDN_CHAINS, GDN_DK, GDN_DV), F32)],
        compiler_params=_cparams(2),
        name="gdn_fused",
    )(qkvn, qkvn, gcc, gcc, gdc, gdc, betac, betac, gcr, gcr, gl, gl)


def _layer_norm(r, g, b):
    mu = jnp.mean(r, axis=-1, keepdims=True)
    var = jnp.mean(jnp.square(r - mu), axis=-1, keepdims=True)
    return (r - mu) * lax.rsqrt(var + LN_EPS) * g + b


def _merge_kernel(ya_ref, of_ref, ob_ref, gate_ref, yc_ref, gz_ref, x_ref, wb_ref, wo_ref,
                  nw_ref, lng_ref, lnb_ref, o_ref):
    segs = []
    for h in range(GDN_HEADS):
        sl = slice(h * GDN_DV, (h + 1) * GDN_DV)
        o = of_ref[:, sl].astype(F32) + ob_ref[:, sl].astype(F32)
        o = o * lax.rsqrt(jnp.mean(o * o, axis=-1, keepdims=True) + RMS_EPS) * nw_ref[...]
        gt = gate_ref[:, sl].astype(F32)
        segs.append(o * (gt * jax.nn.sigmoid(gt)))
    yb = jnp.concatenate(segs, axis=1).astype(BF16)
    merged = None
    for n, y in enumerate((ya_ref[...], yb, yc_ref[...])):
        proj = jnp.dot(y, wb_ref[n], preferred_element_type=F32)
        term = jax.nn.sigmoid(gz_ref[:, n * D_MODEL:(n + 1) * D_MODEL].astype(F32)) * proj
        merged = term if merged is None else merged + term
    hmix = jnp.dot(merged.astype(BF16), wo_ref[...], preferred_element_type=F32)
    o_ref[...] = _layer_norm(DEEPNORM_ALPHA * x_ref[...] + hmix, lng_ref[...], lnb_ref[...])


def _merge(ya, o_f, o_b, zb, yc, x, w_branch, w_out, norm_w, ln_g, ln_b, tm):
    n = x.shape[0]
    row = lambda width, col=0: pl.BlockSpec((tm, width), lambda i, col=col: (i, col))
    full = lambda shape: pl.BlockSpec(shape, lambda i: (0,) * len(shape), pipeline_mode=pl.Buffered(1))
    return pl.pallas_call(
        _merge_kernel,
        grid=(n // tm,),
        in_specs=[row(BRANCH_WIDTH), row(BRANCH_WIDTH), row(BRANCH_WIDTH),
                  row(BRANCH_WIDTH, ZB_GATE // BRANCH_WIDTH), row(BRANCH_WIDTH),
                  row(N_BRANCHES * D_MODEL, ZB_GZ // (N_BRANCHES * D_MODEL)), row(D_MODEL),
                  full((N_BRANCHES, BRANCH_WIDTH, D_MODEL)), full((D_MODEL, D_MODEL)),
                  full((1, GDN_DV)), full((1, D_MODEL)), full((1, D_MODEL))],
        out_specs=row(D_MODEL),
        out_shape=jax.ShapeDtypeStruct((n, D_MODEL), F32),
        compiler_params=_cparams(1),
        name="merge_ln",
    )(ya, o_f, o_b, zb, yc, zb, x, w_branch, w_out, norm_w.astype(F32).reshape(1, GDN_DV),
      ln_g.astype(F32).reshape(1, D_MODEL), ln_b.astype(F32).reshape(1, D_MODEL))


FFN_SPLIT = 2


def _ffn_kernel(xp_ref, x_ref, xn_ref, wup_ref, cw_ref, cb_ref, wdn_ref, lng_ref, lnb_ref, o_ref, ob_ref, *,
                tiles_per_seq):
    i = pl.program_id(0)
    tm = x_ref.shape[0]
    halo = V7X_SUBLANES
    x = x_ref[...]
    xe = jnp.concatenate([xp_ref[...], x, xn_ref[...]], axis=0).astype(BF16)
    xb = x.astype(BF16)
    rows = lax.broadcasted_iota(jnp.int32, (tm + 2 * halo, 1), 0)
    first = (i % tiles_per_seq) == 0
    last = (i % tiles_per_seq) == tiles_per_seq - 1
    keep = jnp.logical_not(((rows < halo) & first) | ((rows >= tm + halo) & last))
    fc = D_FF // FFN_SPLIT
    ext = tm + 2 * halo
    acc = None
    for c in range(FFN_SPLIT):
        ge = jnp.dot(xe, wup_ref[:, c * fc:(c + 1) * fc], preferred_element_type=F32)
        ge = jnp.where(keep, ge, 0.0)
        up = jnp.dot(xb, wup_ref[:, D_FF + c * fc:D_FF + (c + 1) * fc], preferred_element_type=F32)
        cw = cw_ref[:, c * fc:(c + 1) * fc]
        conv = (pltpu.roll(ge, 1, 0) * cw[0:1] + ge * cw[1:2] + pltpu.roll(ge, ext - 1, 0) * cw[2:3])
        gate = conv[halo:halo + tm] + cb_ref[:, c * fc:(c + 1) * fc]
        act = (gate * jax.nn.sigmoid(gate) * up).astype(BF16)
        part = jnp.dot(act, wdn_ref[c * fc:(c + 1) * fc, :], preferred_element_type=F32)
        acc = part if acc is None else acc + part
    y = _layer_norm(DEEPNORM_ALPHA * x + acc, lng_ref[...], lnb_ref[...])
    o_ref[...] = y
    ob_ref[...] = y.astype(ob_ref.dtype)


def _ffn(x, w_up, conv_w, conv_b, w_down, ln_g, ln_b, seq, tm):
    n = x.shape[0]
    halo = V7X_SUBLANES
    per = tm // halo
    nh = n // halo
    full = lambda shape: pl.BlockSpec(shape, lambda i: (0,) * len(shape), pipeline_mode=pl.Buffered(1))
    kern = functools.partial(_ffn_kernel, tiles_per_seq=seq // tm)
    return pl.pallas_call(
        kern,
        grid=(n // tm,),
        in_specs=[pl.BlockSpec((halo, D_MODEL), lambda i: (jnp.maximum(i * per - 1, 0), 0)),
                  pl.BlockSpec((tm, D_MODEL), lambda i: (i, 0)),
                  pl.BlockSpec((halo, D_MODEL), lambda i: (jnp.minimum((i + 1) * per, nh - 1), 0)),
                  full((D_MODEL, 2 * D_FF)), full((FFN_CONV, D_FF)), full((1, D_FF)),
                  full((D_FF, D_MODEL)), full((1, D_MODEL)), full((1, D_MODEL))],
        out_specs=[pl.BlockSpec((tm, D_MODEL), lambda i: (i, 0)), pl.BlockSpec((tm, D_MODEL), lambda i: (i, 0))],
        out_shape=[jax.ShapeDtypeStruct((n, D_MODEL), F32), jax.ShapeDtypeStruct((n, D_MODEL), BF16)],
        compiler_params=_cparams(1),
        name="ffn_ln",
    )(x, x, x, w_up, conv_w.astype(F32), conv_b.astype(F32).reshape(1, D_FF), w_down,
      ln_g.astype(F32).reshape(1, D_MODEL), ln_b.astype(F32).reshape(1, D_MODEL))


def _pack_w_in(w):
    widths = (512, 512, 512, GDN_QKV, 512, 8, 8, 512, 128, 128, N_BRANCHES * D_MODEL)
    offs = [0]
    for wd in widths:
        offs.append(offs[-1] + wd)
    seg = lambda i: w[:, offs[i]:offs[i + 1]]
    aq, ak, av, bqkv, bgate, ba, bb, cq, ck, cv, gz = (seg(i) for i in range(len(widths)))
    dup = lambda t: jnp.concatenate([t[:, :HEAD_DIM], t[:, :HEAD_DIM], t[:, HEAD_DIM:], t[:, HEAD_DIM:]], axis=1)
    w_a = jnp.concatenate([aq * DIFF_QSCALE, ak, av, cq, dup(ck), dup(cv)], axis=1)
    w_b = jnp.concatenate([gz, bqkv, bgate], axis=1)
    pad = jnp.zeros((w.shape[0], ZC_COLS - 2 * GDN_CHAINS), w.dtype)
    w_c = jnp.concatenate([ba, bb, pad], axis=1)
    return w_a.astype(BF16), w_b.astype(BF16), w_c.astype(BF16)


def kernel(x, rel_bias, w_in, diff_lambda, diff_subln, gdn_conv, gdn_a_log, gdn_dt_bias, gdn_norm, swa_sink,
           w_branch, w_out, ln1_g, ln1_b, ffn_up, ffn_conv, ffn_conv_b, ffn_down, ln2_g, ln2_b):
    batch, seq, d = x.shape
    assert d == D_MODEL and seq % GDN_GROUP == 0 and seq % (DIFF_QROWS * DIFF_UNROLL) == 0
    n = batch * seq
    tm = 256
    tb_diff = _diff_bias_tiles(rel_bias, seq) * LOG2_E
    tb_swa = _swa_bias_tiles(rel_bias)
    xf = x.reshape(n, d)
    xb = xf.astype(BF16)
    for l in range(DEPTH):
        w_a, w_b, w_c = _pack_w_in(w_in[l])
        za = _matmul(xb, w_a, BF16, 1024, ZA_COLS // 2, "in_proj_a")
        zb = _matmul(xb, w_b, BF16, 1024, ZB_COLS // 2, "in_proj_b")
        zc = _matmul(xb, w_c, F32, 1024, ZC_COLS, "in_proj_c")
        ya = _diff_attention(za, tb_diff, diff_lambda[l], diff_subln[l], batch, seq, l)
        yc = _swa_attention(za, tb_swa, swa_sink[l], batch, seq)
        qkvn = _gdn_conv(zb, gdn_conv[l], batch, seq)
        ab_t = zc[:, 0:2 * GDN_CHAINS].T
        gc, gd, gl, beta = _gdn_gates(ab_t, gdn_a_log[l], gdn_dt_bias[l], batch, seq)
        o_f, o_b = _gdn_fused(qkvn, gc.T, gd.T, beta.T, gc, gl, batch, seq)
        xf = _merge(ya, o_f, o_b, zb, yc, xf, w_branch[l].astype(BF16), w_out[l].astype(BF16),
                    gdn_norm[l], ln1_g[l], ln1_b[l], 2 * tm)
        xf, xb = _ffn(xf, ffn_up[l].astype(BF16), ffn_conv[l], ffn_conv_b[l], ffn_down[l].astype(BF16),
                      ln2_g[l], ln2_b[l], seq, 2 * tm)
    return xf.reshape(batch, seq, d)
```

```python
import functools
import math

import jax
import jax.numpy as jnp
from jax import lax
from jax.experimental import pallas as pl
from jax.experimental.pallas import tpu as pltpu

F32 = jnp.float32
BF16 = jnp.bfloat16

D_MODEL = 1024
DEPTH = 2
HEAD_DIM = 64
DIFF_HEADS = 4
DIFF_V = 2 * HEAD_DIM
Q_BLOCK = 128
GDN_HEADS = 4
GDN_DK = 128
GDN_DV = 128
GDN_QKV = GDN_HEADS * (2 * GDN_DK + GDN_DV)
GDN_CONV = 5
GDN_CHUNK = 64
SWA_HEADS = 8
SWA_KV_HEADS = 2
SWA_WINDOW = 128
SWA_BLOCK = 128
BRANCH_WIDTH = 512
N_BRANCHES = 3
REL_BUCKETS = 32
REL_MAX_DIST = 128
D_FF = 2816
FFN_CONV = 3
DEEPNORM_ALPHA = (2 * DEPTH) ** 0.25
LN_EPS = 1e-5
RMS_EPS = 1e-6

V7X_LANES = 128
V7X_SUBLANES = 8
V7X_VMEM_BYTES = 64 * 1024 * 1024
VMEM_LIMIT = V7X_VMEM_BYTES * 7 // 8

NEG_BIG = -1e30

ZA_AQ, ZA_AK, ZA_AV = 0, 512, 1024
ZA_CQ = 1536
ZA_CK = 2048
ZA_CV = 2304
ZA_COLS = 2560
ZB_GZ = 0
ZB_QKV = 3072
ZB_GATE = 4608
ZB_COLS = 5120
ZC_COLS = 128

GDN_GROUP = 256
GDN_CHAINS = 2 * GDN_HEADS


def _cparams(n_grid):
    return pltpu.CompilerParams(dimension_semantics=("arbitrary",) * n_grid, vmem_limit_bytes=VMEM_LIMIT)


def _t5_bucket(rel):
    nb = REL_BUCKETS // 2
    ret = jnp.where(rel > 0, nb, 0)
    n = jnp.abs(rel)
    max_exact = nb // 2
    large = max_exact + (jnp.log(jnp.maximum(n, 1).astype(jnp.float32) / max_exact)
                         / math.log(REL_MAX_DIST / max_exact) * (nb - max_exact)).astype(jnp.int32)
    large = jnp.minimum(large, nb - 1)
    return ret + jnp.where(n < max_exact, n, large)


def _table_lookup(table, bucket):
    out = jnp.zeros((table.shape[1],) + bucket.shape, F32)
    for i in range(REL_BUCKETS):
        out = jnp.where(bucket[None] == i, table[i].astype(F32).reshape((-1,) + (1,) * bucket.ndim), out)
    return out


def _diff_bias_tiles(rel_bias, seq):
    nq = seq // Q_BLOCK
    d = jnp.arange(2 * nq - 1) - (nq - 1)
    r = jnp.arange(Q_BLOCK)
    rel = d[:, None, None] * Q_BLOCK + r[None, None, :] - r[None, :, None]
    return _table_lookup(rel_bias[:, :DIFF_HEADS], _t5_bucket(rel))


def _swa_bias_tiles(rel_bias):
    kb = 3 * SWA_BLOCK
    rel = jnp.arange(kb)[None, :] - SWA_WINDOW - jnp.arange(SWA_BLOCK)[:, None]
    bias = _table_lookup(rel_bias[:, DIFF_HEADS:], _t5_bucket(rel))
    return jnp.where((jnp.abs(rel) <= SWA_WINDOW)[None], bias, NEG_BIG)


def _matmul_kernel(x_ref, w_ref, o_ref):
    o_ref[...] = jnp.dot(x_ref[...].astype(BF16), w_ref[...],
                         preferred_element_type=F32).astype(o_ref.dtype)


def _matmul(x, w, out_dtype, tm, tn, name):
    m, k = x.shape
    n = w.shape[1]
    return pl.pallas_call(
        _matmul_kernel,
        grid=(n // tn, m // tm),
        in_specs=[pl.BlockSpec((tm, k), lambda j, i: (i, 0)),
                  pl.BlockSpec((k, tn), lambda j, i: (0, j))],
        out_specs=pl.BlockSpec((tm, tn), lambda j, i: (i, j)),
        out_shape=jax.ShapeDtypeStruct((m, n), out_dtype),
        compiler_params=_cparams(2),
        name=name,
    )(x, w)


LOG2_E = math.log2(math.e)
DIFF_QSCALE = HEAD_DIM ** -0.5 * LOG2_E
DIFF_QROWS = 2 * Q_BLOCK
DIFF_UNROLL = 4


def _diff_attn_kernel(q_ref, k_ref, v_ref, tb_ref, lam_ref, subln_ref, o_ref, s_scr, vx_scr, *, nq, lam_init):
    qr = DIFF_QROWS
    vx_scr[:, 0:DIFF_V] = v_ref[...]
    vx_scr[:, DIFF_V:2 * DIFF_V] = jnp.ones((v_ref.shape[0], DIFF_V), vx_scr.dtype)
    lv = lam_ref[...]
    lam = (jnp.exp(jnp.sum(lv[0:1] * lv[1:2], axis=-1, keepdims=True))
           - jnp.exp(jnp.sum(lv[2:3] * lv[3:4], axis=-1, keepdims=True)) + lam_init)
    lane = lax.broadcasted_iota(jnp.int32, (qr, 128), 1)
    zero = jnp.zeros((qr, 128), q_ref.dtype)

    def scores(qb, scr):
        r0 = pl.multiple_of(qb * qr, qr)
        q = q_ref[pl.ds(r0, qr), :]
        qz = jnp.concatenate([jnp.where(lane < HEAD_DIM, q, zero),
                              jnp.where(lane >= HEAD_DIM, q, zero)], axis=0)
        for kp in range(nq // 2):
            kblk = k_ref[kp * qr:(kp + 1) * qr, :]
            s = lax.dot_general(qz, kblk, (((1,), (1,)), ((), ())), preferred_element_type=F32)
            for a in range(2):
                d0 = nq - 1 - (2 * qb + a) + 2 * kp
                bias = jnp.concatenate([tb_ref[0, d0], tb_ref[0, d0 + 1]], axis=1)
                for c in range(2):
                    rows = slice(c * qr + a * Q_BLOCK, c * qr + (a + 1) * Q_BLOCK)
                    scr[rows, kp * qr:(kp + 1) * qr] = s[rows] + bias

    def softmax(scr):
        s = scr[...]
        return jnp.exp2(s - jnp.max(s, axis=-1, keepdims=True)).astype(BF16)

    def finish(qb, e):
        r0 = pl.multiple_of(qb * qr, qr)
        pv = jnp.dot(e, vx_scr[...], preferred_element_type=F32)
        pv = pv[:, 0:DIFF_V] / pv[:, DIFF_V:2 * DIFF_V]
        o = pv[:qr] - lam * pv[qr:]
        o = o * lax.rsqrt(jnp.mean(o * o, axis=-1, keepdims=True) + RMS_EPS) * subln_ref[...]
        o_ref[pl.ds(r0, qr), :] = (o * (1.0 - lam_init)).astype(o_ref.dtype)

    def body(i, carry):
        blocks = [i * DIFF_UNROLL + j for j in range(DIFF_UNROLL)]
        for j, qb in enumerate(blocks):
            scores(qb, s_scr.at[j])
        probs = [softmax(s_scr.at[j]) for j in range(DIFF_UNROLL)]
        for qb, e in zip(blocks, probs):
            finish(qb, e)
        return carry

    lax.fori_loop(0, nq // (2 * DIFF_UNROLL), body, 0)


def _diff_attention(za, tb, lam_vecs, subln, batch, seq, layer_idx):
    nq = seq // Q_BLOCK
    lam_init = 0.8 - 0.6 * math.exp(-0.3 * layer_idx)
    kern = functools.partial(_diff_attn_kernel, nq=nq, lam_init=lam_init)
    qcol, kcol, vcol = ZA_AQ // 128, ZA_AK // 128, ZA_AV // 128
    return pl.pallas_call(
        kern,
        grid=(batch, DIFF_HEADS),
        in_specs=[
            pl.BlockSpec((seq, 128), lambda b, h: (b, qcol + h)),
            pl.BlockSpec((seq, 128), lambda b, h: (b, kcol + h)),
            pl.BlockSpec((seq, 128), lambda b, h: (b, vcol + h)),
            pl.BlockSpec((1, 2 * nq - 1, Q_BLOCK, Q_BLOCK), lambda b, h: (h, 0, 0, 0)),
            pl.BlockSpec((4, HEAD_DIM), lambda b, h: (0, 0)),
            pl.BlockSpec((1, DIFF_V), lambda b, h: (0, 0)),
        ],
        out_specs=pl.BlockSpec((seq, 128), lambda b, h: (b, h)),
        out_shape=jax.ShapeDtypeStruct((batch * seq, BRANCH_WIDTH), BF16),
        scratch_shapes=[pltpu.VMEM((DIFF_UNROLL, 2 * DIFF_QROWS, seq), F32),
                        pltpu.VMEM((seq, 2 * DIFF_V), BF16)],
        compiler_params=_cparams(2),
        name="diff_attention",
    )(za, za, za, tb, lam_vecs.astype(F32), subln.astype(F32).reshape(1, DIFF_V))


def _swa_kernel(q_ref, k_ref, v_ref, bias_ref, sink_ref, o_ref, kp_scr, vp_scr, *, nb):
    seq = q_ref.shape[0]
    blk = SWA_BLOCK
    pad = jnp.zeros((blk, kp_scr.shape[1]), kp_scr.dtype)
    for scr, ref in ((kp_scr, k_ref), (vp_scr, v_ref)):
        scr[0:blk] = pad
        scr[blk + seq:2 * blk + seq] = pad
        scr[blk:blk + seq] = ref[...]
    krow = lax.broadcasted_iota(jnp.int32, (3 * blk, blk), 0)
    lane = lax.broadcasted_iota(jnp.int32, (blk, 128), 1)
    low = lane < HEAD_DIM
    zero = jnp.zeros((blk, 128), q_ref.dtype)
    scale = HEAD_DIM ** -0.5
    rep = SWA_HEADS // SWA_KV_HEADS

    def body(n, carry):
        r0 = pl.multiple_of(n * blk, blk)
        q = q_ref[pl.ds(r0, blk), :]
        kw = kp_scr[pl.ds(r0, 3 * blk), :]
        vw = vp_scr[pl.ds(r0, 3 * blk), :]
        outside = ((krow < blk) & (n == 0)) | ((krow >= 2 * blk) & (n == nb - 1))
        heads = range(SWA_HEADS)
        qms = [jnp.where(low if hd % 2 == 0 else jnp.logical_not(low),
                         q[:, (hd // 2) * 128:(hd // 2 + 1) * 128] * scale, zero) for hd in heads]
        ss = [lax.dot_general(kw[:, (hd // rep) * 128:(hd // rep + 1) * 128], qms[hd], (((1,), (1,)), ((), ())),
                              preferred_element_type=F32) for hd in heads]
        ss = [jnp.where(outside, NEG_BIG, ss[hd] + bias_ref[hd]) for hd in heads]
        sinks = [sink_ref[hd][:, 0:1] for hd in heads]
        ms = [jnp.maximum(jnp.max(ss[hd], axis=0, keepdims=True), sinks[hd]) for hd in heads]
        es = [jnp.exp(ss[hd] - ms[hd]) for hd in heads]
        dens = [jnp.sum(es[hd], axis=0, keepdims=True) + jnp.exp(sinks[hd] - ms[hd]) for hd in heads]
        ps = [(es[hd] / dens[hd]).astype(BF16) for hd in heads]
        halves = [lax.dot_general(ps[hd], vw[:, (hd // rep) * 128:(hd // rep + 1) * 128], (((0,), (0,)), ((), ())),
                                  preferred_element_type=F32) for hd in heads]
        o = jnp.concatenate([jnp.where(low, halves[2 * c], halves[2 * c + 1]) for c in range(SWA_HEADS // 2)], axis=1)
        o_ref[pl.ds(r0, blk), :] = o.astype(o_ref.dtype)
        return carry

    lax.fori_loop(0, nb, body, 0)


def _swa_attention(za, bias, sink, batch, seq):
    nb = seq // SWA_BLOCK
    kern = functools.partial(_swa_kernel, nb=nb)
    sink_b = jnp.broadcast_to(sink.astype(F32).reshape(SWA_HEADS, 1, 1), (SWA_HEADS, 1, 128))
    kvw = 2 * SWA_KV_HEADS * HEAD_DIM
    return pl.pallas_call(
        kern,
        grid=(batch,),
        in_specs=[
            pl.BlockSpec((seq, BRANCH_WIDTH), lambda b: (b, ZA_CQ // BRANCH_WIDTH)),
            pl.BlockSpec((seq, kvw), lambda b: (b, ZA_CK // kvw)),
            pl.BlockSpec((seq, kvw), lambda b: (b, ZA_CV // kvw)),
            pl.BlockSpec((SWA_HEADS, 3 * SWA_BLOCK, SWA_BLOCK), lambda b: (0, 0, 0)),
            pl.BlockSpec((SWA_HEADS, 1, 128), lambda b: (0, 0, 0)),
        ],
        out_specs=pl.BlockSpec((seq, BRANCH_WIDTH), lambda b: (b, 0)),
        out_shape=jax.ShapeDtypeStruct((batch * seq, BRANCH_WIDTH), BF16),
        scratch_shapes=[pltpu.VMEM((seq + 2 * SWA_BLOCK, kvw), BF16),
                        pltpu.VMEM((seq + 2 * SWA_BLOCK, kvw), BF16)],
        compiler_params=_cparams(1),
        name="swa_attention",
    )(za, za, za, bias.transpose(0, 2, 1), sink_b)


def _shift_rows(x, d):
    rows = x.shape[0]
    t = lax.broadcasted_iota(jnp.int32, x.shape, 0)
    rolled = pltpu.roll(x, (-d) % rows, 0)
    return jnp.where((t + d >= 0) & (t + d < rows), rolled, 0.0)


def _gdn_conv_kernel(x_ref, w_ref, o_ref):
    j = pl.program_id(1)
    x = x_ref[...].astype(F32)
    w = w_ref[...]
    half = GDN_CONV // 2
    acc = x * w[half:half + 1]
    for tap in range(GDN_CONV):
        if tap != half:
            acc = acc + _shift_rows(x, tap - half) * w[tap:tap + 1]
    h = acc * jax.nn.sigmoid(acc)
    inv = lax.rsqrt(jnp.sum(h * h, axis=-1, keepdims=True) + RMS_EPS)
    factor = jnp.where(j < 2 * GDN_HEADS, inv, 1.0) * jnp.where(j < GDN_HEADS, GDN_DK ** -0.5, 1.0)
    o_ref[...] = h * factor


def _gdn_conv(zb, conv_w, batch, seq):
    c0 = ZB_QKV // 128
    return pl.pallas_call(
        _gdn_conv_kernel,
        grid=(batch, GDN_QKV // 128),
        in_specs=[pl.BlockSpec((seq, 128), lambda b, j: (b, c0 + j)),
                  pl.BlockSpec((GDN_CONV, 128), lambda b, j: (0, j))],
        out_specs=pl.BlockSpec((seq, 128), lambda b, j: (b, j)),
        out_shape=jax.ShapeDtypeStruct((batch * seq, GDN_QKV), F32),
        compiler_params=_cparams(2),
        name="gdn_conv",
    )(zb, conv_w.astype(F32))


def _gdn_gate_kernel(ab_ref, alog_ref, dtb_ref, gc_ref, gd_ref, gl_ref, beta_ref):
    ab = ab_ref[...]
    a = ab[0:GDN_CHAINS]
    seq = a.shape[1]
    xs = a + dtb_ref[...]
    softplus = jnp.maximum(xs, 0.0) + jnp.log1p(jnp.exp(-jnp.abs(xs)))
    g = -jnp.exp(alog_ref[...]) * softplus
    beta_ref[...] = jax.nn.sigmoid(ab[GDN_CHAINS:2 * GDN_CHAINS])
    pos = lax.broadcasted_iota(jnp.int32, g.shape, 1) % GDN_CHUNK
    cf = g
    cr = g
    s = 1
    while s < GDN_CHUNK:
        cf = cf + jnp.where(pos >= s, pltpu.roll(cf, s, 1), 0.0)
        cr = cr + jnp.where(pos < GDN_CHUNK - s, pltpu.roll(cr, seq - s, 1), 0.0)
        s *= 2
    fwd = lax.broadcasted_iota(jnp.int32, g.shape, 0) < GDN_HEADS
    gc_ref[...] = jnp.where(fwd, cf, cr)
    gd_ref[...] = jnp.where(fwd, cr, cf) - g
    gl_ref[...] = cf + cr - g


def _gdn_gates(ab_t, a_log, dt_bias, batch, seq):
    row = pl.BlockSpec((GDN_CHAINS, seq), lambda b: (0, b))
    shp = jax.ShapeDtypeStruct((GDN_CHAINS, batch * seq), F32)
    par = pl.BlockSpec((GDN_CHAINS, 1), lambda b: (0, 0))
    return pl.pallas_call(
        _gdn_gate_kernel,
        grid=(batch,),
        in_specs=[pl.BlockSpec((2 * GDN_CHAINS, seq), lambda b: (0, b)), par, par],
        out_specs=[row, row, row, row],
        out_shape=[shp, shp, shp, shp],
        compiler_params=_cparams(1),
        name="gdn_gates",
    )(ab_t, a_log.astype(F32).reshape(GDN_CHAINS, 1), dt_bias.astype(F32).reshape(GDN_CHAINS, 1))


def _hdot(a, b):
    return jnp.dot(a.astype(BF16), b.astype(BF16), preferred_element_type=F32)


def _gdn_fused_kernel(qkvf_ref, qkvb_ref, gccf_ref, gccb_ref, gdcf_ref, gdcb_ref, betaf_ref, betab_ref,
                      gcrf_ref, gcrb_ref, glf_ref, glb_ref, of_ref, ob_ref, s_scr):
    @pl.when(pl.program_id(1) == 0)
    def _():
        s_scr[...] = jnp.zeros_like(s_scr)

    g = GDN_GROUP
    c = GDN_CHUNK
    nc = g // c
    ri = lax.broadcasted_iota(jnp.int32, (g, g), 0)
    ci = lax.broadcasted_iota(jnp.int32, (g, g), 1)
    same = (ri // c) == (ci // c)
    ahead = jnp.where(same, ri - ci, -g)
    behind = jnp.where(same, ci - ri, -g)
    eye_s = (lax.broadcasted_iota(jnp.int32, (c, g), 0)
             == lax.broadcasted_iota(jnp.int32, (c, g), 1) % c).astype(F32)

    def block_diag(xs):
        return jnp.where(same, jnp.concatenate([xs] * nc, axis=0), 0.0)

    def row_blocks_sum(xd):
        out = xd[0:c]
        for i in range(1, nc):
            out = out + xd[i * c:(i + 1) * c]
        return out

    dirs = ((qkvf_ref, gccf_ref, gdcf_ref, betaf_ref, gcrf_ref, glf_ref, of_ref, ahead),
            (qkvb_ref, gccb_ref, gdcb_ref, betab_ref, gcrb_ref, glb_ref, ob_ref, behind))
    chains = [(d, h) for d in range(2) for h in range(GDN_HEADS)]
    n_ch = len(chains)
    xs_l, xd_l, p_l, rhs_l, qg_l, qk_l, kgt_l = [], [], [], [], [], [], []
    for n, (d, h) in enumerate(chains):
        qkv_ref, gcc_ref, gdc_ref, beta_ref, gcr_ref = dirs[d][0:5]
        order = dirs[d][7]
        q = qkv_ref[:, h * GDN_DK:(h + 1) * GDN_DK]
        k = qkv_ref[:, (GDN_HEADS + h) * GDN_DK:(GDN_HEADS + h + 1) * GDN_DK]
        v = qkv_ref[:, (2 * GDN_HEADS + h) * GDN_DK:(2 * GDN_HEADS + h + 1) * GDN_DK]
        kbf = k.astype(BF16)
        qk_raw = lax.dot_general(q.astype(BF16), kbf, (((1,), (1,)), ((), ())), preferred_element_type=F32)
        gcc = gcc_ref[:, n:n + 1]
        gdc = gdc_ref[:, n:n + 1]
        beta = beta_ref[:, n:n + 1]
        gcr = gcr_ref[n:n + 1, :]
        kb = k * beta
        kk = lax.dot_general(kb.astype(BF16), kbf, (((1,), (1,)), ((), ())), preferred_element_type=F32)
        decay = jnp.exp(jnp.where(order >= 0, gcc - gcr, NEG_BIG))
        xd = -jnp.where(order > 0, kk * decay, 0.0)
        qkm = qk_raw * decay
        egc = jnp.exp(gcc)
        qg_l.append((q * egc).astype(BF16))
        qk_l.append((qkm[:, 0:128] + qkm[:, 128:256]).astype(BF16))
        kgt_l.append((k * jnp.exp(gdc)).T.astype(BF16))
        xs = row_blocks_sum(xd)
        xs_l.append(xs)
        xd_l.append(xd)
        p_l.append(eye_s + xs)
        rhs_l.append(jnp.concatenate([v * beta, kb * egc], axis=1).astype(BF16))
    xs_l = [_hdot(xs_l[i], xd_l[i]) for i in range(n_ch)]
    for _ in range(4):
        r_l = [_hdot(jnp.concatenate([p_l[i], xs_l[i]], axis=0), block_diag(xs_l[i])) for i in range(n_ch)]
        p_l = [p_l[i] + r_l[i][0:c] for i in range(n_ch)]
        xs_l = [r_l[i][c:2 * c] for i in range(n_ch)]
    p_l = [p_l[i] + _hdot(p_l[i], block_diag(xs_l[i])) for i in range(n_ch)]
    sol_l = [_hdot(block_diag(p_l[i]), rhs_l[i]) for i in range(n_ch)]
    u_l = [sol[:, 0:GDN_DV] for sol in sol_l]
    w_l = [sol[:, GDN_DV:].astype(BF16) for sol in sol_l]

    zeros = jnp.zeros((c, GDN_DV), BF16)
    states = [s_scr[n] for n in range(n_ch)]
    for step in range(nc):
        tops, v2s = [], []
        for n, (d, h) in enumerate(chains):
            i = step if d == 0 else nc - 1 - step
            rows = slice(i * c, (i + 1) * c)
            sb = states[n].astype(BF16)
            top = jnp.dot(jnp.concatenate([w_l[n][rows], qg_l[n][rows]], axis=0), sb, preferred_element_type=F32)
            vb = (u_l[n][rows] - top[0:c]).astype(BF16)
            v2s.append(jnp.concatenate([vb, zeros] if i % 2 == 0 else [zeros, vb], axis=0))
            tops.append(top)
        for n, (d, h) in enumerate(chains):
            gl_ref, o_ref = dirs[d][5:7]
            i = step if d == 0 else nc - 1 - step
            rows = slice(i * c, (i + 1) * c)
            pair = slice((i // 2) * 2 * c, (i // 2 + 1) * 2 * c)
            bot = jnp.dot(jnp.concatenate([qk_l[n][rows], kgt_l[n][:, pair]], axis=0), v2s[n],
                          preferred_element_type=F32)
            o_ref[rows, h * GDN_DV:(h + 1) * GDN_DV] = tops[n][c:2 * c] + bot[0:c]
            decay = jnp.exp(gl_ref[n:n + 1, i * c:i * c + 1])
            states[n] = states[n] * decay + bot[c:c + GDN_DK]
    for n in range(n_ch):
        s_scr[n] = states[n]


def _gdn_fused(qkvn, gcc, gdc, betac, gcr, gl, batch, seq):
    ng = seq // GDN_GROUP
    half = GDN_HEADS * GDN_DV
    n = batch * seq
    fwd = lambda b, s: (b * ng + s, 0)
    bwd = lambda b, s: (b * ng + ng - 1 - s, 0)
    fwd_t = lambda b, s: (0, b * ng + s)
    bwd_t = lambda b, s: (0, b * ng + ng - 1 - s)
    both = lambda shape, f, r: [pl.BlockSpec(shape, f), pl.BlockSpec(shape, r)]
    col = (GDN_GROUP, GDN_CHAINS)
    row = (GDN_CHAINS, GDN_GROUP)
    return pl.pallas_call(
        _gdn_fused_kernel,
        grid=(batch, ng),
        in_specs=(both((GDN_GROUP, GDN_QKV), fwd, bwd) + both(col, fwd, bwd) + both(col, fwd, bwd)
                  + both(col, fwd, bwd) + both(row, fwd_t, bwd_t) + both(row, fwd_t, bwd_t)),
        out_specs=both((GDN_GROUP, half), fwd, bwd),
        out_shape=[jax.ShapeDtypeStruct((n, half), F32), jax.ShapeDtypeStruct((n, half), F32)],
        scratch_shapes=[pltpu.VMEM((GDN_CHAINS, GDN_DK, GDN_DV), F32)],
        compiler_params=_cparams(2),
        name="gdn_fused",
    )(qkvn, qkvn, gcc, gcc, gdc, gdc, betac, betac, gcr, gcr, gl, gl)


def _layer_norm(r, g, b):
    mu = jnp.mean(r, axis=-1, keepdims=True)
    var = jnp.mean(jnp.square(r - mu), axis=-1, keepdims=True)
    return (r - mu) * lax.rsqrt(var + LN_EPS) * g + b


def _merge_kernel(ya_ref, of_ref, ob_ref, gate_ref, yc_ref, gz_ref, x_ref, wb_ref, wo_ref,
                  nw_ref, lng_ref, lnb_ref, o_ref):
    segs = []
    for h in range(GDN_HEADS):
        sl = slice(h * GDN_DV, (h + 1) * GDN_DV)
        o = of_ref[:, sl] + ob_ref[:, sl]
        o = o * lax.rsqrt(jnp.mean(o * o, axis=-1, keepdims=True) + RMS_EPS) * nw_ref[...]
        gt = gate_ref[:, sl].astype(F32)
        segs.append(o * (gt * jax.nn.sigmoid(gt)))
    yb = jnp.concatenate(segs, axis=1).astype(BF16)
    merged = None
    for n, y in enumerate((ya_ref[...], yb, yc_ref[...])):
        proj = jnp.dot(y, wb_ref[n], preferred_element_type=F32)
        term = jax.nn.sigmoid(gz_ref[:, n * D_MODEL:(n + 1) * D_MODEL].astype(F32)) * proj
        merged = term if merged is None else merged + term
    hmix = jnp.dot(merged.astype(BF16), wo_ref[...], preferred_element_type=F32)
    o_ref[...] = _layer_norm(DEEPNORM_ALPHA * x_ref[...] + hmix, lng_ref[...], lnb_ref[...])


def _merge(ya, o_f, o_b, zb, yc, x, w_branch, w_out, norm_w, ln_g, ln_b, tm):
    n = x.shape[0]
    row = lambda width, col=0: pl.BlockSpec((tm, width), lambda i, col=col: (i, col))
    full = lambda shape: pl.BlockSpec(shape, lambda i: (0,) * len(shape), pipeline_mode=pl.Buffered(1))
    return pl.pallas_call(
        _merge_kernel,
        grid=(n // tm,),
        in_specs=[row(BRANCH_WIDTH), row(BRANCH_WIDTH), row(BRANCH_WIDTH),
                  row(BRANCH_WIDTH, ZB_GATE // BRANCH_WIDTH), row(BRANCH_WIDTH),
                  row(N_BRANCHES * D_MODEL, ZB_GZ // (N_BRANCHES * D_MODEL)), row(D_MODEL),
                  full((N_BRANCHES, BRANCH_WIDTH, D_MODEL)), full((D_MODEL, D_MODEL)),
                  full((1, GDN_DV)), full((1, D_MODEL)), full((1, D_MODEL))],
        out_specs=row(D_MODEL),
        out_shape=jax.ShapeDtypeStruct((n, D_MODEL), F32),
        compiler_params=_cparams(1),
        name="merge_ln",
    )(ya, o_f, o_b, zb, yc, zb, x, w_branch, w_out, norm_w.astype(F32).reshape(1, GDN_DV),
      ln_g.astype(F32).reshape(1, D_MODEL), ln_b.astype(F32).reshape(1, D_MODEL))


FFN_SPLIT = 1


def _ffn_kernel(xp_ref, x_ref, xn_ref, wup_ref, cw_ref, cb_ref, wdn_ref, lng_ref, lnb_ref, o_ref, ob_ref, *,
                tiles_per_seq):
    i = pl.program_id(0)
    tm = x_ref.shape[0]
    halo = V7X_SUBLANES
    x = x_ref[...]
    xe = jnp.concatenate([xp_ref[...], x, xn_ref[...]], axis=0).astype(BF16)
    xb = x.astype(BF16)
    rows = lax.broadcasted_iota(jnp.int32, (tm + 2 * halo, 1), 0)
    first = (i % tiles_per_seq) == 0
    last = (i % tiles_per_seq) == tiles_per_seq - 1
    keep = jnp.logical_not(((rows < halo) & first) | ((rows >= tm + halo) & last))
    fc = D_FF // FFN_SPLIT
    ext = tm + 2 * halo
    acc = None
    for c in range(FFN_SPLIT):
        ge = jnp.dot(xe, wup_ref[:, c * fc:(c + 1) * fc], preferred_element_type=F32)
        ge = jnp.where(keep, ge, 0.0)
        up = jnp.dot(xb, wup_ref[:, D_FF + c * fc:D_FF + (c + 1) * fc], preferred_element_type=F32)
        cw = cw_ref[:, c * fc:(c + 1) * fc]
        conv = (pltpu.roll(ge, 1, 0) * cw[0:1] + ge * cw[1:2] + pltpu.roll(ge, ext - 1, 0) * cw[2:3])
        gate = conv[halo:halo + tm] + cb_ref[:, c * fc:(c + 1) * fc]
        act = (gate * jax.nn.sigmoid(gate) * up).astype(BF16)
        part = jnp.dot(act, wdn_ref[c * fc:(c + 1) * fc, :], preferred_element_type=F32)
        acc = part if acc is None else acc + part
    y = _layer_norm(DEEPNORM_ALPHA * x + acc, lng_ref[...], lnb_ref[...])
    o_ref[...] = y
    ob_ref[...] = y.astype(ob_ref.dtype)


def _ffn(x, w_up, conv_w, conv_b, w_down, ln_g, ln_b, seq, tm):
    n = x.shape[0]
    halo = V7X_SUBLANES
    per = tm // halo
    nh = n // halo
    full = lambda shape: pl.BlockSpec(shape, lambda i: (0,) * len(shape), pipeline_mode=pl.Buffered(1))
    kern = functools.partial(_ffn_kernel, tiles_per_seq=seq // tm)
    return pl.pallas_call(
        kern,
        grid=(n // tm,),
        in_specs=[pl.BlockSpec((halo, D_MODEL), lambda i: (jnp.maximum(i * per - 1, 0), 0)),
                  pl.BlockSpec((tm, D_MODEL), lambda i: (i, 0)),
                  pl.BlockSpec((halo, D_MODEL), lambda i: (jnp.minimum((i + 1) * per, nh - 1), 0)),
                  full((D_MODEL, 2 * D_FF)), full((FFN_CONV, D_FF)), full((1, D_FF)),
                  full((D_FF, D_MODEL)), full((1, D_MODEL)), full((1, D_MODEL))],
        out_specs=[pl.BlockSpec((tm, D_MODEL), lambda i: (i, 0)), pl.BlockSpec((tm, D_MODEL), lambda i: (i, 0))],
        out_shape=[jax.ShapeDtypeStruct((n, D_MODEL), F32), jax.ShapeDtypeStruct((n, D_MODEL), BF16)],
        compiler_params=_cparams(1),
        name="ffn_ln",
    )(x, x, x, w_up, conv_w.astype(F32), conv_b.astype(F32).reshape(1, D_FF), w_down,
      ln_g.astype(F32).reshape(1, D_MODEL), ln_b.astype(F32).reshape(1, D_MODEL))


def _pack_w_in(w):
    widths = (512, 512, 512, GDN_QKV, 512, 8, 8, 512, 128, 128, N_BRANCHES * D_MODEL)
    offs = [0]
    for wd in widths:
        offs.append(offs[-1] + wd)
    seg = lambda i: w[:, offs[i]:offs[i + 1]]
    aq, ak, av, bqkv, bgate, ba, bb, cq, ck, cv, gz = (seg(i) for i in range(len(widths)))
    dup = lambda t: jnp.concatenate([t[:, :HEAD_DIM], t[:, :HEAD_DIM], t[:, HEAD_DIM:], t[:, HEAD_DIM:]], axis=1)
    w_a = jnp.concatenate([aq * DIFF_QSCALE, ak, av, cq, dup(ck), dup(cv)], axis=1)
    w_b = jnp.concatenate([gz, bqkv, bgate], axis=1)
    pad = jnp.zeros((w.shape[0], ZC_COLS - 2 * GDN_CHAINS), w.dtype)
    w_c = jnp.concatenate([ba, bb, pad], axis=1)
    return w_a.astype(BF16), w_b.astype(BF16), w_c.astype(BF16)


def kernel(x, rel_bias, w_in, diff_lambda, diff_subln, gdn_conv, gdn_a_log, gdn_dt_bias, gdn_norm, swa_sink,
           w_branch, w_out, ln1_g, ln1_b, ffn_up, ffn_conv, ffn_conv_b, ffn_down, ln2_g, ln2_b):
    batch, seq, d = x.shape
    assert d == D_MODEL and seq % GDN_GROUP == 0 and seq % (DIFF_QROWS * DIFF_UNROLL) == 0
    n = batch * seq
    tm = 256
    tb_diff = _diff_bias_tiles(rel_bias, seq) * LOG2_E
    tb_swa = _swa_bias_tiles(rel_bias)
    xf = x.reshape(n, d)
    xb = xf.astype(BF16)
    for l in range(DEPTH):
        w_a, w_b, w_c = _pack_w_in(w_in[l])
        za = _matmul(xb, w_a, BF16, 1024, ZA_COLS // 2, "in_proj_a")
        zb = _matmul(xb, w_b, BF16, 1024, ZB_COLS // 2, "in_proj_b")
        zc = _matmul(xb, w_c, F32, 1024, ZC_COLS, "in_proj_c")
        ya = _diff_attention(za, tb_diff, diff_lambda[l], diff_subln[l], batch, seq, l)
        yc = _swa_attention(za, tb_swa, swa_sink[l], batch, seq)
        qkvn = _gdn_conv(zb, gdn_conv[l], batch, seq)
        ab_t = zc[:, 0:2 * GDN_CHAINS].T
        gc, gd, gl, beta = _gdn_gates(ab_t, gdn_a_log[l], gdn_dt_bias[l], batch, seq)
        o_f, o_b = _gdn_fused(qkvn, gc.T, gd.T, beta.T, gc, gl, batch, seq)
        xf = _merge(ya, o_f, o_b, zb, yc, xf, w_branch[l].astype(BF16), w_out[l].astype(BF16),
                    gdn_norm[l], ln1_g[l], ln1_b[l], 2 * tm)
        xf, xb = _ffn(xf, ffn_up[l].astype(BF16), ffn_conv[l], ffn_conv_b[l], ffn_down[l].astype(BF16),
                      ln2_g[l], ln2_b[l], seq, 2 * tm)
    return xf.reshape(batch, seq, d)
```

```python
import functools
import math

import jax
import jax.numpy as jnp
from jax import lax
from jax.experimental import pallas as pl
from jax.experimental.pallas import tpu as pltpu

F32 = jnp.float32
BF16 = jnp.bfloat16

D_MODEL = 1024
DEPTH = 2
HEAD_DIM = 64
DIFF_HEADS = 4
DIFF_V = 2 * HEAD_DIM
Q_BLOCK = 128
GDN_HEADS = 4
GDN_DK = 128
GDN_DV = 128
GDN_QKV = GDN_HEADS * (2 * GDN_DK + GDN_DV)
GDN_CONV = 5
GDN_CHUNK = 64
SWA_HEADS = 8
SWA_KV_HEADS = 2
SWA_WINDOW = 128
SWA_BLOCK = 128
BRANCH_WIDTH = 512
N_BRANCHES = 3
REL_BUCKETS = 32
REL_MAX_DIST = 128
D_FF = 2816
FFN_CONV = 3
DEEPNORM_ALPHA = (2 * DEPTH) ** 0.25
LN_EPS = 1e-5
RMS_EPS = 1e-6

V7X_LANES = 128
V7X_SUBLANES = 8
V7X_VMEM_BYTES = 64 * 1024 * 1024
VMEM_LIMIT = V7X_VMEM_BYTES * 7 // 8

NEG_BIG = -1e30

ZA_AQ, ZA_AK, ZA_AV = 0, 512, 1024
ZA_CQ = 1536
ZA_CK = 2048
ZA_CV = 2304
ZA_COLS = 2560
ZB_GZ = 0
ZB_QKV = 3072
ZB_GATE = 4608
ZB_COLS = 5120
ZC_COLS = 128

GDN_GROUP = 256
GDN_CHAINS = 2 * GDN_HEADS


def _cparams(n_grid):
    return pltpu.CompilerParams(dimension_semantics=("arbitrary",) * n_grid, vmem_limit_bytes=VMEM_LIMIT)


def _t5_bucket(rel):
    nb = REL_BUCKETS // 2
    ret = jnp.where(rel > 0, nb, 0)
    n = jnp.abs(rel)
    max_exact = nb // 2
    large = max_exact + (jnp.log(jnp.maximum(n, 1).astype(jnp.float32) / max_exact)
                         / math.log(REL_MAX_DIST / max_exact) * (nb - max_exact)).astype(jnp.int32)
    large = jnp.minimum(large, nb - 1)
    return ret + jnp.where(n < max_exact, n, large)


def _table_lookup(table, bucket):
    out = jnp.zeros((table.shape[1],) + bucket.shape, F32)
    for i in range(REL_BUCKETS):
        out = jnp.where(bucket[None] == i, table[i].astype(F32).reshape((-1,) + (1,) * bucket.ndim), out)
    return out


def _diff_bias_tiles(rel_bias, seq):
    nq = seq // Q_BLOCK
    d = jnp.arange(2 * nq - 1) - (nq - 1)
    r = jnp.arange(Q_BLOCK)
    rel = d[:, None, None] * Q_BLOCK + r[None, None, :] - r[None, :, None]
    return _table_lookup(rel_bias[:, :DIFF_HEADS], _t5_bucket(rel))


def _swa_bias_tiles(rel_bias):
    kb = 3 * SWA_BLOCK
    rel = jnp.arange(kb)[None, :] - SWA_WINDOW - jnp.arange(SWA_BLOCK)[:, None]
    bias = _table_lookup(rel_bias[:, DIFF_HEADS:], _t5_bucket(rel))
    return jnp.where((jnp.abs(rel) <= SWA_WINDOW)[None], bias, NEG_BIG)


def _matmul_kernel(x_ref, w_ref, o_ref):
    o_ref[...] = jnp.dot(x_ref[...].astype(BF16), w_ref[...],
                         preferred_element_type=F32).astype(o_ref.dtype)


def _matmul(x, w, out_dtype, tm, tn, name):
    m, k = x.shape
    n = w.shape[1]
    return pl.pallas_call(
        _matmul_kernel,
        grid=(n // tn, m // tm),
        in_specs=[pl.BlockSpec((tm, k), lambda j, i: (i, 0)),
                  pl.BlockSpec((k, tn), lambda j, i: (0, j))],
        out_specs=pl.BlockSpec((tm, tn), lambda j, i: (i, j)),
        out_shape=jax.ShapeDtypeStruct((m, n), out_dtype),
        compiler_params=_cparams(2),
        name=name,
    )(x, w)


LOG2_E = math.log2(math.e)
DIFF_QSCALE = HEAD_DIM ** -0.5 * LOG2_E
DIFF_QROWS = 2 * Q_BLOCK
DIFF_UNROLL = 4


def _diff_attn_kernel(q_ref, k_ref, v_ref, tb_ref, lam_ref, subln_ref, o_ref, s_scr, vx_scr, *, nq, lam_init):
    qr = DIFF_QROWS
    vx_scr[:, 0:DIFF_V] = v_ref[...]
    vx_scr[:, DIFF_V:2 * DIFF_V] = jnp.ones((v_ref.shape[0], DIFF_V), vx_scr.dtype)
    lv = lam_ref[...]
    lam = (jnp.exp(jnp.sum(lv[0:1] * lv[1:2], axis=-1, keepdims=True))
           - jnp.exp(jnp.sum(lv[2:3] * lv[3:4], axis=-1, keepdims=True)) + lam_init)
    lane = lax.broadcasted_iota(jnp.int32, (qr, 128), 1)
    zero = jnp.zeros((qr, 128), q_ref.dtype)

    def scores(qb, scr):
        r0 = pl.multiple_of(qb * qr, qr)
        q = q_ref[pl.ds(r0, qr), :]
        qz = jnp.concatenate([jnp.where(lane < HEAD_DIM, q, zero),
                              jnp.where(lane >= HEAD_DIM, q, zero)], axis=0)
        for kp in range(nq // 2):
            kblk = k_ref[kp * qr:(kp + 1) * qr, :]
            s = lax.dot_general(qz, kblk, (((1,), (1,)), ((), ())), preferred_element_type=F32)
            for a in range(2):
                d0 = nq - 1 - (2 * qb + a) + 2 * kp
                bias = jnp.concatenate([tb_ref[0, d0], tb_ref[0, d0 + 1]], axis=1)
                for c in range(2):
                    rows = slice(c * qr + a * Q_BLOCK, c * qr + (a + 1) * Q_BLOCK)
                    scr[rows, kp * qr:(kp + 1) * qr] = s[rows] + bias

    def softmax(scr):
        s = scr[...]
        return jnp.exp2(s - jnp.max(s, axis=-1, keepdims=True)).astype(BF16)

    def finish(qb, e):
        r0 = pl.multiple_of(qb * qr, qr)
        pv = jnp.dot(e, vx_scr[...], preferred_element_type=F32)
        pv = pv[:, 0:DIFF_V] / pv[:, DIFF_V:2 * DIFF_V]
        o = pv[:qr] - lam * pv[qr:]
        o = o * lax.rsqrt(jnp.mean(o * o, axis=-1, keepdims=True) + RMS_EPS) * subln_ref[...]
        o_ref[pl.ds(r0, qr), :] = (o * (1.0 - lam_init)).astype(o_ref.dtype)

    def body(i, carry):
        blocks = [i * DIFF_UNROLL + j for j in range(DIFF_UNROLL)]
        for j, qb in enumerate(blocks):
            scores(qb, s_scr.at[j])
        probs = [softmax(s_scr.at[j]) for j in range(DIFF_UNROLL)]
        for qb, e in zip(blocks, probs):
            finish(qb, e)
        return carry

    lax.fori_loop(0, nq // (2 * DIFF_UNROLL), body, 0)


def _diff_attention(za, tb, lam_vecs, subln, batch, seq, layer_idx):
    nq = seq // Q_BLOCK
    lam_init = 0.8 - 0.6 * math.exp(-0.3 * layer_idx)
    kern = functools.partial(_diff_attn_kernel, nq=nq, lam_init=lam_init)
    qcol, kcol, vcol = ZA_AQ // 128, ZA_AK // 128, ZA_AV // 128
    return pl.pallas_call(
        kern,
        grid=(batch, DIFF_HEADS),
        in_specs=[
            pl.BlockSpec((seq, 128), lambda b, h: (b, qcol + h)),
            pl.BlockSpec((seq, 128), lambda b, h: (b, kcol + h)),
            pl.BlockSpec((seq, 128), lambda b, h: (b, vcol + h)),
            pl.BlockSpec((1, 2 * nq - 1, Q_BLOCK, Q_BLOCK), lambda b, h: (h, 0, 0, 0)),
            pl.BlockSpec((4, HEAD_DIM), lambda b, h: (0, 0)),
            pl.BlockSpec((1, DIFF_V), lambda b, h: (0, 0)),
        ],
        out_specs=pl.BlockSpec((seq, 128), lambda b, h: (b, h)),
        out_shape=jax.ShapeDtypeStruct((batch * seq, BRANCH_WIDTH), BF16),
        scratch_shapes=[pltpu.VMEM((DIFF_UNROLL, 2 * DIFF_QROWS, seq), F32),
                        pltpu.VMEM((seq, 2 * DIFF_V), BF16)],
        compiler_params=_cparams(2),
        name="diff_attention",
    )(za, za, za, tb, lam_vecs.astype(F32), subln.astype(F32).reshape(1, DIFF_V))


def _swa_kernel(q_ref, k_ref, v_ref, bias_ref, sink_ref, o_ref, kp_scr, vp_scr, *, nb):
    seq = q_ref.shape[0]
    blk = SWA_BLOCK
    pad = jnp.zeros((blk, kp_scr.shape[1]), kp_scr.dtype)
    for scr, ref in ((kp_scr, k_ref), (vp_scr, v_ref)):
        scr[0:blk] = pad
        scr[blk + seq:2 * blk + seq] = pad
        scr[blk:blk + seq] = ref[...]
    krow = lax.broadcasted_iota(jnp.int32, (3 * blk, blk), 0)
    lane = lax.broadcasted_iota(jnp.int32, (blk, 128), 1)
    low = lane < HEAD_DIM
    zero = jnp.zeros((blk, 128), q_ref.dtype)
    scale = HEAD_DIM ** -0.5
    rep = SWA_HEADS // SWA_KV_HEADS

    def body(n, carry):
        r0 = pl.multiple_of(n * blk, blk)
        q = q_ref[pl.ds(r0, blk), :]
        kw = kp_scr[pl.ds(r0, 3 * blk), :]
        vw = vp_scr[pl.ds(r0, 3 * blk), :]
        outside = ((krow < blk) & (n == 0)) | ((krow >= 2 * blk) & (n == nb - 1))
        heads = range(SWA_HEADS)
        qms = [jnp.where(low if hd % 2 == 0 else jnp.logical_not(low),
                         q[:, (hd // 2) * 128:(hd // 2 + 1) * 128] * scale, zero) for hd in heads]
        ss = [lax.dot_general(kw[:, (hd // rep) * 128:(hd // rep + 1) * 128], qms[hd], (((1,), (1,)), ((), ())),
                              preferred_element_type=F32) for hd in heads]
        ss = [jnp.where(outside, NEG_BIG, ss[hd] + bias_ref[hd]) for hd in heads]
        sinks = [sink_ref[hd][:, 0:1] for hd in heads]
        ms = [jnp.maximum(jnp.max(ss[hd], axis=0, keepdims=True), sinks[hd]) for hd in heads]
        es = [jnp.exp(ss[hd] - ms[hd]) for hd in heads]
        dens = [jnp.sum(es[hd], axis=0, keepdims=True) + jnp.exp(sinks[hd] - ms[hd]) for hd in heads]
        ps = [(es[hd] / dens[hd]).astype(BF16) for hd in heads]
        halves = [lax.dot_general(ps[hd], vw[:, (hd // rep) * 128:(hd // rep + 1) * 128], (((0,), (0,)), ((), ())),
                                  preferred_element_type=F32) for hd in heads]
        o = jnp.concatenate([jnp.where(low, halves[2 * c], halves[2 * c + 1]) for c in range(SWA_HEADS // 2)], axis=1)
        o_ref[pl.ds(r0, blk), :] = o.astype(o_ref.dtype)
        return carry

    lax.fori_loop(0, nb, body, 0, unroll=2)


def _swa_attention(za, bias, sink, batch, seq):
    nb = seq // SWA_BLOCK
    kern = functools.partial(_swa_kernel, nb=nb)
    sink_b = jnp.broadcast_to(sink.astype(F32).reshape(SWA_HEADS, 1, 1), (SWA_HEADS, 1, 128))
    kvw = 2 * SWA_KV_HEADS * HEAD_DIM
    return pl.pallas_call(
        kern,
        grid=(batch,),
        in_specs=[
            pl.BlockSpec((seq, BRANCH_WIDTH), lambda b: (b, ZA_CQ // BRANCH_WIDTH)),
            pl.BlockSpec((seq, kvw), lambda b: (b, ZA_CK // kvw)),
            pl.BlockSpec((seq, kvw), lambda b: (b, ZA_CV // kvw)),
            pl.BlockSpec((SWA_HEADS, 3 * SWA_BLOCK, SWA_BLOCK), lambda b: (0, 0, 0)),
            pl.BlockSpec((SWA_HEADS, 1, 128), lambda b: (0, 0, 0)),
        ],
        out_specs=pl.BlockSpec((seq, BRANCH_WIDTH), lambda b: (b, 0)),
        out_shape=jax.ShapeDtypeStruct((batch * seq, BRANCH_WIDTH), BF16),
        scratch_shapes=[pltpu.VMEM((seq + 2 * SWA_BLOCK, kvw), BF16),
                        pltpu.VMEM((seq + 2 * SWA_BLOCK, kvw), BF16)],
        compiler_params=_cparams(1),
        name="swa_attention",
    )(za, za, za, bias.transpose(0, 2, 1), sink_b)


def _shift_rows(x, d):
    rows = x.shape[0]
    t = lax.broadcasted_iota(jnp.int32, x.shape, 0)
    rolled = pltpu.roll(x, (-d) % rows, 0)
    return jnp.where((t + d >= 0) & (t + d < rows), rolled, 0.0)


def _gdn_conv_kernel(x_ref, w_ref, o_ref):
    j = pl.program_id(1)
    x = x_ref[...].astype(F32)
    w = w_ref[...]
    half = GDN_CONV // 2
    acc = x * w[half:half + 1]
    for tap in range(GDN_CONV):
        if tap != half:
            acc = acc + _shift_rows(x, tap - half) * w[tap:tap + 1]
    h = acc * jax.nn.sigmoid(acc)
    inv = lax.rsqrt(jnp.sum(h * h, axis=-1, keepdims=True) + RMS_EPS)
    factor = jnp.where(j < 2 * GDN_HEADS, inv, 1.0) * jnp.where(j < GDN_HEADS, GDN_DK ** -0.5, 1.0)
    o_ref[...] = h * factor


def _gdn_conv(zb, conv_w, batch, seq):
    c0 = ZB_QKV // 128
    return pl.pallas_call(
        _gdn_conv_kernel,
        grid=(batch, GDN_QKV // 128),
        in_specs=[pl.BlockSpec((seq, 128), lambda b, j: (b, c0 + j)),
                  pl.BlockSpec((GDN_CONV, 128), lambda b, j: (0, j))],
        out_specs=pl.BlockSpec((seq, 128), lambda b, j: (b, j)),
        out_shape=jax.ShapeDtypeStruct((batch * seq, GDN_QKV), F32),
        compiler_params=_cparams(2),
        name="gdn_conv",
    )(zb, conv_w.astype(F32))


def _gdn_gate_kernel(ab_ref, alog_ref, dtb_ref, gc_ref, gd_ref, gl_ref, beta_ref):
    ab = ab_ref[...]
    a = ab[0:GDN_CHAINS]
    seq = a.shape[1]
    xs = a + dtb_ref[...]
    softplus = jnp.maximum(xs, 0.0) + jnp.log1p(jnp.exp(-jnp.abs(xs)))
    g = -jnp.exp(alog_ref[...]) * softplus
    beta_ref[...] = jax.nn.sigmoid(ab[GDN_CHAINS:2 * GDN_CHAINS])
    pos = lax.broadcasted_iota(jnp.int32, g.shape, 1) % GDN_CHUNK
    cf = g
    cr = g
    s = 1
    while s < GDN_CHUNK:
        cf = cf + jnp.where(pos >= s, pltpu.roll(cf, s, 1), 0.0)
        cr = cr + jnp.where(pos < GDN_CHUNK - s, pltpu.roll(cr, seq - s, 1), 0.0)
        s *= 2
    fwd = lax.broadcasted_iota(jnp.int32, g.shape, 0) < GDN_HEADS
    gc_ref[...] = jnp.where(fwd, cf, cr)
    gd_ref[...] = jnp.where(fwd, cr, cf) - g
    gl_ref[...] = cf + cr - g


def _gdn_gates(ab_t, a_log, dt_bias, batch, seq):
    row = pl.BlockSpec((GDN_CHAINS, seq), lambda b: (0, b))
    shp = jax.ShapeDtypeStruct((GDN_CHAINS, batch * seq), F32)
    par = pl.BlockSpec((GDN_CHAINS, 1), lambda b: (0, 0))
    return pl.pallas_call(
        _gdn_gate_kernel,
        grid=(batch,),
        in_specs=[pl.BlockSpec((2 * GDN_CHAINS, seq), lambda b: (0, b)), par, par],
        out_specs=[row, row, row, row],
        out_shape=[shp, shp, shp, shp],
        compiler_params=_cparams(1),
        name="gdn_gates",
    )(ab_t, a_log.astype(F32).reshape(GDN_CHAINS, 1), dt_bias.astype(F32).reshape(GDN_CHAINS, 1))


def _hdot(a, b):
    return jnp.dot(a.astype(BF16), b.astype(BF16), preferred_element_type=F32)


def _gdn_fused_kernel(qkvf_ref, qkvb_ref, gccf_ref, gccb_ref, gdcf_ref, gdcb_ref, betaf_ref, betab_ref,
                      gcrf_ref, gcrb_ref, glf_ref, glb_ref, of_ref, ob_ref, s_scr):
    @pl.when(pl.program_id(1) == 0)
    def _():
        s_scr[...] = jnp.zeros_like(s_scr)

    g = GDN_GROUP
    c = GDN_CHUNK
    nc = g // c
    ri = lax.broadcasted_iota(jnp.int32, (g, g), 0)
    ci = lax.broadcasted_iota(jnp.int32, (g, g), 1)
    same = (ri // c) == (ci // c)
    ahead = jnp.where(same, ri - ci, -g)
    behind = jnp.where(same, ci - ri, -g)
    eye_s = (lax.broadcasted_iota(jnp.int32, (c, g), 0)
             == lax.broadcasted_iota(jnp.int32, (c, g), 1) % c).astype(F32)

    def block_diag(xs):
        return jnp.where(same, jnp.concatenate([xs] * nc, axis=0), 0.0)

    def row_blocks_sum(xd):
        out = xd[0:c]
        for i in range(1, nc):
            out = out + xd[i * c:(i + 1) * c]
        return out

    dirs = ((qkvf_ref, gccf_ref, gdcf_ref, betaf_ref, gcrf_ref, glf_ref, of_ref, ahead),
            (qkvb_ref, gccb_ref, gdcb_ref, betab_ref, gcrb_ref, glb_ref, ob_ref, behind))
    chains = [(d, h) for d in range(2) for h in range(GDN_HEADS)]
    n_ch = len(chains)
    xs_l, xd_l, p_l, rhs_l, qg_l, qk_l, kgt_l = [], [], [], [], [], [], []
    for n, (d, h) in enumerate(chains):
        qkv_ref, gcc_ref, gdc_ref, beta_ref, gcr_ref = dirs[d][0:5]
        order = dirs[d][7]
        q = qkv_ref[:, h * GDN_DK:(h + 1) * GDN_DK]
        k = qkv_ref[:, (GDN_HEADS + h) * GDN_DK:(GDN_HEADS + h + 1) * GDN_DK]
        v = qkv_ref[:, (2 * GDN_HEADS + h) * GDN_DK:(2 * GDN_HEADS + h + 1) * GDN_DK]
        kbf = k.astype(BF16)
        qk_raw = lax.dot_general(q.astype(BF16), kbf, (((1,), (1,)), ((), ())), preferred_element_type=F32)
        gcc = gcc_ref[:, n:n + 1]
        gdc = gdc_ref[:, n:n + 1]
        beta = beta_ref[:, n:n + 1]
        gcr = gcr_ref[n:n + 1, :]
        kb = k * beta
        kk = lax.dot_general(kb.astype(BF16), kbf, (((1,), (1,)), ((), ())), preferred_element_type=F32)
        decay = jnp.exp(jnp.where(order >= 0, gcc - gcr, NEG_BIG))
        xd = -jnp.where(order > 0, kk * decay, 0.0)
        qkm = qk_raw * decay
        egc = jnp.exp(gcc)
        qg_l.append((q * egc).astype(BF16))
        qk_l.append((qkm[:, 0:128] + qkm[:, 128:256]).astype(BF16))
        kgt_l.append((k * jnp.exp(gdc)).T.astype(BF16))
        xs = row_blocks_sum(xd)
        xs_l.append(xs)
        xd_l.append(xd)
        p_l.append(eye_s + xs)
        rhs_l.append(jnp.concatenate([v * beta, kb * egc], axis=1).astype(BF16))
    xs_l = [_hdot(xs_l[i], xd_l[i]) for i in range(n_ch)]
    for _ in range(4):
        r_l = [_hdot(jnp.concatenate([p_l[i], xs_l[i]], axis=0), block_diag(xs_l[i])) for i in range(n_ch)]
        p_l = [p_l[i] + r_l[i][0:c] for i in range(n_ch)]
        xs_l = [r_l[i][c:2 * c] for i in range(n_ch)]
    p_l = [p_l[i] + _hdot(p_l[i], block_diag(xs_l[i])) for i in range(n_ch)]
    sol_l = [_hdot(block_diag(p_l[i]), rhs_l[i]) for i in range(n_ch)]
    u_l = [sol[:, 0:GDN_DV] for sol in sol_l]
    w_l = [sol[:, GDN_DV:].astype(BF16) for sol in sol_l]

    zeros = jnp.zeros((c, GDN_DV), BF16)
    states = [s_scr[n] for n in range(n_ch)]
    for step in range(nc):
        tops, v2s = [], []
        for n, (d, h) in enumerate(chains):
            i = step if d == 0 else nc - 1 - step
            rows = slice(i * c, (i + 1) * c)
            sb = states[n].astype(BF16)
            top = jnp.dot(jnp.concatenate([w_l[n][rows], qg_l[n][rows]], axis=0), sb, preferred_element_type=F32)
            vb = (u_l[n][rows] - top[0:c]).astype(BF16)
            v2s.append(jnp.concatenate([vb, zeros] if i % 2 == 0 else [zeros, vb], axis=0))
            tops.append(top)
        for n, (d, h) in enumerate(chains):
            gl_ref, o_ref = dirs[d][5:7]
            i = step if d == 0 else nc - 1 - step
            rows = slice(i * c, (i + 1) * c)
            pair = slice((i // 2) * 2 * c, (i // 2 + 1) * 2 * c)
            bot = jnp.dot(jnp.concatenate([qk_l[n][rows], kgt_l[n][:, pair]], axis=0), v2s[n],
                          preferred_element_type=F32)
            o_ref[rows, h * GDN_DV:(h + 1) * GDN_DV] = tops[n][c:2 * c] + bot[0:c]
            decay = jnp.exp(gl_ref[n:n + 1, i * c:i * c + 1])
            states[n] = states[n] * decay + bot[c:c + GDN_DK]
    for n in range(n_ch):
        s_scr[n] = states[n]


def _gdn_fused(qkvn, gcc, gdc, betac, gcr, gl, batch, seq):
    ng = seq // GDN_GROUP
    half = GDN_HEADS * GDN_DV
    n = batch * seq
    fwd = lambda b, s: (b * ng + s, 0)
    bwd = lambda b, s: (b * ng + ng - 1 - s, 0)
    fwd_t = lambda b, s: (0, b * ng + s)
    bwd_t = lambda b, s: (0, b * ng + ng - 1 - s)
    both = lambda shape, f, r: [pl.BlockSpec(shape, f), pl.BlockSpec(shape, r)]
    col = (GDN_GROUP, GDN_CHAINS)
    row = (GDN_CHAINS, GDN_GROUP)
    return pl.pallas_call(
        _gdn_fused_kernel,
        grid=(batch, ng),
        in_specs=(both((GDN_GROUP, GDN_QKV), fwd, bwd) + both(col, fwd, bwd) + both(col, fwd, bwd)
                  + both(col, fwd, bwd) + both(row, fwd_t, bwd_t) + both(row, fwd_t, bwd_t)),
        out_specs=both((GDN_GROUP, half), fwd, bwd),
        out_shape=[jax.ShapeDtypeStruct((n, half), F32), jax.ShapeDtypeStruct((n, half), F32)],
        scratch_shapes=[pltpu.VMEM((GDN_CHAINS, GDN_DK, GDN_DV), F32)],
        compiler_params=_cparams(2),
        name="gdn_fused",
    )(qkvn, qkvn, gcc, gcc, gdc, gdc, betac, betac, gcr, gcr, gl, gl)


def _layer_norm(r, g, b):
    mu = jnp.mean(r, axis=-1, keepdims=True)
    var = jnp.mean(jnp.square(r - mu), axis=-1, keepdims=True)
    return (r - mu) * lax.rsqrt(var + LN_EPS) * g + b


def _merge_kernel(ya_ref, of_ref, ob_ref, gate_ref, yc_ref, gz_ref, x_ref, wb_ref, wo_ref,
                  nw_ref, lng_ref, lnb_ref, o_ref):
    segs = []
    for h in range(GDN_HEADS):
        sl = slice(h * GDN_DV, (h + 1) * GDN_DV)
        o = of_ref[:, sl] + ob_ref[:, sl]
        o = o * lax.rsqrt(jnp.mean(o * o, axis=-1, keepdims=True) + RMS_EPS) * nw_ref[...]
        gt = gate_ref[:, sl].astype(F32)
        segs.append(o * (gt * jax.nn.sigmoid(gt)))
    yb = jnp.concatenate(segs, axis=1).astype(BF16)
    merged = None
    for n, y in enumerate((ya_ref[...], yb, yc_ref[...])):
        proj = jnp.dot(y, wb_ref[n], preferred_element_type=F32)
        term = jax.nn.sigmoid(gz_ref[:, n * D_MODEL:(n + 1) * D_MODEL].astype(F32)) * proj
        merged = term if merged is None else merged + term
    hmix = jnp.dot(merged.astype(BF16), wo_ref[...], preferred_element_type=F32)
    o_ref[...] = _layer_norm(DEEPNORM_ALPHA * x_ref[...] + hmix, lng_ref[...], lnb_ref[...])


def _merge(ya, o_f, o_b, zb, yc, x, w_branch, w_out, norm_w, ln_g, ln_b, tm):
    n = x.shape[0]
    row = lambda width, col=0: pl.BlockSpec((tm, width), lambda i, col=col: (i, col))
    full = lambda shape: pl.BlockSpec(shape, lambda i: (0,) * len(shape), pipeline_mode=pl.Buffered(1))
    return pl.pallas_call(
        _merge_kernel,
        grid=(n // tm,),
        in_specs=[row(BRANCH_WIDTH), row(BRANCH_WIDTH), row(BRANCH_WIDTH),
                  row(BRANCH_WIDTH, ZB_GATE // BRANCH_WIDTH), row(BRANCH_WIDTH),
                  row(N_BRANCHES * D_MODEL, ZB_GZ // (N_BRANCHES * D_MODEL)), row(D_MODEL),
                  full((N_BRANCHES, BRANCH_WIDTH, D_MODEL)), full((D_MODEL, D_MODEL)),
                  full((1, GDN_DV)), full((1, D_MODEL)), full((1, D_MODEL))],
        out_specs=row(D_MODEL),
        out_shape=jax.ShapeDtypeStruct((n, D_MODEL), F32),
        compiler_params=_cparams(1),
        name="merge_ln",
    )(ya, o_f, o_b, zb, yc, zb, x, w_branch, w_out, norm_w.astype(F32).reshape(1, GDN_DV),
      ln_g.astype(F32).reshape(1, D_MODEL), ln_b.astype(F32).reshape(1, D_MODEL))


FFN_SPLIT = 1


def _ffn_kernel(xp_ref, x_ref, xn_ref, wup_ref, cw_ref, cb_ref, wdn_ref, lng_ref, lnb_ref, o_ref, ob_ref, *,
                tiles_per_seq):
    i = pl.program_id(0)
    tm = x_ref.shape[0]
    halo = V7X_SUBLANES
    x = x_ref[...]
    xe = jnp.concatenate([xp_ref[...], x, xn_ref[...]], axis=0).astype(BF16)
    xb = x.astype(BF16)
    rows = lax.broadcasted_iota(jnp.int32, (tm + 2 * halo, 1), 0)
    first = (i % tiles_per_seq) == 0
    last = (i % tiles_per_seq) == tiles_per_seq - 1
    keep = jnp.logical_not(((rows < halo) & first) | ((rows >= tm + halo) & last))
    fc = D_FF // FFN_SPLIT
    ext = tm + 2 * halo
    acc = None
    for c in range(FFN_SPLIT):
        ge = jnp.dot(xe, wup_ref[:, c * fc:(c + 1) * fc], preferred_element_type=F32)
        ge = jnp.where(keep, ge, 0.0)
        up = jnp.dot(xb, wup_ref[:, D_FF + c * fc:D_FF + (c + 1) * fc], preferred_element_type=F32)
        cw = cw_ref[:, c * fc:(c + 1) * fc]
        conv = (pltpu.roll(ge, 1, 0) * cw[0:1] + ge * cw[1:2] + pltpu.roll(ge, ext - 1, 0) * cw[2:3])
        gate = conv[halo:halo + tm] + cb_ref[:, c * fc:(c + 1) * fc]
        act = (gate * jax.nn.sigmoid(gate) * up).astype(BF16)
        part = jnp.dot(act, wdn_ref[c * fc:(c + 1) * fc, :], preferred_element_type=F32)
        acc = part if acc is None else acc + part
    y = _layer_norm(DEEPNORM_ALPHA * x + acc, lng_ref[...], lnb_ref[...])
    o_ref[...] = y
    ob_ref[...] = y.astype(ob_ref.dtype)


def _ffn(x, w_up, conv_w, conv_b, w_down, ln_g, ln_b, seq, tm):
    n = x.shape[0]
    halo = V7X_SUBLANES
    per = tm // halo
    nh = n // halo
    full = lambda shape: pl.BlockSpec(shape, lambda i: (0,) * len(shape), pipeline_mode=pl.Buffered(1))
    kern = functools.partial(_ffn_kernel, tiles_per_seq=seq // tm)
    return pl.pallas_call(
        kern,
        grid=(n // tm,),
        in_specs=[pl.BlockSpec((halo, D_MODEL), lambda i: (jnp.maximum(i * per - 1, 0), 0)),
                  pl.BlockSpec((tm, D_MODEL), lambda i: (i, 0)),
                  pl.BlockSpec((halo, D_MODEL), lambda i: (jnp.minimum((i + 1) * per, nh - 1), 0)),
                  full((D_MODEL, 2 * D_FF)), full((FFN_CONV, D_FF)), full((1, D_FF)),
                  full((D_FF, D_MODEL)), full((1, D_MODEL)), full((1, D_MODEL))],
        out_specs=[pl.BlockSpec((tm, D_MODEL), lambda i: (i, 0)), pl.BlockSpec((tm, D_MODEL), lambda i: (i, 0))],
        out_shape=[jax.ShapeDtypeStruct((n, D_MODEL), F32), jax.ShapeDtypeStruct((n, D_MODEL), BF16)],
        compiler_params=_cparams(1),
        name="ffn_ln",
    )(x, x, x, w_up, conv_w.astype(F32), conv_b.astype(F32).reshape(1, D_FF), w_down,
      ln_g.astype(F32).reshape(1, D_MODEL), ln_b.astype(F32).reshape(1, D_MODEL))


def _pack_w_in(w):
    widths = (512, 512, 512, GDN_QKV, 512, 8, 8, 512, 128, 128, N_BRANCHES * D_MODEL)
    offs = [0]
    for wd in widths:
        offs.append(offs[-1] + wd)
    seg = lambda i: w[:, offs[i]:offs[i + 1]]
    aq, ak, av, bqkv, bgate, ba, bb, cq, ck, cv, gz = (seg(i) for i in range(len(widths)))
    dup = lambda t: jnp.concatenate([t[:, :HEAD_DIM], t[:, :HEAD_DIM], t[:, HEAD_DIM:], t[:, HEAD_DIM:]], axis=1)
    w_a = jnp.concatenate([aq * DIFF_QSCALE, ak, av, cq, dup(ck), dup(cv)], axis=1)
    w_b = jnp.concatenate([gz, bqkv, bgate], axis=1)
    pad = jnp.zeros((w.shape[0], ZC_COLS - 2 * GDN_CHAINS), w.dtype)
    w_c = jnp.concatenate([ba, bb, pad], axis=1)
    return w_a.astype(BF16), w_b.astype(BF16), w_c.astype(BF16)


def kernel(x, rel_bias, w_in, diff_lambda, diff_subln, gdn_conv, gdn_a_log, gdn_dt_bias, gdn_norm, swa_sink,
           w_branch, w_out, ln1_g, ln1_b, ffn_up, ffn_conv, ffn_conv_b, ffn_down, ln2_g, ln2_b):
    batch, seq, d = x.shape
    assert d == D_MODEL and seq % GDN_GROUP == 0 and seq % (DIFF_QROWS * DIFF_UNROLL) == 0
    n = batch * seq
    tm = 256
    tb_diff = _diff_bias_tiles(rel_bias, seq) * LOG2_E
    tb_swa = _swa_bias_tiles(rel_bias)
    xf = x.reshape(n, d)
    xb = xf.astype(BF16)
    for l in range(DEPTH):
        w_a, w_b, w_c = _pack_w_in(w_in[l])
        za = _matmul(xb, w_a, BF16, 1024, ZA_COLS // 2, "in_proj_a")
        zb = _matmul(xb, w_b, BF16, 1024, ZB_COLS // 2, "in_proj_b")
        zc = _matmul(xb, w_c, F32, 1024, ZC_COLS, "in_proj_c")
        ya = _diff_attention(za, tb_diff, diff_lambda[l], diff_subln[l], batch, seq, l)
        yc = _swa_attention(za, tb_swa, swa_sink[l], batch, seq)
        qkvn = _gdn_conv(zb, gdn_conv[l], batch, seq)
        ab_t = zc[:, 0:2 * GDN_CHAINS].T
        gc, gd, gl, beta = _gdn_gates(ab_t, gdn_a_log[l], gdn_dt_bias[l], batch, seq)
        o_f, o_b = _gdn_fused(qkvn, gc.T, gd.T, beta.T, gc, gl, batch, seq)
        xf = _merge(ya, o_f, o_b, zb, yc, xf, w_branch[l].astype(BF16), w_out[l].astype(BF16),
                    gdn_norm[l], ln1_g[l], ln1_b[l], 2 * tm)
        xf, xb = _ffn(xf, ffn_up[l].astype(BF16), ffn_conv[l], ffn_conv_b[l], ffn_down[l].astype(BF16),
                      ln2_g[l], ln2_b[l], seq, 2 * tm)
    return xf.reshape(batch, seq, d)
```
